```python
import math
import jax
import jax.numpy as jnp
from jax import lax
import numpy as np

D_MODEL = 1024
BATCH = 32
SEQ = 256
DEPTH = 4
DEC_BATCH = 2
DEC_SEQ = 2048
PAST_LEN = 512

GRID_W = 64
N_MIXERS = 2
N_HY = (DEPTH + 1) // 2
N_RG = DEPTH // 2
HY_ORDER = 2
HY_SHORT = 3
FILTER_BANDS = 16
FILTER_EMB = 1 + 2 * FILTER_BANDS
FILTER_HIDDEN = 64
FILTER_EPS = 1e-6
MIN_DECAY = math.log(1e-2) / 0.3
MAX_DECAY = math.log(1e-2) / 1.5
D_RNN = D_MODEL
RG_BLOCK = 256
RG_NBLOCKS = D_RNN // RG_BLOCK
RG_CONV = 4
RG_C = 8.0
N_EXPERTS = 16
N_GROUPS = 4
EXPERTS_PER_GROUP = N_EXPERTS // N_GROUPS
TOP_K = 2
D_EXPERT = 1024
MOE_BLOCK = 128
NORM_EPS = 1e-6

kernel_name = 'hybrid_hyena_rglru_moe_diffusion_step'


def rms_norm(x, g):
    x32 = x.astype(jnp.float32)
    y = x32 * lax.rsqrt(jnp.mean(x32 * x32, axis=-1, keepdims=True) + NORM_EPS)
    return (y * g.astype(jnp.float32)).astype(x.dtype)


def depthwise_conv(x, w, b):
    width = w.shape[0]
    n = x.shape[-2]
    pad_l = (width - 1) // 2
    pad_r = width - 1 - pad_l
    xp = jnp.pad(x, [(0, 0)] * (x.ndim - 2) + [(pad_l, pad_r), (0, 0)])
    y = b
    for k in range(width):
        y = y + w[k] * xp[..., k:k + n, :]
    return y


def seq_conv(x, w, b, rows):
    if rows is None:
        return depthwise_conv(x, w, b)
    bsz, length, ch = x.shape
    y = depthwise_conv(x.reshape(bsz, rows, length // rows, ch), w, b)
    return y.reshape(bsz, length, ch)


def hyena_filter_spectrum(length, f_w1, f_b1, f_freq, f_w2, f_b2, f_w3):
    f32 = jnp.float32
    t = jnp.linspace(0.0, 1.0, length, dtype=f32)[:, None]
    w = (2.0 * math.pi / length) * jnp.arange(length, dtype=f32)[:, None]
    bands = jnp.linspace(1e-4, FILTER_BANDS - 1, FILTER_BANDS, dtype=f32)[None, :]
    z = jnp.concatenate([t, jnp.cos(bands * w), -jnp.sin(bands * w)], axis=-1)
    freq = f_freq.astype(f32)
    h = jnp.sin(freq * (z @ f_w1.astype(f32) + f_b1.astype(f32)))
    h = jnp.sin(freq * (h @ f_w2.astype(f32) + f_b2.astype(f32)))
    h = (h @ f_w3.astype(f32)).reshape(length, 2, HY_ORDER, D_MODEL)
    deltas = jnp.abs(jnp.linspace(MIN_DECAY, MAX_DECAY, D_MODEL, dtype=f32))
    h = h * jnp.exp(-t[:, :, None, None] * deltas)
    h_fwd = h[:, 0]
    h_bwd = h[1:, 1]
    norm = jnp.sum(jnp.abs(h_fwd), axis=0) + jnp.sum(jnp.abs(h_bwd), axis=0) + FILTER_EPS
    k = jnp.concatenate([h_fwd, jnp.zeros((1, HY_ORDER, D_MODEL), f32), h_bwd[::-1]], axis=0) / norm
    return jnp.fft.rfft(k, axis=0)


def long_conv(z, spec, skip):
    length = z.shape[1]
    zf = jnp.fft.rfft(z, n=2 * length, axis=1)
    y = jnp.fft.irfft(zf * spec[None], n=2 * length, axis=1)[:, :length]
    return y + z * skip.astype(jnp.float32)


def hyena_mixer(h, rows, w_in, b_in, conv_w, conv_b, f_w1, f_b1, f_freq, f_w2, f_b2, f_w3, f_skip,
                w_out, b_out):
    u = seq_conv(h @ w_in + b_in, conv_w, conv_b, rows).astype(jnp.float32)
    v, *gates = jnp.split(u, HY_ORDER + 1, axis=-1)
    spec = hyena_filter_spectrum(h.shape[1], f_w1, f_b1, f_freq, f_w2, f_b2, f_w3)
    z = v
    for n in range(HY_ORDER):
        z = gates[n] * long_conv(z, spec[:, n], f_skip[n])
    return z.astype(h.dtype) @ w_out + b_out


def linear_scan(a, b, h0, reverse):
    if reverse:
        a, b = jnp.flip(a, axis=1), jnp.flip(b, axis=1)
    b = b.at[:, 0].add(a[:, 0] * h0)

    def combine(left, right):
        return right[0] * left[0], right[0] * left[1] + right[1]

    _, hs = lax.associative_scan(combine, (a, b), axis=1)
    return jnp.flip(hs, axis=1) if reverse else hs


def rglru_mixer(h, h0, rows, w_in, b_in, conv_w, conv_b, wa, ba, wx, bx, lam, w_out, b_out):
    f32 = jnp.float32
    gate_br, rec = jnp.split(h @ w_in + b_in, 2, axis=-1)
    rec = seq_conv(rec, conv_w, conv_b, rows).astype(f32)
    bsz, length, _ = rec.shape
    rec_blk = rec.reshape(bsz, length, RG_NBLOCKS, RG_BLOCK)

    def block_linear(wblk, bias):
        y = jnp.einsum('blnc,ncd->blnd', rec_blk, wblk.astype(f32))
        return y.reshape(bsz, length, D_RNN) + bias.astype(f32)

    scans = []
    for d in range(2):
        r_gate = jax.nn.sigmoid(block_linear(wa[d], ba[d]))
        i_gate = jax.nn.sigmoid(block_linear(wx[d], bx[d]))
        log_a = -RG_C * r_gate * jax.nn.softplus(-lam[d].astype(f32))
        b = jnp.sqrt(-jnp.expm1(2.0 * log_a)) * (i_gate * rec)
        scans.append(linear_scan(jnp.exp(log_a), b, h0[:, d].astype(f32), reverse=(d == 1)))
    y = (scans[0] + scans[1]) * jax.nn.gelu(gate_br.astype(f32))
    return y.astype(h.dtype) @ w_out + b_out, scans[0], scans[1]


def moe_ffn(h, router_w, router_bias, w1, w3, w2):
    bsz, length, dim = h.shape
    n_tok = bsz * length
    xf = h.reshape(n_tok, dim)
    scores = jax.nn.softmax((xf @ router_w).astype(jnp.float32), axis=-1)
    sel = scores + router_bias.astype(jnp.float32)
    group_score = lax.top_k(sel.reshape(n_tok, N_GROUPS, EXPERTS_PER_GROUP), TOP_K)[0].sum(-1)
    best_group = jnp.argmax(group_score, axis=-1)
    in_group = (jnp.arange(N_EXPERTS) // EXPERTS_PER_GROUP)[None, :] == best_group[:, None]
    _, idx = lax.top_k(jnp.where(in_group, sel, -jnp.inf), TOP_K)
    gates = jnp.take_along_axis(scores, idx, axis=-1)
    gates = gates / jnp.sum(gates, axis=-1, keepdims=True)

    n_assign = n_tok * TOP_K
    flat_e = idx.reshape(-1).astype(jnp.int32)
    order = jnp.argsort(flat_e)
    sorted_e = flat_e[order]
    tok = (order // TOP_K).astype(jnp.int32)
    counts = jnp.zeros((N_EXPERTS,), jnp.int32).at[flat_e].add(1)
    padded = (counts + MOE_BLOCK - 1) // MOE_BLOCK * MOE_BLOCK
    pad_end = jnp.cumsum(padded)
    pad_start = pad_end - padded
    start = jnp.cumsum(counts) - counts
    dest = pad_start[sorted_e] + jnp.arange(n_assign, dtype=jnp.int32) - start[sorted_e]
    n_blocks = -(-n_assign // MOE_BLOCK) + N_EXPERTS
    slot_tok = jnp.full((n_blocks * MOE_BLOCK,), n_tok, jnp.int32).at[dest].set(tok)
    x_pad = jnp.concatenate([xf, jnp.zeros((1, dim), xf.dtype)], axis=0)
    xb = x_pad[slot_tok].reshape(n_blocks, MOE_BLOCK, dim)
    block_start = jnp.arange(n_blocks, dtype=jnp.int32) * MOE_BLOCK
    block_e = jnp.minimum(jnp.searchsorted(pad_end, block_start, side='right'), N_EXPERTS - 1)

    def expert_block(args):
        xblk, e = args
        return (jax.nn.silu(xblk @ w1[e]) * (xblk @ w3[e])) @ w2[e]

    yb = lax.map(expert_block, (xb, block_e)).reshape(n_blocks * MOE_BLOCK, dim)
    y_assign = yb[dest] * gates.reshape(-1)[order][:, None].astype(yb.dtype)
    out = jax.ops.segment_sum(y_assign, tok, num_segments=n_tok)
    return out.reshape(bsz, length, dim)


def run_stream(x, cond, h0_all, rows, collect_state, p):
    silu_c = jax.nn.silu(cond)
    finals = []
    for i in range(DEPTH):
        mod = (silu_c @ p['ada_w'][i] + p['ada_b'][i])[:, None, :]
        sh1, sc1, g1, sh2, sc2, g2 = jnp.split(mod, 6, axis=-1)
        h = rms_norm(x, p['norm_mix'][i]) * (1 + sc1) + sh1
        j = i // N_MIXERS
        if i % N_MIXERS == 0:
            m = hyena_mixer(h, rows, p['hy_w_in'][j], p['hy_b_in'][j], p['hy_conv_w'][j],
                            p['hy_conv_b'][j], p['hy_f_w1'][j], p['hy_f_b1'][j], p['hy_f_freq'][j],
                            p['hy_f_w2'][j], p['hy_f_b2'][j], p['hy_f_w3'][j], p['hy_f_skip'][j],
                            p['hy_w_out'][j], p['hy_b_out'][j])
        else:
            m, h_fwd, h_bwd = rglru_mixer(h, h0_all[:, j], rows, p['rg_w_in'][j], p['rg_b_in'][j],
                                          p['rg_conv_w'][j], p['rg_conv_b'][j], p['rg_wa'][j],
                                          p['rg_ba'][j], p['rg_wx'][j], p['rg_bx'][j],
                                          p['rg_lambda'][j], p['rg_w_out'][j], p['rg_b_out'][j])
            if collect_state:
                finals.append(jnp.stack([h_fwd[:, -1], h_bwd[:, 0]], axis=1))
        x = x + g1 * m
        h = rms_norm(x, p['norm_moe'][i]) * (1 + sc2) + sh2
        x = x + g2 * moe_ffn(h, p['router_w'], p['router_bias'], p['moe_w1'][i], p['moe_w3'][i],
                             p['moe_w2'][i])
    y = rms_norm(x, p['norm_final'])
    if collect_state:
        return y, jnp.stack(finals, axis=1).astype(x.dtype)
    return y


def setup_inputs(seed: int = 0) -> dict:
    key = jax.random.key(seed)
    ks = iter(jax.random.split(key, 64))
    f32 = jnp.float32
    D = D_MODEL

    def nrm(shape, scale):
        return jax.random.normal(next(ks), shape, f32) * scale

    lam_u = jax.random.uniform(next(ks), (N_RG, 2, D_RNN), f32, 0.9, 0.999)
    lam_a = lam_u ** (1.0 / RG_C)
    return {
        'x_prompt': nrm((BATCH, SEQ, D), 1.0),
        'x_sample': nrm((DEC_BATCH, DEC_SEQ, D), 1.0),
        'state_rglru': nrm((DEC_BATCH, N_RG, 2, D_RNN), 0.5),
        'c': nrm((DEC_BATCH, D), 1.0),
        'c_ctx': nrm((D,), 1.0),
        'ada_w': nrm((DEPTH, D, 6 * D), 0.5 * D ** -0.5),
        'ada_b': nrm((DEPTH, 6 * D), 0.02),
        'norm_mix': 1.0 + nrm((DEPTH, D), 0.05),
        'norm_moe': 1.0 + nrm((DEPTH, D), 0.05),
        'norm_final': 1.0 + nrm((D,), 0.05),
        'hy_w_in': nrm((N_HY, D, (HY_ORDER + 1) * D), D ** -0.5),
        'hy_b_in': nrm((N_HY, (HY_ORDER + 1) * D), 0.02),
        'hy_conv_w': nrm((N_HY, HY_SHORT, (HY_ORDER + 1) * D), HY_SHORT ** -0.5),
        'hy_conv_b': nrm((N_HY, (HY_ORDER + 1) * D), 0.02),
        'hy_f_w1': nrm((N_HY, FILTER_EMB, FILTER_HIDDEN), FILTER_EMB ** -0.5),
        'hy_f_b1': nrm((N_HY, FILTER_HIDDEN), 0.1),
        'hy_f_freq': 1.0 + nrm((N_HY, FILTER_HIDDEN), 0.1),
        'hy_f_w2': nrm((N_HY, FILTER_HIDDEN, FILTER_HIDDEN), FILTER_HIDDEN ** -0.5),
        'hy_f_b2': nrm((N_HY, FILTER_HIDDEN), 0.1),
        'hy_f_w3': nrm((N_HY, FILTER_HIDDEN, 2 * HY_ORDER * D), FILTER_HIDDEN ** -0.5),
        'hy_f_skip': nrm((N_HY, HY_ORDER, D), 0.5),
        'hy_w_out': nrm((N_HY, D, D), D ** -0.5),
        'hy_b_out': nrm((N_HY, D), 0.02),
        'rg_w_in': nrm((N_RG, D, 2 * D_RNN), D ** -0.5),
        'rg_b_in': nrm((N_RG, 2 * D_RNN), 0.02),
        'rg_conv_w': nrm((N_RG, RG_CONV, D_RNN), RG_CONV ** -0.5),
        'rg_conv_b': nrm((N_RG, D_RNN), 0.02),
        'rg_wa': nrm((N_RG, 2, RG_NBLOCKS, RG_BLOCK, RG_BLOCK), RG_BLOCK ** -0.5),
        'rg_ba': nrm((N_RG, 2, D_RNN), 0.1),
        'rg_wx': nrm((N_RG, 2, RG_NBLOCKS, RG_BLOCK, RG_BLOCK), RG_BLOCK ** -0.5),
        'rg_bx': nrm((N_RG, 2, D_RNN), 0.1),
        'rg_lambda': jnp.log(lam_a) - jnp.log1p(-lam_a),
        'rg_w_out': nrm((N_RG, D_RNN, D), D_RNN ** -0.5),
        'rg_b_out': nrm((N_RG, D), 0.02),
        'router_w': nrm((D, N_EXPERTS), D ** -0.5),
        'router_bias': nrm((N_EXPERTS,), 0.01),
        'moe_w1': nrm((DEPTH, N_EXPERTS, D, D_EXPERT), D ** -0.5),
        'moe_w3': nrm((DEPTH, N_EXPERTS, D, D_EXPERT), D ** -0.5),
        'moe_w2': nrm((DEPTH, N_EXPERTS, D_EXPERT, D), D_EXPERT ** -0.5),
    }


def reference(x_prompt, x_sample, state_rglru, c, c_ctx, ada_w, ada_b, norm_mix, norm_moe,
              norm_final, hy_w_in, hy_b_in, hy_conv_w, hy_conv_b, hy_f_w1, hy_f_b1, hy_f_freq,
              hy_f_w2, hy_f_b2, hy_f_w3, hy_f_skip, hy_w_out, hy_b_out, rg_w_in, rg_b_in,
              rg_conv_w, rg_conv_b, rg_wa, rg_ba, rg_wx, rg_bx, rg_lambda, rg_w_out, rg_b_out,
              router_w, router_bias, moe_w1, moe_w3, moe_w2):
    p = dict(ada_w=ada_w, ada_b=ada_b, norm_mix=norm_mix, norm_moe=norm_moe, norm_final=norm_final,
             hy_w_in=hy_w_in, hy_b_in=hy_b_in, hy_conv_w=hy_conv_w, hy_conv_b=hy_conv_b,
             hy_f_w1=hy_f_w1, hy_f_b1=hy_f_b1, hy_f_freq=hy_f_freq, hy_f_w2=hy_f_w2,
             hy_f_b2=hy_f_b2, hy_f_w3=hy_f_w3, hy_f_skip=hy_f_skip, hy_w_out=hy_w_out,
             hy_b_out=hy_b_out, rg_w_in=rg_w_in, rg_b_in=rg_b_in, rg_conv_w=rg_conv_w,
             rg_conv_b=rg_conv_b, rg_wa=rg_wa, rg_ba=rg_ba, rg_wx=rg_wx, rg_bx=rg_bx,
             rg_lambda=rg_lambda, rg_w_out=rg_w_out, rg_b_out=rg_b_out, router_w=router_w,
             router_bias=router_bias, moe_w1=moe_w1, moe_w3=moe_w3, moe_w2=moe_w2)
    ctx_h0 = jnp.zeros((x_prompt.shape[0], N_RG, 2, D_RNN), x_prompt.dtype)
    y_prompt, new_state_rglru = run_stream(x_prompt, c_ctx[None, :], ctx_h0, None, True, p)
    rows = x_sample.shape[1] // GRID_W
    y_sample = run_stream(x_sample, c, state_rglru, rows, False, p)
    return (y_prompt, y_sample, new_state_rglru)
```

```python
import functools
import math

import numpy as np
import jax
import jax.numpy as jnp
from jax import lax
from jax.experimental import pallas as pl
from jax.experimental.pallas import tpu as pltpu

F32 = jnp.float32
BF16 = jnp.bfloat16

GRID_W = 64
FILTER_BANDS = 16
FILTER_EPS = 1e-6
MIN_DECAY = math.log(1e-2) / 0.3
MAX_DECAY = math.log(1e-2) / 1.5
RG_C = 8.0
N_GROUPS = 4
TOP_K = 2
NORM_EPS = 1e-6

V7X_LANES = 128
V7X_SUBLANES = 8
V7X_VMEM_BYTES = 64 * 1024 * 1024

TOKEN_TILE = 512
COL_TILE = 512
MOE_ROWS = 256
DFT_ROWS = 512
DFT_SPLIT = 64
RG_COLS = 256


def _params(sem, vmem_mb):
    return pltpu.CompilerParams(dimension_semantics=sem,
                                vmem_limit_bytes=vmem_mb * 1024 * 1024)


def _dot(a, b):
    return jnp.dot(a, b, preferred_element_type=F32)


def _split(x):
    hi = x.astype(BF16)
    lo = (x - hi.astype(F32)).astype(BF16)
    return hi, lo


def _dot3(a, b):
    ah, al = _split(a)
    bh, bl = _split(b)
    return _dot(ah, bh) + (_dot(ah, bl) + _dot(al, bh))


def _dot3_nt(a, b):
    dn = (((1,), (1,)), ((), ()))
    d = lambda x, y: lax.dot_general(x, y, dn, preferred_element_type=F32)
    ah, al = _split(a)
    bh, bl = _split(b)
    return d(ah, bh) + (d(ah, bl) + d(al, bh))


def _sigmoid(x):
    return 1.0 / (1.0 + jnp.exp(-x))


def _norm_mod(x, g, shift, scale):
    ms = jnp.mean(x * x, axis=-1, keepdims=True)
    return (x * lax.rsqrt(ms + NORM_EPS) * g) * (1.0 + scale) + shift


def _ada_kernel(c_ref, w_ref, b_ref, o_ref):
    c = c_ref[...]
    o_ref[...] = _dot3(c * _sigmoid(c), w_ref[...]) + b_ref[...]


def _ada_modulation(cond, ada_w, ada_b):
    depth, d, n = ada_w.shape
    tn = n // 4
    out = pl.pallas_call(
        _ada_kernel,
        out_shape=jax.ShapeDtypeStruct((depth, cond.shape[0], n), F32),
        grid=(depth, n // tn),
        in_specs=[pl.BlockSpec(cond.shape, lambda i, j: (0, 0)),
                  pl.BlockSpec((None, d, tn), lambda i, j: (i, 0, j)),
                  pl.BlockSpec((None, 1, tn), lambda i, j: (i, 0, j))],
        out_specs=pl.BlockSpec((None, cond.shape[0], tn), lambda i, j: (i, 0, j)),
        compiler_params=_params(("parallel", "parallel"), 40),
        name="ada_modulation",
    )(cond, ada_w, ada_b.reshape(depth, 1, n))
    return out.reshape(depth, cond.shape[0], 6, d)


def _inproj_kernel(x_ref, mod_ref, g_ref, w_ref, b_ref, cw_ref, cb_ref, o_ref, h_scr, *,
                   n_ctx_tiles, ctx_len, lat_len, width, plain_col_tiles):
    i = pl.program_id(0)
    j = pl.program_id(1)

    @pl.when(j == 0)
    def _():
        h = _norm_mod(x_ref[...], g_ref[...], mod_ref[0:1, :], mod_ref[1:2, :])
        h_scr[...] = h.astype(BF16)

    u = _dot(h_scr[...], w_ref[...]) + b_ref[...]

    def conv():
        rows = u.shape[0]
        seg_mask = jnp.where(i < n_ctx_tiles, ctx_len - 1, lat_len - 1)
        pos = lax.broadcasted_iota(jnp.int32, (rows, 1), 0) & seg_mask
        pad_l = (width - 1) // 2
        acc = jnp.zeros_like(u) + cb_ref[...]
        for k in range(width):
            off = k - pad_l
            if off == 0:
                term = u
            else:
                shifted = pltpu.roll(u, (-off) % rows, axis=0)
                ok = (pos + off >= 0) & (pos + off <= seg_mask)
                term = jnp.where(ok, shifted, 0.0)
            acc = acc + cw_ref[k:k + 1, :] * term
        o_ref[...] = acc

    if plain_col_tiles == 0:
        conv()
    else:
        pl.when(j >= plain_col_tiles)(conv)

        @pl.when(j < plain_col_tiles)
        def _():
            o_ref[...] = u


def _inproj(x, mod, g, w, b, conv_w, conv_b, *, n_ctx_tiles, ctx_len, lat_len, plain_cols):
    t, d = x.shape
    n = w.shape[1]
    tm, tn = TOKEN_TILE, COL_TILE
    width = conv_w.shape[0]
    plain_tiles = plain_cols // tn
    lat_tiles = (t // tm - n_ctx_tiles) // (mod.shape[0] - 1)

    def cond_of(i):
        return jnp.where(i < n_ctx_tiles, 0, 1 + (i - n_ctx_tiles) // lat_tiles)

    kern = functools.partial(_inproj_kernel, n_ctx_tiles=n_ctx_tiles, ctx_len=ctx_len,
                             lat_len=lat_len, width=width, plain_col_tiles=plain_tiles)
    return pl.pallas_call(
        kern,
        out_shape=jax.ShapeDtypeStruct((t, n), F32),
        grid=(t // tm, n // tn),
        in_specs=[pl.BlockSpec((tm, d), lambda i, j: (i, 0)),
                  pl.BlockSpec((None, 6, d), lambda i, j: (cond_of(i), 0, 0)),
                  pl.BlockSpec((1, d), lambda i, j: (0, 0)),
                  pl.BlockSpec((d, tn), lambda i, j: (0, j)),
                  pl.BlockSpec((1, tn), lambda i, j: (0, j)),
                  pl.BlockSpec((width, tn), lambda i, j: (0, jnp.maximum(j - plain_tiles, 0))),
                  pl.BlockSpec((1, tn), lambda i, j: (0, jnp.maximum(j - plain_tiles, 0)))],
        out_specs=pl.BlockSpec((tm, tn), lambda i, j: (i, j)),
        scratch_shapes=[pltpu.VMEM((tm, d), BF16)],
        compiler_params=_params(("parallel", "arbitrary"), 32),
        name="norm_inproj_conv",
    )(x, mod, g.reshape(1, d), w, b.reshape(1, n), conv_w, conv_b.reshape(1, -1))


def _outproj_kernel(y_ref, w_ref, b_ref, mod_ref, x_ref, o_ref):
    m = _dot(y_ref[...], w_ref[...]) + b_ref[...]
    o_ref[...] = x_ref[...] + mod_ref[2:3, :] * m


def _outproj(y, w, b, mod, x, *, n_ctx_tiles):
    t, d = x.shape
    tm = TOKEN_TILE
    lat_tiles = (t // tm - n_ctx_tiles) // (mod.shape[0] - 1)

    def cond_of(i):
        return jnp.where(i < n_ctx_tiles, 0, 1 + (i - n_ctx_tiles) // lat_tiles)

    return pl.pallas_call(
        _outproj_kernel,
        out_shape=jax.ShapeDtypeStruct((t, d), F32),
        grid=(t // tm,),
        in_specs=[pl.BlockSpec((tm, y.shape[1]), lambda i: (i, 0)),
                  pl.BlockSpec(w.shape, lambda i: (0, 0)),
                  pl.BlockSpec((1, d), lambda i: (0, 0)),
                  pl.BlockSpec((None, 6, d), lambda i: (cond_of(i), 0, 0)),
                  pl.BlockSpec((tm, d), lambda i: (i, 0))],
        out_specs=pl.BlockSpec((tm, d), lambda i: (i, 0)),
        input_output_aliases={4: 0},
        compiler_params=_params(("parallel",), 32),
        name="outproj_residual",
    )(y, w, b.reshape(1, d), mod, x)


def _filter_kernel(z_ref, w1_ref, b1_ref, fr_ref, w2_ref, b2_ref, w3a_ref, w3b_ref, w3c_ref,
                   w3d_ref, t_ref, dl_ref, o_ref, *, orders):
    fr = fr_ref[...]
    h = jnp.sin(fr * (_dot3(z_ref[...], w1_ref[...]) + b1_ref[...]))
    h = jnp.sin(fr * (_dot3(h, w2_ref[...]) + b2_ref[...]))
    window = jnp.exp(-t_ref[...] * dl_ref[...])
    row = lax.broadcasted_iota(jnp.int32, (h.shape[0], 1), 0)
    w3 = ((w3a_ref, w3b_ref), (w3c_ref, w3d_ref))
    for o in range(orders):
        h_fwd = _dot3(h, w3[0][o][...]) * window
        h_bwd = jnp.where(row == 0, 0.0, _dot3(h, w3[1][o][...]) * window)
        norm = (jnp.sum(jnp.abs(h_fwd), axis=0, keepdims=True)
                + jnp.sum(jnp.abs(h_bwd), axis=0, keepdims=True) + FILTER_EPS)
        o_ref[2 * o] = h_fwd / norm
        o_ref[2 * o + 1] = h_bwd / norm


def _hyena_filter(length, f_w1, f_b1, f_freq, f_w2, f_b2, f_w3, d):
    emb, hid = f_w1.shape
    orders = f_w3.shape[1] // (2 * d)
    t = jnp.linspace(0.0, 1.0, length, dtype=F32)[:, None]
    w = (2.0 * math.pi / length) * jnp.arange(length, dtype=F32)[:, None]
    bands = jnp.linspace(1e-4, FILTER_BANDS - 1, FILTER_BANDS, dtype=F32)[None, :]
    z = jnp.concatenate([t, jnp.cos(bands * w), -jnp.sin(bands * w)], axis=-1)
    emb_pad = V7X_LANES
    z = jnp.pad(z, ((0, 0), (0, emb_pad - emb)))
    w1 = jnp.pad(f_w1, ((0, emb_pad - emb), (0, 0)))
    deltas = jnp.abs(jnp.linspace(MIN_DECAY, MAX_DECAY, d, dtype=F32))[None, :]
    td = 256
    nd = d // td
    full = lambda shape: pl.BlockSpec(shape, lambda j: (0,) * len(shape))
    w3_spec = lambda side, o: pl.BlockSpec((hid, td), lambda j: (0, (side * orders + o) * nd + j))
    assert orders == 2
    return pl.pallas_call(
        functools.partial(_filter_kernel, orders=orders),
        out_shape=jax.ShapeDtypeStruct((2 * orders, length, d), F32),
        grid=(nd,),
        in_specs=[full((length, emb_pad)), full((emb_pad, hid)), full((1, hid)), full((1, hid)),
                  full((hid, hid)), full((1, hid)),
                  w3_spec(0, 0), w3_spec(0, 1), w3_spec(1, 0), w3_spec(1, 1),
                  full((length, 1)), pl.BlockSpec((1, td), lambda j: (0, j))],
        out_specs=pl.BlockSpec((2 * orders, length, td), lambda j: (0, 0, j)),
        compiler_params=_params(("parallel",), 48),
        name="hyena_filter",
    )(z, w1, f_b1.reshape(1, hid), f_freq.reshape(1, hid), f_w2, f_b2.reshape(1, hid),
      f_w3, f_w3, f_w3, f_w3, t, deltas)


def _dft_matrices(length):
    n = 2 * length
    f0n, f1n = DFT_SPLIT, length // DFT_SPLIT
    tt = np.arange(length, dtype=np.int64)[None, :]
    ang_a = 2.0 * np.pi * ((DFT_SPLIT * np.arange(f1n, dtype=np.int64)[:, None] * tt) % n) / n
    ang_b = 2.0 * np.pi * ((np.arange(f0n, dtype=np.int64)[:, None] * tt) % n) / n
    ca, sa = (jnp.asarray(f(ang_a), F32)[:, None, :] for f in (np.cos, np.sin))
    cb, sb = (jnp.asarray(f(ang_b), F32)[None, :, :] for f in (np.cos, np.sin))
    cos_m = (ca * cb - sa * sb).reshape(length, length)
    sin_m = (sa * cb + ca * sb).reshape(length, length)
    nyq = jnp.asarray(1.0 - 2.0 * (np.arange(length) % 2), F32)[None, :]
    f_is0 = (jnp.arange(length) == 0)[:, None]
    im_m = jnp.where(f_is0, nyq, -sin_m)
    th = DFT_ROWS // 2
    fwd = jnp.stack([cos_m.reshape(length // th, th, length),
                     im_m.reshape(length // th, th, length)], axis=1).reshape(n, length)
    scale = jnp.where(f_is0, 1.0 / n, 2.0 / n)
    inv = jnp.stack([(cos_m * scale).reshape(length // th, th, length),
                     (im_m * scale).reshape(length // th, th, length)], axis=1).reshape(n, length).T
    return fwd.astype(BF16), inv.astype(BF16)


def _spec_kernel(a_ref, hf_ref, hb_ref, o_ref, hf_scr, hb_scr):
    m = pl.program_id(2)

    @pl.when(m == 0)
    def _():
        hf_scr[...] = hf_ref[...].astype(BF16)
        hb_scr[...] = hb_ref[...].astype(BF16)

    a = a_ref[...]
    ff = _dot(a, hf_scr[...])
    fb = _dot(a, hb_scr[...])
    th = ff.shape[0] // 2
    row = lax.broadcasted_iota(jnp.int32, (th, 1), 0)
    dc = (row == 0) & (m == 0)
    o_ref[0:th, :] = ff[:th] + fb[:th]
    o_ref[th:, :] = jnp.where(dc, ff[th:] + fb[th:], ff[th:] - fb[th:])


def _filter_spectrum(fwd, filt):
    n2, length, d = filt.shape
    orders = n2 // 2
    n = 2 * length
    tm, tc = DFT_ROWS, 512
    filt2 = filt.reshape(n2 * length, d)
    return pl.pallas_call(
        _spec_kernel,
        out_shape=jax.ShapeDtypeStruct((orders * n, d), F32),
        grid=(orders, d // tc, n // tm),
        in_specs=[pl.BlockSpec((tm, length), lambda o, c, m: (m, 0)),
                  pl.BlockSpec((length, tc), lambda o, c, m: (2 * o, c)),
                  pl.BlockSpec((length, tc), lambda o, c, m: (2 * o + 1, c))],
        out_specs=pl.BlockSpec((tm, tc), lambda o, c, m: (o * (n // tm) + m, c)),
        scratch_shapes=[pltpu.VMEM((length, tc), BF16), pltpu.VMEM((length, tc), BF16)],
        compiler_params=_params(("parallel", "parallel", "arbitrary"), 48),
        name="hyena_filter_spectrum",
    )(fwd, filt2, filt2)


def _dft_fwd_kernel(a_ref, z_ref, k_ref, o_ref, z_scr):
    m = pl.program_id(1)

    @pl.when(m == 0)
    def _():
        z_scr[...] = z_ref[...].astype(BF16)

    acc = _dot(a_ref[...], z_scr[...])
    th = acc.shape[0] // 2
    xr, xi = acc[:th], acc[th:]
    kr, ki = k_ref[0:th, :], k_ref[th:, :]
    row = lax.broadcasted_iota(jnp.int32, (th, 1), 0)
    dc = (row == 0) & (m == 0)
    o_ref[0:th, :] = (xr * kr - jnp.where(dc, 0.0, xi * ki)).astype(o_ref.dtype)
    o_ref[th:, :] = jnp.where(dc, xi * ki, xr * ki + xi * kr).astype(o_ref.dtype)


def _dft_forward(fwd, src, spec, *, nb, length, row_off, col_blk, order):
    n = 2 * length
    d = spec.shape[1]
    tm = DFT_ROWS
    nt = n // tm
    return pl.pallas_call(
        _dft_fwd_kernel,
        out_shape=jax.ShapeDtypeStruct((nb * n, d), BF16),
        grid=(nb, nt),
        in_specs=[pl.BlockSpec((tm, length), lambda b, m: (m, 0)),
                  pl.BlockSpec((length, d), lambda b, m: (row_off + b, col_blk)),
                  pl.BlockSpec((tm, d), lambda b, m: (order * nt + m, 0))],
        out_specs=pl.BlockSpec((tm, d), lambda b, m: (b * nt + m, 0)),
        scratch_shapes=[pltpu.VMEM((length, d), BF16)],
        compiler_params=_params(("parallel", "arbitrary"), 48),
        name="hyena_dft_forward",
    )(fwd, src, spec)


def _dft_inv_kernel(a_ref, p_ref, v_ref, g_ref, s_ref, o_ref):
    y = _dot(a_ref[...], p_ref[...])
    v = v_ref[...]
    o_ref[...] = (g_ref[...] * (y + v * s_ref[...])).astype(o_ref.dtype)


def _dft_inverse(inv, prod, vsrc, gsrc, skip, *, nb, length, v_off, v_col, g_off, g_col,
                 out_dtype):
    n = 2 * length
    d = prod.shape[1]
    tm = min(DFT_ROWS, length)
    nt = length // tm
    return pl.pallas_call(
        _dft_inv_kernel,
        out_shape=jax.ShapeDtypeStruct((nb * length, d), out_dtype),
        grid=(nb, nt),
        in_specs=[pl.BlockSpec((tm, n), lambda b, m: (m, 0)),
                  pl.BlockSpec((n, d), lambda b, m: (b, 0)),
                  pl.BlockSpec((tm, d), lambda b, m: ((v_off + b) * nt + m, v_col)),
                  pl.BlockSpec((tm, d), lambda b, m: ((g_off + b) * nt + m, g_col)),
                  pl.BlockSpec((1, d), lambda b, m: (0, 0))],
        out_specs=pl.BlockSpec((tm, d), lambda b, m: (b * nt + m, 0)),
        compiler_params=_params(("parallel", "arbitrary"), 48),
        name="hyena_dft_inverse",
    )(inv, prod, vsrc, gsrc, skip.reshape(1, d))


def _hyena_stream(u, row_off, nb, length, filt_w, f_skip, d):
    fwd, inv = _dft_matrices(length)
    filt = _hyena_filter(length, *filt_w, d)
    spec = _filter_spectrum(fwd, filt)
    p1 = _dft_forward(fwd, u, spec, nb=nb, length=length, row_off=row_off, col_blk=0, order=0)
    z1 = _dft_inverse(inv, p1, u, u, f_skip[0], nb=nb, length=length, v_off=row_off, v_col=0,
                      g_off=row_off, g_col=1, out_dtype=F32)
    p2 = _dft_forward(fwd, z1, spec, nb=nb, length=length, row_off=0, col_blk=0, order=1)
    return _dft_inverse(inv, p2, z1, u, f_skip[1], nb=nb, length=length, v_off=0, v_col=0,
                        g_off=row_off, g_col=2, out_dtype=BF16)


def _gelu_tanh(x):
    return 0.5 * x * (1.0 + jnp.tanh(math.sqrt(2.0 / math.pi) * (x + 0.044715 * (x * x * x))))


def _log1p(e):
    u = 1.0 + e
    d = u - 1.0
    return jnp.where(d == 0.0, e, jnp.log(u) * (e / jnp.where(d == 0.0, 1.0, d)))


def _rglru_kernel(gate_ref, rec_ref, wa_ref, wx_ref, ba_ref, bx_ref, lam_ref, h0_ref,
                  y_ref, st_ref, a_scr, b_scr):
    length, cols = rec_ref.shape
    groups = length // V7X_SUBLANES
    rec = rec_ref[...]
    rec16 = rec.astype(BF16)
    pos = lax.broadcasted_iota(jnp.int32, (groups, V7X_SUBLANES, cols), 1)

    for d in range(2):
        r_gate = _sigmoid(_dot(rec16, wa_ref[d]) + ba_ref[d])
        i_gate = _sigmoid(_dot(rec16, wx_ref[d]) + bx_ref[d])
        nlam = -lam_ref[d]
        softplus = jnp.maximum(nlam, 0.0) + _log1p(jnp.exp(-jnp.abs(nlam)))
        log_a = (-RG_C * softplus) * r_gate
        a = jnp.exp(log_a).reshape(groups, V7X_SUBLANES, cols)
        th = jnp.tanh(log_a)
        b = (jnp.sqrt(-2.0 * th / (1.0 - th)) * (i_gate * rec)).reshape(
            groups, V7X_SUBLANES, cols)
        for s in (1, 2, 4):
            if d == 0:
                a_sh = pltpu.roll(a, s, axis=1)
                b_sh = pltpu.roll(b, s, axis=1)
                live = pos >= s
            else:
                a_sh = pltpu.roll(a, V7X_SUBLANES - s, axis=1)
                b_sh = pltpu.roll(b, V7X_SUBLANES - s, axis=1)
                live = pos < V7X_SUBLANES - s
            b = jnp.where(live, a * b_sh, 0.0) + b
            a = jnp.where(live, a * a_sh, a)
        a_scr[d] = a.reshape(length, cols)
        b_scr[d] = b.reshape(length, cols)

    def step(g, carry):
        cf, cb = carry
        rf = pl.multiple_of(g * V7X_SUBLANES, V7X_SUBLANES)
        rb = pl.multiple_of((groups - 1 - g) * V7X_SUBLANES, V7X_SUBLANES)
        hf = a_scr[0, pl.ds(rf, V7X_SUBLANES), :] * cf + b_scr[0, pl.ds(rf, V7X_SUBLANES), :]
        hb = a_scr[1, pl.ds(rb, V7X_SUBLANES), :] * cb + b_scr[1, pl.ds(rb, V7X_SUBLANES), :]
        b_scr[0, pl.ds(rf, V7X_SUBLANES), :] = hf
        b_scr[1, pl.ds(rb, V7X_SUBLANES), :] = hb
        cf = jnp.broadcast_to(hf[V7X_SUBLANES - 1:V7X_SUBLANES, :], hf.shape)
        cb = jnp.broadcast_to(hb[0:1, :], hb.shape)
        return cf, cb

    init = (jnp.broadcast_to(h0_ref[0], (V7X_SUBLANES, cols)),
            jnp.broadcast_to(h0_ref[1], (V7X_SUBLANES, cols)))
    cf, cb = lax.fori_loop(0, groups, step, init)
    st_ref[0] = cf[0:1, :]
    st_ref[1] = cb[0:1, :]
    y_ref[...] = ((b_scr[0] + b_scr[1]) * _gelu_tanh(gate_ref[...])).astype(y_ref.dtype)


def _rglru_stream(u, h0, row_off, nb, length, wa, wx, ba, bx, lam):
    dr = u.shape[1] // 2
    tc = RG_COLS
    nc = dr // tc
    vec = lambda a: a.reshape(2, 1, dr)
    vec_spec = pl.BlockSpec((2, 1, tc), lambda b, c: (0, 0, c))
    w_spec = pl.BlockSpec((2, None, tc, tc), lambda b, c: (0, c, 0, 0))
    y, st = pl.pallas_call(
        _rglru_kernel,
        out_shape=(jax.ShapeDtypeStruct((nb * length, dr), BF16),
                   jax.ShapeDtypeStruct((nb, 2, 1, dr), F32)),
        grid=(nb, nc),
        in_specs=[pl.BlockSpec((length, tc), lambda b, c: (row_off + b, c)),
                  pl.BlockSpec((length, tc), lambda b, c: (row_off + b, nc + c)),
                  w_spec, w_spec, vec_spec, vec_spec, vec_spec,
                  pl.BlockSpec((None, 2, 1, tc), lambda b, c: (b, 0, 0, c))],
        out_specs=(pl.BlockSpec((length, tc), lambda b, c: (b, c)),
                   pl.BlockSpec((None, 2, 1, tc), lambda b, c: (b, 0, 0, c))),
        scratch_shapes=[pltpu.VMEM((2, length, tc), F32), pltpu.VMEM((2, length, tc), F32)],
        compiler_params=_params(("parallel", "parallel"), 48),
        name="rglru_scan",
    )(u, u, wa, wx, vec(ba), vec(bx), vec(lam), h0.reshape(nb, 2, 1, dr))
    return y, st.reshape(nb, 2, dr)


def _router_kernel(x_ref, mod_ref, g_ref, rwt_ref, bias_ref, h_ref, idx_ref, gate_ref,
                   rank_ref, cnt_ref, tri_scr, carry_scr, *, n_experts):
    i = pl.program_id(0)
    tm = x_ref.shape[0]
    per_group = n_experts // N_GROUPS

    @pl.when(i == 0)
    def _():
        r = lax.broadcasted_iota(jnp.int32, (tm, tm), 0)
        c = lax.broadcasted_iota(jnp.int32, (tm, tm), 1)
        tri_scr[...] = jnp.where(r < c, 1.0, 0.0).astype(BF16)
        carry_scr[...] = jnp.zeros_like(carry_scr)

    h = _norm_mod(x_ref[...], g_ref[...], mod_ref[3:4, :], mod_ref[4:5, :])
    h_ref[...] = h
    logits = _dot3_nt(rwt_ref[...], h)
    p = jnp.exp(logits - jnp.max(logits, axis=0, keepdims=True))
    scores = p / jnp.sum(p, axis=0, keepdims=True)
    sel = scores + bias_ref[...]
    rows = [sel[e:e + 1, :] for e in range(n_experts)]

    best_val = None
    for gi in range(N_GROUPS):
        v = rows[gi * per_group:(gi + 1) * per_group]
        pair = None
        for a in range(per_group):
            for b in range(a + 1, per_group):
                s = v[a] + v[b]
                pair = s if pair is None else jnp.maximum(pair, s)
        if best_val is None:
            best_val, best_grp = pair, jnp.zeros_like(pair, dtype=jnp.int32)
        else:
            take = pair > best_val
            best_val = jnp.where(take, pair, best_val)
            best_grp = jnp.where(take, gi, best_grp)

    neg = jnp.float32(-jnp.inf)
    masked = [jnp.where(best_grp == e // per_group, rows[e], neg) for e in range(n_experts)]

    def argmax_first(vals):
        bv, bi = vals[0], jnp.zeros_like(best_grp)
        for e in range(1, n_experts):
            take = vals[e] > bv
            bv = jnp.where(take, vals[e], bv)
            bi = jnp.where(take, e, bi)
        return bi

    idx0 = argmax_first(masked)
    idx1 = argmax_first([jnp.where(idx0 == e, neg, masked[e]) for e in range(n_experts)])

    e_iota = lax.broadcasted_iota(jnp.int32, (n_experts, tm), 0)
    hit0 = e_iota == idx0
    hit1 = e_iota == idx1
    g0 = jnp.sum(jnp.where(hit0, scores, 0.0), axis=0, keepdims=True)
    g1 = jnp.sum(jnp.where(hit1, scores, 0.0), axis=0, keepdims=True)
    gsum = g0 + g1
    onehot = jnp.where(hit0 | hit1, 1.0, 0.0)
    before = _dot(onehot.astype(BF16), tri_scr[...]) + carry_scr[:, 0:1]
    r0 = jnp.sum(jnp.where(hit0, before, 0.0), axis=0, keepdims=True)
    r1 = jnp.sum(jnp.where(hit1, before, 0.0), axis=0, keepdims=True)
    idx_ref[0:1, :] = idx0
    idx_ref[1:2, :] = idx1
    gate_ref[0:1, :] = g0 / gsum
    gate_ref[1:2, :] = g1 / gsum
    rank_ref[0:1, :] = r0.astype(jnp.int32)
    rank_ref[1:2, :] = r1.astype(jnp.int32)
    carry_scr[...] = carry_scr[...] + jnp.sum(onehot, axis=1, keepdims=True)
    cnt_ref[...] = carry_scr[...]


def _router(x, mod, g, router_w, router_bias, *, n_ctx_tiles):
    t, d = x.shape
    ne = router_w.shape[1]
    tm = TOKEN_TILE
    lat_tiles = (t // tm - n_ctx_tiles) // (mod.shape[0] - 1)

    def cond_of(i):
        return jnp.where(i < n_ctx_tiles, 0, 1 + (i - n_ctx_tiles) // lat_tiles)

    row2 = pl.BlockSpec((TOP_K, tm), lambda i: (0, i))
    return pl.pallas_call(
        functools.partial(_router_kernel, n_experts=ne),
        out_shape=(jax.ShapeDtypeStruct((t, d), F32),
                   jax.ShapeDtypeStruct((TOP_K, t), jnp.int32),
                   jax.ShapeDtypeStruct((TOP_K, t), F32),
                   jax.ShapeDtypeStruct((TOP_K, t), jnp.int32),
                   jax.ShapeDtypeStruct((ne, V7X_LANES), F32)),
        grid=(t // tm,),
        in_specs=[pl.BlockSpec((tm, d), lambda i: (i, 0)),
                  pl.BlockSpec((None, 6, d), lambda i: (cond_of(i), 0, 0)),
                  pl.BlockSpec((1, d), lambda i: (0, 0)),
                  pl.BlockSpec((ne, d), lambda i: (0, 0)),
                  pl.BlockSpec((ne, 1), lambda i: (0, 0))],
        out_specs=(pl.BlockSpec((tm, d), lambda i: (i, 0)), row2, row2, row2,
                   pl.BlockSpec((ne, V7X_LANES), lambda i: (0, 0))),
        scratch_shapes=[pltpu.VMEM((tm, tm), BF16), pltpu.VMEM((ne, V7X_LANES), F32)],
        compiler_params=_params(("arbitrary",), 32),
        name="moe_router",
    )(x, mod, g.reshape(1, d), router_w.T, router_bias.reshape(ne, 1))


def _row_copy(src, src_row, dst, dst_row, sem):
    return pltpu.make_async_copy(src.at[pl.ds(src_row, 1)], dst.at[pl.ds(dst_row, 1)], sem)


def _dispatch_kernel(dest_ref, h_hbm, xs_in, xs_out, sem):
    del xs_in
    tm = dest_ref.shape[0] // TOP_K
    base = pl.program_id(0) * tm

    def issue(t, c):
        for k in range(TOP_K):
            _row_copy(h_hbm, base + t, xs_out, dest_ref[TOP_K * t + k], sem).start()
        return c

    def drain(t, c):
        for k in range(TOP_K):
            _row_copy(h_hbm, 0, xs_out, 0, sem).wait()
        return c

    lax.fori_loop(0, tm, issue, 0)
    lax.fori_loop(0, tm, drain, 0)


def _dispatch(h, dest, n_slots):
    t, d = h.shape
    tm = TOKEN_TILE
    return pl.pallas_call(
        _dispatch_kernel,
        out_shape=jax.ShapeDtypeStruct((n_slots, d), h.dtype),
        grid=(t // tm,),
        in_specs=[pl.BlockSpec((TOP_K * tm,), lambda i: (i,), memory_space=pltpu.SMEM),
                  pl.BlockSpec(memory_space=pl.ANY),
                  pl.BlockSpec(memory_space=pl.ANY)],
        out_specs=pl.BlockSpec(memory_space=pl.ANY),
        scratch_shapes=[pltpu.SemaphoreType.DMA],
        input_output_aliases={2: 0},
        compiler_params=_params(("arbitrary",), 32),
        name="moe_dispatch",
    )(dest.reshape(-1), h, jnp.zeros((n_slots, d), h.dtype))


def _experts_kernel(be_ref, nu_ref, xs_ref, w1_ref, w3_ref, w2_ref, ys_ref, w1_s, w3_s, w2_s):
    i = pl.program_id(0)
    e = be_ref[i]
    e_prev = be_ref[jnp.maximum(i - 1, 0)]

    @pl.when((i == 0) | (e != e_prev))
    def _():
        w1_s[...] = w1_ref[...].astype(BF16)
        w3_s[...] = w3_ref[...].astype(BF16)
        w2_s[...] = w2_ref[...].astype(BF16)

    @pl.when(i < nu_ref[0])
    def _():
        x = xs_ref[...].astype(BF16)
        h1 = _dot(x, w1_s[...])
        h3 = _dot(x, w3_s[...])
        act = (h1 * _sigmoid(h1)) * h3
        ys_ref[...] = _dot(act.astype(BF16), w2_s[...])

    @pl.when(i >= nu_ref[0])
    def _():
        ys_ref[...] = jnp.zeros_like(ys_ref)


def _experts(xs, block_e, n_used, w1, w3, w2):
    n_slots, d = xs.shape
    de = w1.shape[2]
    tm = MOE_ROWS
    wspec = lambda shape: pl.BlockSpec((None,) + shape, lambda i, be, nu: (be[i], 0, 0))
    return pl.pallas_call(
        _experts_kernel,
        out_shape=jax.ShapeDtypeStruct((n_slots, d), F32),
        grid_spec=pltpu.PrefetchScalarGridSpec(
            num_scalar_prefetch=2,
            grid=(n_slots // tm,),
            in_specs=[pl.BlockSpec((tm, d), lambda i, be, nu: (i, 0)),
                      wspec((d, de)), wspec((d, de)), wspec((de, d))],
            out_specs=pl.BlockSpec((tm, d), lambda i, be, nu: (i, 0)),
            scratch_shapes=[pltpu.VMEM((d, de), BF16), pltpu.VMEM((d, de), BF16),
                            pltpu.VMEM((de, d), BF16)]),
        compiler_params=_params(("arbitrary",), 52),
        name="moe_experts",
    )(block_e, n_used, xs, w1, w3, w2)


def _combine_kernel(dest_ref, x_ref, gate_ref, mod_ref, gf_ref, ys_hbm, o_ref, buf0, buf1, sem,
                    *, final_norm):
    tm = x_ref.shape[0]
    bufs = (buf0, buf1)

    def issue(t, c):
        for k in range(TOP_K):
            _row_copy(ys_hbm, dest_ref[TOP_K * t + k], bufs[k], t, sem).start()
        return c

    def drain(t, c):
        for k in range(TOP_K):
            _row_copy(ys_hbm, 0, bufs[k], 0, sem).wait()
        return c

    lax.fori_loop(0, tm, issue, 0)
    lax.fori_loop(0, tm, drain, 0)
    gate = gate_ref[...]
    m = gate[:, 0:1] * buf0[...] + gate[:, 1:2] * buf1[...]
    x = x_ref[...] + mod_ref[5:6, :] * m
    if final_norm:
        ms = jnp.mean(x * x, axis=-1, keepdims=True)
        x = x * lax.rsqrt(ms + NORM_EPS) * gf_ref[...]
    o_ref[...] = x


def _combine(x, ys, dest, gates, mod, g_final, *, n_ctx_tiles, final_norm):
    t, d = x.shape
    tm = TOKEN_TILE
    lat_tiles = (t // tm - n_ctx_tiles) // (mod.shape[0] - 1)

    def cond_of(i):
        return jnp.where(i < n_ctx_tiles, 0, 1 + (i - n_ctx_tiles) // lat_tiles)

    return pl.pallas_call(
        functools.partial(_combine_kernel, final_norm=final_norm),
        out_shape=jax.ShapeDtypeStruct((t, d), F32),
        grid=(t // tm,),
        in_specs=[pl.BlockSpec((TOP_K * tm,), lambda i: (i,), memory_space=pltpu.SMEM),
                  pl.BlockSpec((tm, d), lambda i: (i, 0)),
                  pl.BlockSpec((tm, TOP_K), lambda i: (i, 0)),
                  pl.BlockSpec((None, 6, d), lambda i: (cond_of(i), 0, 0)),
                  pl.BlockSpec((1, d), lambda i: (0, 0)),
                  pl.BlockSpec(memory_space=pl.ANY)],
        out_specs=pl.BlockSpec((tm, d), lambda i: (i, 0)),
        scratch_shapes=[pltpu.VMEM((tm, d), F32), pltpu.VMEM((tm, d), F32),
                        pltpu.SemaphoreType.DMA],
        input_output_aliases={1: 0},
        compiler_params=_params(("arbitrary",), 32),
        name="moe_combine",
    )(dest.reshape(-1), x, gates, mod, g_final.reshape(1, d), ys)


def _moe(x, mod, g, router_w, router_bias, w1, w3, w2, g_final, *, n_ctx_tiles, final_norm):
    t, d = x.shape
    ne = router_w.shape[1]
    h, idx, gates, rank, cnt = _router(x, mod, g, router_w, router_bias, n_ctx_tiles=n_ctx_tiles)
    counts = cnt[:, 0].astype(jnp.int32)
    padded = (counts + MOE_ROWS - 1) // MOE_ROWS * MOE_ROWS
    pad_end = jnp.cumsum(padded)
    pad_start = pad_end - padded
    dest = (pad_start[idx] + rank).T
    n_blocks = -(-(t * TOP_K) // MOE_ROWS) + ne
    block_start = jnp.arange(n_blocks, dtype=jnp.int32) * MOE_ROWS
    block_e = jnp.minimum(jnp.searchsorted(pad_end, block_start, side='right'),
                          ne - 1).astype(jnp.int32)
    n_used = (pad_end[-1:] // MOE_ROWS).astype(jnp.int32)
    xs = _dispatch(h, dest, n_blocks * MOE_ROWS)
    ys = _experts(xs, block_e, n_used, w1, w3, w2)
    return _combine(x, ys, dest, gates.T, mod, g_final, n_ctx_tiles=n_ctx_tiles,
                    final_norm=final_norm)


def kernel(x_prompt, x_sample, state_rglru, c, c_ctx, ada_w, ada_b, norm_mix, norm_moe, norm_final, hy_w_in, hy_b_in, hy_conv_w, hy_conv_b, hy_f_w1, hy_f_b1, hy_f_freq, hy_f_w2, hy_f_b2, hy_f_w3, hy_f_skip, hy_w_out, hy_b_out, rg_w_in, rg_b_in, rg_conv_w, rg_conv_b, rg_wa, rg_ba, rg_wx, rg_bx, rg_lambda, rg_w_out, rg_b_out, router_w, router_bias, moe_w1, moe_w3, moe_w2):
    nb_ctx, len_ctx, d = x_prompt.shape
    nb_lat, len_lat, _ = x_sample.shape
    depth = ada_w.shape[0]
    n_rg = rg_w_in.shape[0]
    d_rnn = rg_w_out.shape[1]
    tok_ctx = nb_ctx * len_ctx
    tiles_ctx = tok_ctx // TOKEN_TILE
    assert tok_ctx % TOKEN_TILE == 0 and TOKEN_TILE % len_ctx == 0 and TOKEN_TILE % GRID_W == 0
    assert len_lat % TOKEN_TILE == 0 and tok_ctx % len_lat == 0
    assert V7X_SUBLANES - (nb_lat + 1) >= 0
    lat_off = tok_ctx // len_lat

    x = jnp.concatenate([x_prompt.reshape(tok_ctx, d), x_sample.reshape(nb_lat * len_lat, d)], 0)
    cond = jnp.concatenate([c_ctx[None, :], c,
                            jnp.zeros((V7X_SUBLANES - 1 - nb_lat, d), F32)], axis=0)
    mods = _ada_modulation(cond, ada_w, ada_b)[:, :1 + nb_lat]
    ctx_h0 = jnp.zeros((nb_ctx, 2, d_rnn), F32)
    states = []

    for i in range(depth):
        mod = mods[i]
        j = i // 2
        if i % 2 == 0:
            u = _inproj(x, mod, norm_mix[i], hy_w_in[j].astype(BF16), hy_b_in[j], hy_conv_w[j],
                        hy_conv_b[j], n_ctx_tiles=tiles_ctx, ctx_len=len_ctx, lat_len=GRID_W,
                        plain_cols=0)
            filt_w = (hy_f_w1[j], hy_f_b1[j], hy_f_freq[j], hy_f_w2[j], hy_f_b2[j], hy_f_w3[j])
            y_ctx = _hyena_stream(u, 0, nb_ctx, len_ctx, filt_w, hy_f_skip[j], d)
            y_lat = _hyena_stream(u, lat_off, nb_lat, len_lat, filt_w, hy_f_skip[j], d)
            y = jnp.concatenate([y_ctx, y_lat], axis=0)
            x = _outproj(y, hy_w_out[j].astype(BF16), hy_b_out[j], mod, x, n_ctx_tiles=tiles_ctx)
        else:
            u = _inproj(x, mod, norm_mix[i], rg_w_in[j].astype(BF16), rg_b_in[j], rg_conv_w[j],
                        rg_conv_b[j], n_ctx_tiles=tiles_ctx, ctx_len=len_ctx, lat_len=GRID_W,
                        plain_cols=d_rnn)
            wa, wx = rg_wa[j].astype(BF16), rg_wx[j].astype(BF16)
            y_ctx, st = _rglru_stream(u, ctx_h0, 0, nb_ctx, len_ctx, wa, wx, rg_ba[j], rg_bx[j],
                                      rg_lambda[j])
            y_lat, _ = _rglru_stream(u, state_rglru[:, j], lat_off, nb_lat, len_lat, wa, wx,
                                     rg_ba[j], rg_bx[j], rg_lambda[j])
            states.append(st)
            y = jnp.concatenate([y_ctx, y_lat], axis=0)
            x = _outproj(y, rg_w_out[j].astype(BF16), rg_b_out[j], mod, x, n_ctx_tiles=tiles_ctx)
        x = _moe(x, mod, norm_moe[i], router_w, router_bias, moe_w1[i], moe_w3[i], moe_w2[i],
                 norm_final, n_ctx_tiles=tiles_ctx, final_norm=(i == depth - 1))

    y_prompt = x[:tok_ctx].reshape(nb_ctx, len_ctx, d)
    y_sample = x[tok_ctx:].reshape(nb_lat, len_lat, d)
    new_state = jnp.stack(states, axis=1).astype(x_prompt.dtype)
    return (y_prompt, y_sample, new_state)
```

```python
import functools
import math

import numpy as np
import jax
import jax.numpy as jnp
from jax import lax
from jax.experimental import pallas as pl
from jax.experimental.pallas import tpu as pltpu

F32 = jnp.float32
BF16 = jnp.bfloat16

GRID_W = 64
FILTER_BANDS = 16
FILTER_EPS = 1e-6
MIN_DECAY = math.log(1e-2) / 0.3
MAX_DECAY = math.log(1e-2) / 1.5
RG_C = 8.0
N_GROUPS = 4
TOP_K = 2
NORM_EPS = 1e-6

V7X_LANES = 128
V7X_SUBLANES = 8
V7X_VMEM_BYTES = 64 * 1024 * 1024

TOKEN_TILE = 512
COL_TILE = 512
MOE_ROWS = 256
DFT_ROWS = 512
DFT_SPLIT = 64
RG_COLS = 256
ROW_DMA_UNROLL = 8


def _params(sem, vmem_mb):
    return pltpu.CompilerParams(dimension_semantics=sem,
                                vmem_limit_bytes=vmem_mb * 1024 * 1024)


def _dot(a, b):
    return jnp.dot(a, b, preferred_element_type=F32)


def _split(x):
    hi = x.astype(BF16)
    lo = (x - hi.astype(F32)).astype(BF16)
    return hi, lo


def _dot3(a, b):
    ah, al = _split(a)
    bh, bl = _split(b)
    return _dot(ah, bh) + (_dot(ah, bl) + _dot(al, bh))


def _dot3_nt(a, b):
    dn = (((1,), (1,)), ((), ()))
    d = lambda x, y: lax.dot_general(x, y, dn, preferred_element_type=F32)
    ah, al = _split(a)
    bh, bl = _split(b)
    return d(ah, bh) + (d(ah, bl) + d(al, bh))


def _sigmoid(x):
    return 1.0 / (1.0 + jnp.exp(-x))


def _norm_mod(x, g, shift, scale):
    ms = jnp.mean(x * x, axis=-1, keepdims=True)
    return (x * lax.rsqrt(ms + NORM_EPS) * g) * (1.0 + scale) + shift


def _ada_kernel(c_ref, w_ref, b_ref, o_ref):
    c = c_ref[...]
    o_ref[...] = _dot3(c * _sigmoid(c), w_ref[...]) + b_ref[...]


def _ada_modulation(cond, ada_w, ada_b):
    depth, d, n = ada_w.shape
    tn = n // 4
    out = pl.pallas_call(
        _ada_kernel,
        out_shape=jax.ShapeDtypeStruct((depth, cond.shape[0], n), F32),
        grid=(depth, n // tn),
        in_specs=[pl.BlockSpec(cond.shape, lambda i, j: (0, 0)),
                  pl.BlockSpec((None, d, tn), lambda i, j: (i, 0, j)),
                  pl.BlockSpec((None, 1, tn), lambda i, j: (i, 0, j))],
        out_specs=pl.BlockSpec((None, cond.shape[0], tn), lambda i, j: (i, 0, j)),
        compiler_params=_params(("parallel", "parallel"), 40),
        name="ada_modulation",
    )(cond, ada_w, ada_b.reshape(depth, 1, n))
    return out.reshape(depth, cond.shape[0], 6, d)


def _inproj_kernel(x_ref, mod_ref, g_ref, w_ref, b_ref, cw_ref, cb_ref, o_ref, h_scr, *,
                   n_ctx_tiles, ctx_len, lat_len, width, plain_col_tiles):
    i = pl.program_id(0)
    j = pl.program_id(1)

    @pl.when(j == 0)
    def _():
        h = _norm_mod(x_ref[...], g_ref[...], mod_ref[0:1, :], mod_ref[1:2, :])
        h_scr[...] = h.astype(BF16)

    u = _dot(h_scr[...], w_ref[...]) + b_ref[...]

    def conv():
        rows = u.shape[0]
        seg_mask = jnp.where(i < n_ctx_tiles, ctx_len - 1, lat_len - 1)
        pos = lax.broadcasted_iota(jnp.int32, (rows, 1), 0) & seg_mask
        pad_l = (width - 1) // 2
        acc = jnp.zeros_like(u) + cb_ref[...]
        for k in range(width):
            off = k - pad_l
            if off == 0:
                term = u
            else:
                shifted = pltpu.roll(u, (-off) % rows, axis=0)
                ok = (pos + off >= 0) & (pos + off <= seg_mask)
                term = jnp.where(ok, shifted, 0.0)
            acc = acc + cw_ref[k:k + 1, :] * term
        o_ref[...] = acc

    if plain_col_tiles == 0:
        conv()
    else:
        pl.when(j >= plain_col_tiles)(conv)

        @pl.when(j < plain_col_tiles)
        def _():
            o_ref[...] = u


def _inproj(x, mod, g, w, b, conv_w, conv_b, *, n_ctx_tiles, ctx_len, lat_len, plain_cols):
    t, d = x.shape
    n = w.shape[1]
    tm, tn = TOKEN_TILE, COL_TILE
    width = conv_w.shape[0]
    plain_tiles = plain_cols // tn
    lat_tiles = (t // tm - n_ctx_tiles) // (mod.shape[0] - 1)

    def cond_of(i):
        return jnp.where(i < n_ctx_tiles, 0, 1 + (i - n_ctx_tiles) // lat_tiles)

    kern = functools.partial(_inproj_kernel, n_ctx_tiles=n_ctx_tiles, ctx_len=ctx_len,
                             lat_len=lat_len, width=width, plain_col_tiles=plain_tiles)
    return pl.pallas_call(
        kern,
        out_shape=jax.ShapeDtypeStruct((t, n), F32),
        grid=(t // tm, n // tn),
        in_specs=[pl.BlockSpec((tm, d), lambda i, j: (i, 0)),
                  pl.BlockSpec((None, 6, d), lambda i, j: (cond_of(i), 0, 0)),
                  pl.BlockSpec((1, d), lambda i, j: (0, 0)),
                  pl.BlockSpec((d, tn), lambda i, j: (0, j)),
                  pl.BlockSpec((1, tn), lambda i, j: (0, j)),
                  pl.BlockSpec((width, tn), lambda i, j: (0, jnp.maximum(j - plain_tiles, 0))),
                  pl.BlockSpec((1, tn), lambda i, j: (0, jnp.maximum(j - plain_tiles, 0)))],
        out_specs=pl.BlockSpec((tm, tn), lambda i, j: (i, j)),
        scratch_shapes=[pltpu.VMEM((tm, d), BF16)],
        compiler_params=_params(("parallel", "arbitrary"), 32),
        name="norm_inproj_conv",
    )(x, mod, g.reshape(1, d), w, b.reshape(1, n), conv_w, conv_b.reshape(1, -1))


def _outproj_kernel(y_ref, w_ref, b_ref, mod_ref, x_ref, o_ref):
    m = _dot(y_ref[...], w_ref[...]) + b_ref[...]
    o_ref[...] = x_ref[...] + mod_ref[2:3, :] * m


def _outproj(y, w, b, mod, x, *, n_ctx_tiles):
    t, d = x.shape
    tm = TOKEN_TILE
    lat_tiles = (t // tm - n_ctx_tiles) // (mod.shape[0] - 1)

    def cond_of(i):
        return jnp.where(i < n_ctx_tiles, 0, 1 + (i - n_ctx_tiles) // lat_tiles)

    return pl.pallas_call(
        _outproj_kernel,
        out_shape=jax.ShapeDtypeStruct((t, d), F32),
        grid=(t // tm,),
        in_specs=[pl.BlockSpec((tm, y.shape[1]), lambda i: (i, 0)),
                  pl.BlockSpec(w.shape, lambda i: (0, 0)),
                  pl.BlockSpec((1, d), lambda i: (0, 0)),
                  pl.BlockSpec((None, 6, d), lambda i: (cond_of(i), 0, 0)),
                  pl.BlockSpec((tm, d), lambda i: (i, 0))],
        out_specs=pl.BlockSpec((tm, d), lambda i: (i, 0)),
        input_output_aliases={4: 0},
        compiler_params=_params(("parallel",), 32),
        name="outproj_residual",
    )(y, w, b.reshape(1, d), mod, x)


def _filter_kernel(z_ref, w1_ref, b1_ref, fr_ref, w2_ref, b2_ref, w3a_ref, w3b_ref, w3c_ref,
                   w3d_ref, t_ref, dl_ref, o_ref, *, orders):
    fr = fr_ref[...]
    h = jnp.sin(fr * (_dot3(z_ref[...], w1_ref[...]) + b1_ref[...]))
    h = jnp.sin(fr * (_dot3(h, w2_ref[...]) + b2_ref[...]))
    window = jnp.exp(-t_ref[...] * dl_ref[...])
    row = lax.broadcasted_iota(jnp.int32, (h.shape[0], 1), 0)
    w3 = ((w3a_ref, w3b_ref), (w3c_ref, w3d_ref))
    for o in range(orders):
        h_fwd = _dot3(h, w3[0][o][...]) * window
        h_bwd = jnp.where(row == 0, 0.0, _dot3(h, w3[1][o][...]) * window)
        norm = (jnp.sum(jnp.abs(h_fwd), axis=0, keepdims=True)
                + jnp.sum(jnp.abs(h_bwd), axis=0, keepdims=True) + FILTER_EPS)
        o_ref[2 * o] = h_fwd / norm
        o_ref[2 * o + 1] = h_bwd / norm


def _hyena_filter(length, f_w1, f_b1, f_freq, f_w2, f_b2, f_w3, d):
    emb, hid = f_w1.shape
    orders = f_w3.shape[1] // (2 * d)
    t = jnp.linspace(0.0, 1.0, length, dtype=F32)[:, None]
    w = (2.0 * math.pi / length) * jnp.arange(length, dtype=F32)[:, None]
    bands = jnp.linspace(1e-4, FILTER_BANDS - 1, FILTER_BANDS, dtype=F32)[None, :]
    z = jnp.concatenate([t, jnp.cos(bands * w), -jnp.sin(bands * w)], axis=-1)
    emb_pad = V7X_LANES
    z = jnp.pad(z, ((0, 0), (0, emb_pad - emb)))
    w1 = jnp.pad(f_w1, ((0, emb_pad - emb), (0, 0)))
    deltas = jnp.abs(jnp.linspace(MIN_DECAY, MAX_DECAY, d, dtype=F32))[None, :]
    td = 256
    nd = d // td
    full = lambda shape: pl.BlockSpec(shape, lambda j: (0,) * len(shape))
    w3_spec = lambda side, o: pl.BlockSpec((hid, td), lambda j: (0, (side * orders + o) * nd + j))
    assert orders == 2
    return pl.pallas_call(
        functools.partial(_filter_kernel, orders=orders),
        out_shape=jax.ShapeDtypeStruct((2 * orders, length, d), F32),
        grid=(nd,),
        in_specs=[full((length, emb_pad)), full((emb_pad, hid)), full((1, hid)), full((1, hid)),
                  full((hid, hid)), full((1, hid)),
                  w3_spec(0, 0), w3_spec(0, 1), w3_spec(1, 0), w3_spec(1, 1),
                  full((length, 1)), pl.BlockSpec((1, td), lambda j: (0, j))],
        out_specs=pl.BlockSpec((2 * orders, length, td), lambda j: (0, 0, j)),
        compiler_params=_params(("parallel",), 48),
        name="hyena_filter",
    )(z, w1, f_b1.reshape(1, hid), f_freq.reshape(1, hid), f_w2, f_b2.reshape(1, hid),
      f_w3, f_w3, f_w3, f_w3, t, deltas)


def _dft_matrices(length):
    n = 2 * length
    f0n, f1n = DFT_SPLIT, length // DFT_SPLIT
    tt = np.arange(length, dtype=np.int64)[None, :]
    ang_a = 2.0 * np.pi * ((DFT_SPLIT * np.arange(f1n, dtype=np.int64)[:, None] * tt) % n) / n
    ang_b = 2.0 * np.pi * ((np.arange(f0n, dtype=np.int64)[:, None] * tt) % n) / n
    ca, sa = (jnp.asarray(f(ang_a), F32)[:, None, :] for f in (np.cos, np.sin))
    cb, sb = (jnp.asarray(f(ang_b), F32)[None, :, :] for f in (np.cos, np.sin))
    cos_m = (ca * cb - sa * sb).reshape(length, length)
    sin_m = (sa * cb + ca * sb).reshape(length, length)
    nyq = jnp.asarray(1.0 - 2.0 * (np.arange(length) % 2), F32)[None, :]
    f_is0 = (jnp.arange(length) == 0)[:, None]
    im_m = jnp.where(f_is0, nyq, -sin_m)
    th = DFT_ROWS // 2
    fwd = jnp.stack([cos_m.reshape(length // th, th, length),
                     im_m.reshape(length // th, th, length)], axis=1).reshape(n, length)
    scale = jnp.where(f_is0, 1.0 / n, 2.0 / n)
    inv = jnp.stack([(cos_m * scale).reshape(length // th, th, length),
                     (im_m * scale).reshape(length // th, th, length)], axis=1).reshape(n, length).T
    return fwd.astype(BF16), inv.astype(BF16)


def _spec_kernel(a_ref, hf_ref, hb_ref, o_ref, hf_scr, hb_scr):
    m = pl.program_id(2)

    @pl.when(m == 0)
    def _():
        hf_scr[...] = hf_ref[...].astype(BF16)
        hb_scr[...] = hb_ref[...].astype(BF16)

    a = a_ref[...]
    ff = _dot(a, hf_scr[...])
    fb = _dot(a, hb_scr[...])
    th = ff.shape[0] // 2
    row = lax.broadcasted_iota(jnp.int32, (th, 1), 0)
    dc = (row == 0) & (m == 0)
    o_ref[0:th, :] = ff[:th] + fb[:th]
    o_ref[th:, :] = jnp.where(dc, ff[th:] + fb[th:], ff[th:] - fb[th:])


def _filter_spectrum(fwd, filt):
    n2, length, d = filt.shape
    orders = n2 // 2
    n = 2 * length
    tm, tc = DFT_ROWS, 512
    filt2 = filt.reshape(n2 * length, d)
    return pl.pallas_call(
        _spec_kernel,
        out_shape=jax.ShapeDtypeStruct((orders * n, d), F32),
        grid=(orders, d // tc, n // tm),
        in_specs=[pl.BlockSpec((tm, length), lambda o, c, m: (m, 0)),
                  pl.BlockSpec((length, tc), lambda o, c, m: (2 * o, c)),
                  pl.BlockSpec((length, tc), lambda o, c, m: (2 * o + 1, c))],
        out_specs=pl.BlockSpec((tm, tc), lambda o, c, m: (o * (n // tm) + m, c)),
        scratch_shapes=[pltpu.VMEM((length, tc), BF16), pltpu.VMEM((length, tc), BF16)],
        compiler_params=_params(("parallel", "parallel", "arbitrary"), 48),
        name="hyena_filter_spectrum",
    )(fwd, filt2, filt2)


def _dft_fwd_kernel(a_ref, z_ref, k_ref, o_ref, z_scr):
    m = pl.program_id(1)

    @pl.when(m == 0)
    def _():
        z_scr[...] = z_ref[...].astype(BF16)

    acc = _dot(a_ref[...], z_scr[...])
    th = acc.shape[0] // 2
    xr, xi = acc[:th], acc[th:]
    kr, ki = k_ref[0:th, :], k_ref[th:, :]
    row = lax.broadcasted_iota(jnp.int32, (th, 1), 0)
    dc = (row == 0) & (m == 0)
    o_ref[0:th, :] = (xr * kr - jnp.where(dc, 0.0, xi * ki)).astype(o_ref.dtype)
    o_ref[th:, :] = jnp.where(dc, xi * ki, xr * ki + xi * kr).astype(o_ref.dtype)


def _dft_forward(fwd, src, spec, *, nb, length, row_off, col_blk, order):
    n = 2 * length
    d = spec.shape[1]
    tm = DFT_ROWS
    nt = n // tm
    return pl.pallas_call(
        _dft_fwd_kernel,
        out_shape=jax.ShapeDtypeStruct((nb * n, d), BF16),
        grid=(nb, nt),
        in_specs=[pl.BlockSpec((tm, length), lambda b, m: (m, 0)),
                  pl.BlockSpec((length, d), lambda b, m: (row_off + b, col_blk)),
                  pl.BlockSpec((tm, d), lambda b, m: (order * nt + m, 0))],
        out_specs=pl.BlockSpec((tm, d), lambda b, m: (b * nt + m, 0)),
        scratch_shapes=[pltpu.VMEM((length, d), BF16)],
        compiler_params=_params(("parallel", "arbitrary"), 48),
        name="hyena_dft_forward",
    )(fwd, src, spec)


def _dft_inv_kernel(a_ref, p_ref, v_ref, g_ref, s_ref, o_ref):
    y = _dot(a_ref[...], p_ref[...])
    v = v_ref[...]
    o_ref[...] = (g_ref[...] * (y + v * s_ref[...])).astype(o_ref.dtype)


def _dft_inverse(inv, prod, vsrc, gsrc, skip, *, nb, length, v_off, v_col, g_off, g_col,
                 out_dtype):
    n = 2 * length
    d = prod.shape[1]
    tm = min(DFT_ROWS, length)
    nt = length // tm
    return pl.pallas_call(
        _dft_inv_kernel,
        out_shape=jax.ShapeDtypeStruct((nb * length, d), out_dtype),
        grid=(nb, nt),
        in_specs=[pl.BlockSpec((tm, n), lambda b, m: (m, 0)),
                  pl.BlockSpec((n, d), lambda b, m: (b, 0)),
                  pl.BlockSpec((tm, d), lambda b, m: ((v_off + b) * nt + m, v_col)),
                  pl.BlockSpec((tm, d), lambda b, m: ((g_off + b) * nt + m, g_col)),
                  pl.BlockSpec((1, d), lambda b, m: (0, 0))],
        out_specs=pl.BlockSpec((tm, d), lambda b, m: (b * nt + m, 0)),
        compiler_params=_params(("parallel", "arbitrary"), 48),
        name="hyena_dft_inverse",
    )(inv, prod, vsrc, gsrc, skip.reshape(1, d))


def _hyena_stream(u, row_off, nb, length, filt_w, f_skip, d):
    fwd, inv = _dft_matrices(length)
    filt = _hyena_filter(length, *filt_w, d)
    spec = _filter_spectrum(fwd, filt)
    p1 = _dft_forward(fwd, u, spec, nb=nb, length=length, row_off=row_off, col_blk=0, order=0)
    z1 = _dft_inverse(inv, p1, u, u, f_skip[0], nb=nb, length=length, v_off=row_off, v_col=0,
                      g_off=row_off, g_col=1, out_dtype=F32)
    p2 = _dft_forward(fwd, z1, spec, nb=nb, length=length, row_off=0, col_blk=0, order=1)
    return _dft_inverse(inv, p2, z1, u, f_skip[1], nb=nb, length=length, v_off=0, v_col=0,
                        g_off=row_off, g_col=2, out_dtype=BF16)


def _gelu_tanh(x):
    return 0.5 * x * (1.0 + jnp.tanh(math.sqrt(2.0 / math.pi) * (x + 0.044715 * (x * x * x))))


def _log1p(e):
    u = 1.0 + e
    d = u - 1.0
    return jnp.where(d == 0.0, e, jnp.log(u) * (e / jnp.where(d == 0.0, 1.0, d)))


def _rglru_kernel(gate_ref, rec_ref, wa_ref, wx_ref, ba_ref, bx_ref, lam_ref, h0_ref,
                  y_ref, st_ref, a_scr, b_scr):
    length, cols = rec_ref.shape
    groups = length // V7X_SUBLANES
    rec = rec_ref[...]
    rec16 = rec.astype(BF16)
    pos = lax.broadcasted_iota(jnp.int32, (groups, V7X_SUBLANES, cols), 1)

    for d in range(2):
        r_gate = _sigmoid(_dot(rec16, wa_ref[d]) + ba_ref[d])
        i_gate = _sigmoid(_dot(rec16, wx_ref[d]) + bx_ref[d])
        nlam = -lam_ref[d]
        softplus = jnp.maximum(nlam, 0.0) + _log1p(jnp.exp(-jnp.abs(nlam)))
        log_a = (-RG_C * softplus) * r_gate
        a = jnp.exp(log_a).reshape(groups, V7X_SUBLANES, cols)
        th = jnp.tanh(log_a)
        b = (jnp.sqrt(-2.0 * th / (1.0 - th)) * (i_gate * rec)).reshape(
            groups, V7X_SUBLANES, cols)
        for s in (1, 2, 4):
            if d == 0:
                a_sh = pltpu.roll(a, s, axis=1)
                b_sh = pltpu.roll(b, s, axis=1)
                live = pos >= s
            else:
                a_sh = pltpu.roll(a, V7X_SUBLANES - s, axis=1)
                b_sh = pltpu.roll(b, V7X_SUBLANES - s, axis=1)
                live = pos < V7X_SUBLANES - s
            b = jnp.where(live, a * b_sh, 0.0) + b
            a = jnp.where(live, a * a_sh, a)
        a_scr[d] = a.reshape(length, cols)
        b_scr[d] = b.reshape(length, cols)

    def step(g, carry):
        cf, cb = carry
        rf = pl.multiple_of(g * V7X_SUBLANES, V7X_SUBLANES)
        rb = pl.multiple_of((groups - 1 - g) * V7X_SUBLANES, V7X_SUBLANES)
        hf = a_scr[0, pl.ds(rf, V7X_SUBLANES), :] * cf + b_scr[0, pl.ds(rf, V7X_SUBLANES), :]
        hb = a_scr[1, pl.ds(rb, V7X_SUBLANES), :] * cb + b_scr[1, pl.ds(rb, V7X_SUBLANES), :]
        b_scr[0, pl.ds(rf, V7X_SUBLANES), :] = hf
        b_scr[1, pl.ds(rb, V7X_SUBLANES), :] = hb
        cf = jnp.broadcast_to(hf[V7X_SUBLANES - 1:V7X_SUBLANES, :], hf.shape)
        cb = jnp.broadcast_to(hb[0:1, :], hb.shape)
        return cf, cb

    init = (jnp.broadcast_to(h0_ref[0], (V7X_SUBLANES, cols)),
            jnp.broadcast_to(h0_ref[1], (V7X_SUBLANES, cols)))
    cf, cb = lax.fori_loop(0, groups, step, init)
    st_ref[0] = cf[0:1, :]
    st_ref[1] = cb[0:1, :]
    y_ref[...] = ((b_scr[0] + b_scr[1]) * _gelu_tanh(gate_ref[...])).astype(y_ref.dtype)


def _rglru_stream(u, h0, row_off, nb, length, wa, wx, ba, bx, lam):
    dr = u.shape[1] // 2
    tc = RG_COLS
    nc = dr // tc
    vec = lambda a: a.reshape(2, 1, dr)
    vec_spec = pl.BlockSpec((2, 1, tc), lambda b, c: (0, 0, c))
    w_spec = pl.BlockSpec((2, None, tc, tc), lambda b, c: (0, c, 0, 0))
    y, st = pl.pallas_call(
        _rglru_kernel,
        out_shape=(jax.ShapeDtypeStruct((nb * length, dr), BF16),
                   jax.ShapeDtypeStruct((nb, 2, 1, dr), F32)),
        grid=(nb, nc),
        in_specs=[pl.BlockSpec((length, tc), lambda b, c: (row_off + b, c)),
                  pl.BlockSpec((length, tc), lambda b, c: (row_off + b, nc + c)),
                  w_spec, w_spec, vec_spec, vec_spec, vec_spec,
                  pl.BlockSpec((None, 2, 1, tc), lambda b, c: (b, 0, 0, c))],
        out_specs=(pl.BlockSpec((length, tc), lambda b, c: (b, c)),
                   pl.BlockSpec((None, 2, 1, tc), lambda b, c: (b, 0, 0, c))),
        scratch_shapes=[pltpu.VMEM((2, length, tc), F32), pltpu.VMEM((2, length, tc), F32)],
        compiler_params=_params(("parallel", "parallel"), 48),
        name="rglru_scan",
    )(u, u, wa, wx, vec(ba), vec(bx), vec(lam), h0.reshape(nb, 2, 1, dr))
    return y, st.reshape(nb, 2, dr)


def _router_kernel(x_ref, mod_ref, g_ref, rwt_ref, bias_ref, h_ref, idx_ref, gate_ref,
                   rank_ref, cnt_ref, tri_scr, carry_scr, *, n_experts):
    i = pl.program_id(0)
    tm = x_ref.shape[0]
    per_group = n_experts // N_GROUPS

    @pl.when(i == 0)
    def _():
        r = lax.broadcasted_iota(jnp.int32, (tm, tm), 0)
        c = lax.broadcasted_iota(jnp.int32, (tm, tm), 1)
        tri_scr[...] = jnp.where(r < c, 1.0, 0.0).astype(BF16)
        carry_scr[...] = jnp.zeros_like(carry_scr)

    h = _norm_mod(x_ref[...], g_ref[...], mod_ref[3:4, :], mod_ref[4:5, :])
    h_ref[...] = h
    logits = _dot3_nt(rwt_ref[...], h)
    p = jnp.exp(logits - jnp.max(logits, axis=0, keepdims=True))
    scores = p / jnp.sum(p, axis=0, keepdims=True)
    sel = scores + bias_ref[...]
    rows = [sel[e:e + 1, :] for e in range(n_experts)]

    best_val = None
    for gi in range(N_GROUPS):
        v = rows[gi * per_group:(gi + 1) * per_group]
        pair = None
        for a in range(per_group):
            for b in range(a + 1, per_group):
                s = v[a] + v[b]
                pair = s if pair is None else jnp.maximum(pair, s)
        if best_val is None:
            best_val, best_grp = pair, jnp.zeros_like(pair, dtype=jnp.int32)
        else:
            take = pair > best_val
            best_val = jnp.where(take, pair, best_val)
            best_grp = jnp.where(take, gi, best_grp)

    neg = jnp.float32(-jnp.inf)
    masked = [jnp.where(best_grp == e // per_group, rows[e], neg) for e in range(n_experts)]

    def argmax_first(vals):
        bv, bi = vals[0], jnp.zeros_like(best_grp)
        for e in range(1, n_experts):
            take = vals[e] > bv
            bv = jnp.where(take, vals[e], bv)
            bi = jnp.where(take, e, bi)
        return bi

    idx0 = argmax_first(masked)
    idx1 = argmax_first([jnp.where(idx0 == e, neg, masked[e]) for e in range(n_experts)])

    e_iota = lax.broadcasted_iota(jnp.int32, (n_experts, tm), 0)
    hit0 = e_iota == idx0
    hit1 = e_iota == idx1
    g0 = jnp.sum(jnp.where(hit0, scores, 0.0), axis=0, keepdims=True)
    g1 = jnp.sum(jnp.where(hit1, scores, 0.0), axis=0, keepdims=True)
    gsum = g0 + g1
    onehot = jnp.where(hit0 | hit1, 1.0, 0.0)
    before = _dot(onehot.astype(BF16), tri_scr[...]) + carry_scr[:, 0:1]
    r0 = jnp.sum(jnp.where(hit0, before, 0.0), axis=0, keepdims=True)
    r1 = jnp.sum(jnp.where(hit1, before, 0.0), axis=0, keepdims=True)
    idx_ref[0:1, :] = idx0
    idx_ref[1:2, :] = idx1
    gate_ref[0:1, :] = g0 / gsum
    gate_ref[1:2, :] = g1 / gsum
    rank_ref[0:1, :] = r0.astype(jnp.int32)
    rank_ref[1:2, :] = r1.astype(jnp.int32)
    carry_scr[...] = carry_scr[...] + jnp.sum(onehot, axis=1, keepdims=True)
    cnt_ref[...] = carry_scr[...]


def _router(x, mod, g, router_w, router_bias, *, n_ctx_tiles):
    t, d = x.shape
    ne = router_w.shape[1]
    tm = TOKEN_TILE
    lat_tiles = (t // tm - n_ctx_tiles) // (mod.shape[0] - 1)

    def cond_of(i):
        return jnp.where(i < n_ctx_tiles, 0, 1 + (i - n_ctx_tiles) // lat_tiles)

    row2 = pl.BlockSpec((TOP_K, tm), lambda i: (0, i))
    return pl.pallas_call(
        functools.partial(_router_kernel, n_experts=ne),
        out_shape=(jax.ShapeDtypeStruct((t, d), F32),
                   jax.ShapeDtypeStruct((TOP_K, t), jnp.int32),
                   jax.ShapeDtypeStruct((TOP_K, t), F32),
                   jax.ShapeDtypeStruct((TOP_K, t), jnp.int32),
                   jax.ShapeDtypeStruct((ne, V7X_LANES), F32)),
        grid=(t // tm,),
        in_specs=[pl.BlockSpec((tm, d), lambda i: (i, 0)),
                  pl.BlockSpec((None, 6, d), lambda i: (cond_of(i), 0, 0)),
                  pl.BlockSpec((1, d), lambda i: (0, 0)),
                  pl.BlockSpec((ne, d), lambda i: (0, 0)),
                  pl.BlockSpec((ne, 1), lambda i: (0, 0))],
        out_specs=(pl.BlockSpec((tm, d), lambda i: (i, 0)), row2, row2, row2,
                   pl.BlockSpec((ne, V7X_LANES), lambda i: (0, 0))),
        scratch_shapes=[pltpu.VMEM((tm, tm), BF16), pltpu.VMEM((ne, V7X_LANES), F32)],
        compiler_params=_params(("arbitrary",), 32),
        name="moe_router",
    )(x, mod, g.reshape(1, d), router_w.T, router_bias.reshape(ne, 1))


def _row_copy(src, src_row, dst, dst_row, sem):
    return pltpu.make_async_copy(src.at[pl.ds(src_row, 1)], dst.at[pl.ds(dst_row, 1)], sem)


def _dispatch_kernel(dest_ref, h_ref, xs_in, xs_out, sem):
    del xs_in
    tm = h_ref.shape[0]

    def issue(g, c):
        for r in range(ROW_DMA_UNROLL):
            t = g * ROW_DMA_UNROLL + r
            for k in range(TOP_K):
                _row_copy(h_ref, t, xs_out, dest_ref[TOP_K * t + k], sem).start(priority=k)
        return c

    def drain(g, c):
        for _ in range(ROW_DMA_UNROLL * TOP_K):
            _row_copy(h_ref, 0, xs_out, 0, sem).wait()
        return c

    lax.fori_loop(0, tm // ROW_DMA_UNROLL, issue, 0)
    lax.fori_loop(0, tm // ROW_DMA_UNROLL, drain, 0)


def _dispatch(h, dest, n_slots):
    t, d = h.shape
    tm = TOKEN_TILE
    return pl.pallas_call(
        _dispatch_kernel,
        out_shape=jax.ShapeDtypeStruct((n_slots, d), h.dtype),
        grid=(t // tm,),
        in_specs=[pl.BlockSpec((TOP_K * tm,), lambda i: (i,), memory_space=pltpu.SMEM),
                  pl.BlockSpec((tm, d), lambda i: (i, 0)),
                  pl.BlockSpec(memory_space=pl.ANY)],
        out_specs=pl.BlockSpec(memory_space=pl.ANY),
        scratch_shapes=[pltpu.SemaphoreType.DMA],
        input_output_aliases={2: 0},
        compiler_params=_params(("arbitrary",), 32),
        name="moe_dispatch",
    )(dest.reshape(-1), h, jnp.zeros((n_slots, d), h.dtype))


def _experts_kernel(be_ref, nu_ref, xs_ref, w1_ref, w3_ref, w2_ref, ys_ref, w1_s, w3_s, w2_s):
    i = pl.program_id(0)
    e = be_ref[i]
    e_prev = be_ref[jnp.maximum(i - 1, 0)]

    @pl.when((i == 0) | (e != e_prev))
    def _():
        w1_s[...] = w1_ref[...].astype(BF16)
        w3_s[...] = w3_ref[...].astype(BF16)
        w2_s[...] = w2_ref[...].astype(BF16)

    @pl.when(i < nu_ref[0])
    def _():
        x = xs_ref[...].astype(BF16)
        h1 = _dot(x, w1_s[...])
        h3 = _dot(x, w3_s[...])
        act = (h1 * _sigmoid(h1)) * h3
        ys_ref[...] = _dot(act.astype(BF16), w2_s[...])

    @pl.when(i >= nu_ref[0])
    def _():
        ys_ref[...] = jnp.zeros_like(ys_ref)


def _experts(xs, block_e, n_used, w1, w3, w2, layer):
    n_slots, d = xs.shape
    de = w1.shape[3]
    tm = MOE_ROWS
    wspec = lambda shape: pl.BlockSpec((None, None) + shape,
                                       lambda i, be, nu: (layer, be[i], 0, 0))
    return pl.pallas_call(
        _experts_kernel,
        out_shape=jax.ShapeDtypeStruct((n_slots, d), F32),
        grid_spec=pltpu.PrefetchScalarGridSpec(
            num_scalar_prefetch=2,
            grid=(n_slots // tm,),
            in_specs=[pl.BlockSpec((tm, d), lambda i, be, nu: (i, 0)),
                      wspec((d, de)), wspec((d, de)), wspec((de, d))],
            out_specs=pl.BlockSpec((tm, d), lambda i, be, nu: (i, 0)),
            scratch_shapes=[pltpu.VMEM((d, de), BF16), pltpu.VMEM((d, de), BF16),
                            pltpu.VMEM((de, d), BF16)]),
        compiler_params=_params(("arbitrary",), 52),
        name="moe_experts",
    )(block_e, n_used, xs, w1, w3, w2)


def _combine_kernel(dest_ref, x_ref, gate_ref, mod_ref, gf_ref, ys_hbm, o_ref, buf0, buf1, sem,
                    *, final_norm):
    tm = x_ref.shape[0]
    bufs = (buf0, buf1)

    def issue(g, c):
        for r in range(ROW_DMA_UNROLL):
            t = g * ROW_DMA_UNROLL + r
            for k in range(TOP_K):
                _row_copy(ys_hbm, dest_ref[TOP_K * t + k], bufs[k], t, sem).start(priority=k)
        return c

    def drain(g, c):
        for r in range(ROW_DMA_UNROLL):
            for k in range(TOP_K):
                _row_copy(ys_hbm, 0, bufs[k], 0, sem).wait()
        return c

    lax.fori_loop(0, tm // ROW_DMA_UNROLL, issue, 0)
    lax.fori_loop(0, tm // ROW_DMA_UNROLL, drain, 0)
    gate = gate_ref[...]
    m = gate[:, 0:1] * buf0[...] + gate[:, 1:2] * buf1[...]
    x = x_ref[...] + mod_ref[5:6, :] * m
    if final_norm:
        ms = jnp.mean(x * x, axis=-1, keepdims=True)
        x = x * lax.rsqrt(ms + NORM_EPS) * gf_ref[...]
    o_ref[...] = x


def _combine(x, ys, dest, gates, mod, g_final, *, n_ctx_tiles, final_norm):
    t, d = x.shape
    tm = TOKEN_TILE
    lat_tiles = (t // tm - n_ctx_tiles) // (mod.shape[0] - 1)

    def cond_of(i):
        return jnp.where(i < n_ctx_tiles, 0, 1 + (i - n_ctx_tiles) // lat_tiles)

    return pl.pallas_call(
        functools.partial(_combine_kernel, final_norm=final_norm),
        out_shape=jax.ShapeDtypeStruct((t, d), F32),
        grid=(t // tm,),
        in_specs=[pl.BlockSpec((TOP_K * tm,), lambda i: (i,), memory_space=pltpu.SMEM),
                  pl.BlockSpec((tm, d), lambda i: (i, 0)),
                  pl.BlockSpec((tm, TOP_K), lambda i: (i, 0)),
                  pl.BlockSpec((None, 6, d), lambda i: (cond_of(i), 0, 0)),
                  pl.BlockSpec((1, d), lambda i: (0, 0)),
                  pl.BlockSpec(memory_space=pl.ANY)],
        out_specs=pl.BlockSpec((tm, d), lambda i: (i, 0)),
        scratch_shapes=[pltpu.VMEM((tm, d), F32), pltpu.VMEM((tm, d), F32),
                        pltpu.SemaphoreType.DMA],
        input_output_aliases={1: 0},
        compiler_params=_params(("arbitrary",), 32),
        name="moe_combine",
    )(dest.reshape(-1), x, gates, mod, g_final.reshape(1, d), ys)


def _moe(x, mod, g, router_w, router_bias, w1, w3, w2, layer, g_final, *, n_ctx_tiles,
         final_norm):
    t, d = x.shape
    ne = router_w.shape[1]
    h, idx, gates, rank, cnt = _router(x, mod, g, router_w, router_bias, n_ctx_tiles=n_ctx_tiles)
    counts = cnt[:, 0].astype(jnp.int32)
    padded = (counts + MOE_ROWS - 1) // MOE_ROWS * MOE_ROWS
    pad_end = jnp.cumsum(padded)
    pad_start = pad_end - padded
    e_ids = jnp.arange(ne, dtype=jnp.int32)
    start_of = jnp.sum(jnp.where(idx[None] == e_ids[:, None, None],
                                 pad_start[:, None, None], 0), axis=0)
    dest = (start_of + rank).T
    n_blocks = -(-(t * TOP_K) // MOE_ROWS) + ne
    block_start = jnp.arange(n_blocks, dtype=jnp.int32) * MOE_ROWS
    block_e = jnp.minimum(jnp.sum(block_start[:, None] >= pad_end[None, :], axis=1),
                          ne - 1).astype(jnp.int32)
    n_used = (pad_end[-1:] // MOE_ROWS).astype(jnp.int32)
    xs = _dispatch(h, dest, n_blocks * MOE_ROWS)
    ys = _experts(xs, block_e, n_used, w1, w3, w2, layer)
    return _combine(x, ys, dest, gates.T, mod, g_final, n_ctx_tiles=n_ctx_tiles,
                    final_norm=final_norm)


def kernel(x_prompt, x_sample, state_rglru, c, c_ctx, ada_w, ada_b, norm_mix, norm_moe, norm_final, hy_w_in, hy_b_in, hy_conv_w, hy_conv_b, hy_f_w1, hy_f_b1, hy_f_freq, hy_f_w2, hy_f_b2, hy_f_w3, hy_f_skip, hy_w_out, hy_b_out, rg_w_in, rg_b_in, rg_conv_w, rg_conv_b, rg_wa, rg_ba, rg_wx, rg_bx, rg_lambda, rg_w_out, rg_b_out, router_w, router_bias, moe_w1, moe_w3, moe_w2):
    nb_ctx, len_ctx, d = x_prompt.shape
    nb_lat, len_lat, _ = x_sample.shape
    depth = ada_w.shape[0]
    n_rg = rg_w_in.shape[0]
    d_rnn = rg_w_out.shape[1]
    tok_ctx = nb_ctx * len_ctx
    tiles_ctx = tok_ctx // TOKEN_TILE
    assert tok_ctx % TOKEN_TILE == 0 and TOKEN_TILE % len_ctx == 0 and TOKEN_TILE % GRID_W == 0
    assert len_lat % TOKEN_TILE == 0 and tok_ctx % len_lat == 0
    assert V7X_SUBLANES - (nb_lat + 1) >= 0
    lat_off = tok_ctx // len_lat

    x = jnp.concatenate([x_prompt.reshape(tok_ctx, d), x_sample.reshape(nb_lat * len_lat, d)], 0)
    cond = jnp.concatenate([c_ctx[None, :], c,
                            jnp.zeros((V7X_SUBLANES - 1 - nb_lat, d), F32)], axis=0)
    mods = _ada_modulation(cond, ada_w, ada_b)[:, :1 + nb_lat]
    ctx_h0 = jnp.zeros((nb_ctx, 2, d_rnn), F32)
    states = []

    for i in range(depth):
        mod = mods[i]
        j = i // 2
        if i % 2 == 0:
            u = _inproj(x, mod, norm_mix[i], hy_w_in[j].astype(BF16), hy_b_in[j], hy_conv_w[j],
                        hy_conv_b[j], n_ctx_tiles=tiles_ctx, ctx_len=len_ctx, lat_len=GRID_W,
                        plain_cols=0)
            filt_w = (hy_f_w1[j], hy_f_b1[j], hy_f_freq[j], hy_f_w2[j], hy_f_b2[j], hy_f_w3[j])
            y_ctx = _hyena_stream(u, 0, nb_ctx, len_ctx, filt_w, hy_f_skip[j], d)
            y_lat = _hyena_stream(u, lat_off, nb_lat, len_lat, filt_w, hy_f_skip[j], d)
            y = jnp.concatenate([y_ctx, y_lat], axis=0)
            x = _outproj(y, hy_w_out[j].astype(BF16), hy_b_out[j], mod, x, n_ctx_tiles=tiles_ctx)
        else:
            u = _inproj(x, mod, norm_mix[i], rg_w_in[j].astype(BF16), rg_b_in[j], rg_conv_w[j],
                        rg_conv_b[j], n_ctx_tiles=tiles_ctx, ctx_len=len_ctx, lat_len=GRID_W,
                        plain_cols=d_rnn)
            wa, wx = rg_wa[j].astype(BF16), rg_wx[j].astype(BF16)
            y_ctx, st = _rglru_stream(u, ctx_h0, 0, nb_ctx, len_ctx, wa, wx, rg_ba[j], rg_bx[j],
                                      rg_lambda[j])
            y_lat, _ = _rglru_stream(u, state_rglru[:, j], lat_off, nb_lat, len_lat, wa, wx,
                                     rg_ba[j], rg_bx[j], rg_lambda[j])
            states.append(st)
            y = jnp.concatenate([y_ctx, y_lat], axis=0)
            x = _outproj(y, rg_w_out[j].astype(BF16), rg_b_out[j], mod, x, n_ctx_tiles=tiles_ctx)
        x = _moe(x, mod, norm_moe[i], router_w, router_bias, moe_w1, moe_w3, moe_w2, i,
                 norm_final, n_ctx_tiles=tiles_ctx, final_norm=(i == depth - 1))

    y_prompt = x[:tok_ctx].reshape(nb_ctx, len_ctx, d)
    y_sample = x[tok_ctx:].reshape(nb_lat, len_lat, d)
    new_state = jnp.stack(states, axis=1).astype(x_prompt.dtype)
    return (y_prompt, y_sample, new_state)
```

```python
import functools
import math

import numpy as np
import jax
import jax.numpy as jnp
from jax import lax
from jax.experimental import pallas as pl
from jax.experimental.pallas import tpu as pltpu

F32 = jnp.float32
BF16 = jnp.bfloat16

GRID_W = 64
FILTER_BANDS = 16
FILTER_EPS = 1e-6
MIN_DECAY = math.log(1e-2) / 0.3
MAX_DECAY = math.log(1e-2) / 1.5
RG_C = 8.0
N_GROUPS = 4
TOP_K = 2
NORM_EPS = 1e-6

V7X_LANES = 128
V7X_SUBLANES = 8
V7X_VMEM_BYTES = 64 * 1024 * 1024

TOKEN_TILE = 512
COL_TILE = 512
MOE_ROWS = 256
DFT_ROWS = 512
DFT_SPLIT = 64
RG_COLS = 256
ROW_DMA_UNROLL = 8


def _params(sem, vmem_mb):
    return pltpu.CompilerParams(dimension_semantics=sem,
                                vmem_limit_bytes=vmem_mb * 1024 * 1024)


def _dot(a, b):
    return jnp.dot(a, b, preferred_element_type=F32)


def _split(x):
    hi = x.astype(BF16)
    lo = (x - hi.astype(F32)).astype(BF16)
    return hi, lo


def _dot3(a, b):
    ah, al = _split(a)
    bh, bl = _split(b)
    return _dot(ah, bh) + (_dot(ah, bl) + _dot(al, bh))


def _dot3_nt(a, b):
    dn = (((1,), (1,)), ((), ()))
    d = lambda x, y: lax.dot_general(x, y, dn, preferred_element_type=F32)
    ah, al = _split(a)
    bh, bl = _split(b)
    return d(ah, bh) + (d(ah, bl) + d(al, bh))


def _sigmoid(x):
    return 1.0 / (1.0 + jnp.exp(-x))


def _sigmoid_tanh(x):
    return 0.5 * jnp.tanh(0.5 * x) + 0.5


def _norm_mod(x, g, shift, scale):
    ms = jnp.mean(x * x, axis=-1, keepdims=True)
    return (x * lax.rsqrt(ms + NORM_EPS) * g) * (1.0 + scale) + shift


def _ada_kernel(c_ref, w_ref, b_ref, o_ref):
    c = c_ref[...]
    o_ref[...] = _dot3(c * _sigmoid(c), w_ref[...]) + b_ref[...]


def _ada_modulation(cond, ada_w, ada_b):
    depth, d, n = ada_w.shape
    tn = n // 4
    out = pl.pallas_call(
        _ada_kernel,
        out_shape=jax.ShapeDtypeStruct((depth, cond.shape[0], n), F32),
        grid=(depth, n // tn),
        in_specs=[pl.BlockSpec(cond.shape, lambda i, j: (0, 0)),
                  pl.BlockSpec((None, d, tn), lambda i, j: (i, 0, j)),
                  pl.BlockSpec((None, 1, tn), lambda i, j: (i, 0, j))],
        out_specs=pl.BlockSpec((None, cond.shape[0], tn), lambda i, j: (i, 0, j)),
        compiler_params=_params(("parallel", "parallel"), 40),
        name="ada_modulation",
    )(cond, ada_w, ada_b.reshape(depth, 1, n))
    return out.reshape(depth, cond.shape[0], 6, d)


def _inproj_kernel(x_ref, mod_ref, g_ref, w_ref, b_ref, cw_ref, cb_ref, o_ref, h_scr, *,
                   n_ctx_tiles, ctx_len, lat_len, width, plain_col_tiles):
    i = pl.program_id(0)
    j = pl.program_id(1)

    @pl.when(j == 0)
    def _():
        h = _norm_mod(x_ref[...], g_ref[...], mod_ref[0:1, :], mod_ref[1:2, :])
        h_scr[...] = h.astype(BF16)

    sub = max(ctx_len, lat_len)
    assert sub % ctx_len == 0 and sub % lat_len == 0 and x_ref.shape[0] % sub == 0
    blocks = [slice(s, s + sub) for s in range(0, x_ref.shape[0], sub)]

    def project(rows):
        return _dot(h_scr[rows, :], w_ref[...]) + b_ref[...]

    def conv():
        seg_mask = jnp.where(i < n_ctx_tiles, ctx_len - 1, lat_len - 1)
        pos = lax.broadcasted_iota(jnp.int32, (sub, 1), 0) & seg_mask
        pad_l = (width - 1) // 2
        for rows in blocks:
            u = project(rows)
            acc = jnp.zeros_like(u) + cb_ref[...]
            for k in range(width):
                off = k - pad_l
                if off == 0:
                    term = u
                else:
                    shifted = pltpu.roll(u, (-off) % sub, axis=0)
                    ok = (pos + off >= 0) & (pos + off <= seg_mask)
                    term = jnp.where(ok, shifted, 0.0)
                acc = acc + cw_ref[k:k + 1, :] * term
            o_ref[rows, :] = acc

    if plain_col_tiles == 0:
        conv()
    else:
        pl.when(j >= plain_col_tiles)(conv)

        @pl.when(j < plain_col_tiles)
        def _():
            for rows in blocks:
                o_ref[rows, :] = project(rows)


def _inproj(x, mod, g, w, b, conv_w, conv_b, *, n_ctx_tiles, ctx_len, lat_len, plain_cols):
    t, d = x.shape
    n = w.shape[1]
    tm, tn = TOKEN_TILE, COL_TILE
    width = conv_w.shape[0]
    plain_tiles = plain_cols // tn
    lat_tiles = (t // tm - n_ctx_tiles) // (mod.shape[0] - 1)

    def cond_of(i):
        return jnp.where(i < n_ctx_tiles, 0, 1 + (i - n_ctx_tiles) // lat_tiles)

    kern = functools.partial(_inproj_kernel, n_ctx_tiles=n_ctx_tiles, ctx_len=ctx_len,
                             lat_len=lat_len, width=width, plain_col_tiles=plain_tiles)
    return pl.pallas_call(
        kern,
        out_shape=jax.ShapeDtypeStruct((t, n), F32),
        grid=(t // tm, n // tn),
        in_specs=[pl.BlockSpec((tm, d), lambda i, j: (i, 0)),
                  pl.BlockSpec((None, 6, d), lambda i, j: (cond_of(i), 0, 0)),
                  pl.BlockSpec((1, d), lambda i, j: (0, 0)),
                  pl.BlockSpec((d, tn), lambda i, j: (0, j)),
                  pl.BlockSpec((1, tn), lambda i, j: (0, j)),
                  pl.BlockSpec((width, tn), lambda i, j: (0, jnp.maximum(j - plain_tiles, 0))),
                  pl.BlockSpec((1, tn), lambda i, j: (0, jnp.maximum(j - plain_tiles, 0)))],
        out_specs=pl.BlockSpec((tm, tn), lambda i, j: (i, j)),
        scratch_shapes=[pltpu.VMEM((tm, d), BF16)],
        compiler_params=_params(("parallel", "arbitrary"), 32),
        name="norm_inproj_conv",
    )(x, mod, g.reshape(1, d), w, b.reshape(1, n), conv_w, conv_b.reshape(1, -1))


def _outproj_kernel(yc_ref, yl_ref, w_ref, b_ref, mod_ref, x_ref, o_ref, *, n_ctx_tiles):
    i = pl.program_id(0)

    def run(y_ref):
        m = _dot(y_ref[...], w_ref[...]) + b_ref[...]
        o_ref[...] = x_ref[...] + mod_ref[2:3, :] * m

    pl.when(i < n_ctx_tiles)(lambda: run(yc_ref))
    pl.when(i >= n_ctx_tiles)(lambda: run(yl_ref))


def _outproj(y_ctx, y_lat, w, b, mod, x, *, n_ctx_tiles):
    t, d = x.shape
    tm = TOKEN_TILE
    lat_tiles = (t // tm - n_ctx_tiles) // (mod.shape[0] - 1)

    def cond_of(i):
        return jnp.where(i < n_ctx_tiles, 0, 1 + (i - n_ctx_tiles) // lat_tiles)

    return pl.pallas_call(
        functools.partial(_outproj_kernel, n_ctx_tiles=n_ctx_tiles),
        out_shape=jax.ShapeDtypeStruct((t, d), F32),
        grid=(t // tm,),
        in_specs=[pl.BlockSpec((tm, w.shape[0]), lambda i: (jnp.minimum(i, n_ctx_tiles - 1), 0)),
                  pl.BlockSpec((tm, w.shape[0]), lambda i: (jnp.maximum(i - n_ctx_tiles, 0), 0)),
                  pl.BlockSpec(w.shape, lambda i: (0, 0)),
                  pl.BlockSpec((1, d), lambda i: (0, 0)),
                  pl.BlockSpec((None, 6, d), lambda i: (cond_of(i), 0, 0)),
                  pl.BlockSpec((tm, d), lambda i: (i, 0))],
        out_specs=pl.BlockSpec((tm, d), lambda i: (i, 0)),
        input_output_aliases={5: 0},
        compiler_params=_params(("parallel",), 32),
        name="outproj_residual",
    )(y_ctx, y_lat, w, b.reshape(1, d), mod, x)


def _filter_kernel(z_ref, w1_ref, b1_ref, fr_ref, w2_ref, b2_ref, w3a_ref, w3b_ref, w3c_ref,
                   w3d_ref, t_ref, dl_ref, o_ref, *, orders):
    fr = fr_ref[...]
    h = jnp.sin(fr * (_dot3(z_ref[...], w1_ref[...]) + b1_ref[...]))
    h = jnp.sin(fr * (_dot3(h, w2_ref[...]) + b2_ref[...]))
    window = jnp.exp(-t_ref[...] * dl_ref[...])
    row = lax.broadcasted_iota(jnp.int32, (h.shape[0], 1), 0)
    w3 = ((w3a_ref, w3b_ref), (w3c_ref, w3d_ref))
    for o in range(orders):
        h_fwd = _dot3(h, w3[0][o][...]) * window
        h_bwd = jnp.where(row == 0, 0.0, _dot3(h, w3[1][o][...]) * window)
        norm = (jnp.sum(jnp.abs(h_fwd), axis=0, keepdims=True)
                + jnp.sum(jnp.abs(h_bwd), axis=0, keepdims=True) + FILTER_EPS)
        o_ref[2 * o] = h_fwd / norm
        o_ref[2 * o + 1] = h_bwd / norm


def _hyena_filter(length, f_w1, f_b1, f_freq, f_w2, f_b2, f_w3, d):
    emb, hid = f_w1.shape
    orders = f_w3.shape[1] // (2 * d)
    t = jnp.linspace(0.0, 1.0, length, dtype=F32)[:, None]
    w = (2.0 * math.pi / length) * jnp.arange(length, dtype=F32)[:, None]
    bands = jnp.linspace(1e-4, FILTER_BANDS - 1, FILTER_BANDS, dtype=F32)[None, :]
    z = jnp.concatenate([t, jnp.cos(bands * w), -jnp.sin(bands * w)], axis=-1)
    emb_pad = V7X_LANES
    z = jnp.pad(z, ((0, 0), (0, emb_pad - emb)))
    w1 = jnp.pad(f_w1, ((0, emb_pad - emb), (0, 0)))
    deltas = jnp.abs(jnp.linspace(MIN_DECAY, MAX_DECAY, d, dtype=F32))[None, :]
    td = 256
    nd = d // td
    full = lambda shape: pl.BlockSpec(shape, lambda j: (0,) * len(shape))
    w3_spec = lambda side, o: pl.BlockSpec((hid, td), lambda j: (0, (side * orders + o) * nd + j))
    assert orders == 2
    return pl.pallas_call(
        functools.partial(_filter_kernel, orders=orders),
        out_shape=jax.ShapeDtypeStruct((2 * orders, length, d), F32),
        grid=(nd,),
        in_specs=[full((length, emb_pad)), full((emb_pad, hid)), full((1, hid)), full((1, hid)),
                  full((hid, hid)), full((1, hid)),
                  w3_spec(0, 0), w3_spec(0, 1), w3_spec(1, 0), w3_spec(1, 1),
                  full((length, 1)), pl.BlockSpec((1, td), lambda j: (0, j))],
        out_specs=pl.BlockSpec((2 * orders, length, td), lambda j: (0, 0, j)),
        compiler_params=_params(("parallel",), 48),
        name="hyena_filter",
    )(z, w1, f_b1.reshape(1, hid), f_freq.reshape(1, hid), f_w2, f_b2.reshape(1, hid),
      f_w3, f_w3, f_w3, f_w3, t, deltas)


def _dft_matrices(length):
    n = 2 * length
    f0n, f1n = DFT_SPLIT, length // DFT_SPLIT
    tt = np.arange(length, dtype=np.int64)[None, :]
    ang_a = 2.0 * np.pi * ((DFT_SPLIT * np.arange(f1n, dtype=np.int64)[:, None] * tt) % n) / n
    ang_b = 2.0 * np.pi * ((np.arange(f0n, dtype=np.int64)[:, None] * tt) % n) / n
    ca, sa = (jnp.asarray(f(ang_a), F32)[:, None, :] for f in (np.cos, np.sin))
    cb, sb = (jnp.asarray(f(ang_b), F32)[None, :, :] for f in (np.cos, np.sin))
    cos_m = (ca * cb - sa * sb).reshape(length, length)
    sin_m = (sa * cb + ca * sb).reshape(length, length)
    nyq = jnp.asarray(1.0 - 2.0 * (np.arange(length) % 2), F32)[None, :]
    f_is0 = (jnp.arange(length) == 0)[:, None]
    im_m = jnp.where(f_is0, nyq, -sin_m)
    th = DFT_ROWS // 2
    fwd = jnp.stack([cos_m.reshape(length // th, th, length),
                     im_m.reshape(length // th, th, length)], axis=1).reshape(n, length)
    scale = jnp.where(f_is0, 1.0 / n, 2.0 / n)
    inv = jnp.stack([(cos_m * scale).reshape(length // th, th, length),
                     (im_m * scale).reshape(length // th, th, length)], axis=1).reshape(n, length).T
    return fwd.astype(BF16), inv.astype(BF16)


def _spec_kernel(a_ref, hf_ref, hb_ref, o_ref, hf_scr, hb_scr):
    m = pl.program_id(2)

    @pl.when(m == 0)
    def _():
        hf_scr[...] = hf_ref[...].astype(BF16)
        hb_scr[...] = hb_ref[...].astype(BF16)

    a = a_ref[...]
    ff = _dot(a, hf_scr[...])
    fb = _dot(a, hb_scr[...])
    th = ff.shape[0] // 2
    row = lax.broadcasted_iota(jnp.int32, (th, 1), 0)
    dc = (row == 0) & (m == 0)
    o_ref[0:th, :] = ff[:th] + fb[:th]
    o_ref[th:, :] = jnp.where(dc, ff[th:] + fb[th:], ff[th:] - fb[th:])


def _filter_spectrum(fwd, filt):
    n2, length, d = filt.shape
    orders = n2 // 2
    n = 2 * length
    tm, tc = DFT_ROWS, 512
    filt2 = filt.reshape(n2 * length, d)
    return pl.pallas_call(
        _spec_kernel,
        out_shape=jax.ShapeDtypeStruct((orders * n, d), F32),
        grid=(orders, d // tc, n // tm),
        in_specs=[pl.BlockSpec((tm, length), lambda o, c, m: (m, 0)),
                  pl.BlockSpec((length, tc), lambda o, c, m: (2 * o, c)),
                  pl.BlockSpec((length, tc), lambda o, c, m: (2 * o + 1, c))],
        out_specs=pl.BlockSpec((tm, tc), lambda o, c, m: (o * (n // tm) + m, c)),
        scratch_shapes=[pltpu.VMEM((length, tc), BF16), pltpu.VMEM((length, tc), BF16)],
        compiler_params=_params(("parallel", "parallel", "arbitrary"), 48),
        name="hyena_filter_spectrum",
    )(fwd, filt2, filt2)


def _dft_fwd_kernel(a_ref, z_ref, k_ref, o_ref, z_scr):
    m = pl.program_id(1)

    @pl.when(m == 0)
    def _():
        z_scr[...] = z_ref[...].astype(BF16)

    acc = _dot(a_ref[...], z_scr[...])
    th = acc.shape[0] // 2
    xr, xi = acc[:th], acc[th:]
    kr, ki = k_ref[0:th, :], k_ref[th:, :]
    row = lax.broadcasted_iota(jnp.int32, (th, 1), 0)
    dc = (row == 0) & (m == 0)
    o_ref[0:th, :] = (xr * kr - jnp.where(dc, 0.0, xi * ki)).astype(o_ref.dtype)
    o_ref[th:, :] = jnp.where(dc, xi * ki, xr * ki + xi * kr).astype(o_ref.dtype)


def _dft_forward(fwd, src, spec, *, nb, length, row_off, col_blk, order):
    n = 2 * length
    d = spec.shape[1]
    tm = DFT_ROWS
    nt = n // tm
    return pl.pallas_call(
        _dft_fwd_kernel,
        out_shape=jax.ShapeDtypeStruct((nb * n, d), BF16),
        grid=(nb, nt),
        in_specs=[pl.BlockSpec((tm, length), lambda b, m: (m, 0)),
                  pl.BlockSpec((length, d), lambda b, m: (row_off + b, col_blk)),
                  pl.BlockSpec((tm, d), lambda b, m: (order * nt + m, 0))],
        out_specs=pl.BlockSpec((tm, d), lambda b, m: (b * nt + m, 0)),
        scratch_shapes=[pltpu.VMEM((length, d), BF16)],
        compiler_params=_params(("parallel", "arbitrary"), 48),
        name="hyena_dft_forward",
    )(fwd, src, spec)


def _dft_inv_kernel(a_ref, p_ref, v_ref, g_ref, s_ref, o_ref):
    y = _dot(a_ref[...], p_ref[...])
    v = v_ref[...]
    o_ref[...] = (g_ref[...] * (y + v * s_ref[...])).astype(o_ref.dtype)


def _dft_inverse(inv, prod, vsrc, gsrc, skip, *, nb, length, v_off, v_col, g_off, g_col,
                 out_dtype):
    n = 2 * length
    d = prod.shape[1]
    tm = min(DFT_ROWS, length)
    nt = length // tm
    return pl.pallas_call(
        _dft_inv_kernel,
        out_shape=jax.ShapeDtypeStruct((nb * length, d), out_dtype),
        grid=(nb, nt),
        in_specs=[pl.BlockSpec((tm, n), lambda b, m: (m, 0)),
                  pl.BlockSpec((n, d), lambda b, m: (b, 0)),
                  pl.BlockSpec((tm, d), lambda b, m: ((v_off + b) * nt + m, v_col)),
                  pl.BlockSpec((tm, d), lambda b, m: ((g_off + b) * nt + m, g_col)),
                  pl.BlockSpec((1, d), lambda b, m: (0, 0))],
        out_specs=pl.BlockSpec((tm, d), lambda b, m: (b * nt + m, 0)),
        compiler_params=_params(("parallel", "arbitrary"), 48),
        name="hyena_dft_inverse",
    )(inv, prod, vsrc, gsrc, skip.reshape(1, d))


def _spectral_product(acc, k_ref, order):
    th = acc.shape[0] // 2
    xr, xi = acc[:th], acc[th:]
    kr, ki = k_ref[order, 0:th, :], k_ref[order, th:, :]
    dc = lax.broadcasted_iota(jnp.int32, (th, 1), 0) == 0
    pr = xr * kr - jnp.where(dc, 0.0, xi * ki)
    pi = jnp.where(dc, xi * ki, xr * ki + xi * kr)
    return jnp.concatenate([pr, pi], axis=0).astype(BF16)


def _hyena_short_kernel(u_ref, f_ref, i_ref, k_ref, s_ref, o_ref):
    d = o_ref.shape[1]
    fwd, inv = f_ref[...], i_ref[...]
    z = u_ref[:, 0:d]
    for order in range(2):
        p = _spectral_product(_dot(fwd, z.astype(BF16)), k_ref, order)
        gate = u_ref[:, (order + 1) * d:(order + 2) * d]
        z = gate * (_dot(inv, p) + z * s_ref[order:order + 1, :])
    o_ref[...] = z.astype(o_ref.dtype)


def _hyena_short(u, fwd, inv, spec, f_skip, *, nb, length):
    n = 2 * length
    d = spec.shape[1]
    assert n == DFT_ROWS and spec.shape[0] == 2 * n
    return pl.pallas_call(
        _hyena_short_kernel,
        out_shape=jax.ShapeDtypeStruct((nb * length, d), BF16),
        grid=(nb,),
        in_specs=[pl.BlockSpec((length, 3 * d), lambda b: (b, 0)),
                  pl.BlockSpec((n, length), lambda b: (0, 0)),
                  pl.BlockSpec((length, n), lambda b: (0, 0)),
                  pl.BlockSpec((2, n, d), lambda b: (0, 0, 0)),
                  pl.BlockSpec((2, d), lambda b: (0, 0))],
        out_specs=pl.BlockSpec((length, d), lambda b: (b, 0)),
        compiler_params=_params(("parallel",), 48),
        name="hyena_short_sequences",
    )(u, fwd, inv, spec.reshape(2, n, d), f_skip)


def _hyena_stream(u, row_off, nb, length, filt_w, f_skip, d):
    fwd, inv = _dft_matrices(length)
    filt = _hyena_filter(length, *filt_w, d)
    spec = _filter_spectrum(fwd, filt)
    if 2 * length == DFT_ROWS:
        assert row_off == 0
        return _hyena_short(u, fwd, inv, spec, f_skip, nb=nb, length=length)
    p1 = _dft_forward(fwd, u, spec, nb=nb, length=length, row_off=row_off, col_blk=0, order=0)
    z1 = _dft_inverse(inv, p1, u, u, f_skip[0], nb=nb, length=length, v_off=row_off, v_col=0,
                      g_off=row_off, g_col=1, out_dtype=F32)
    p2 = _dft_forward(fwd, z1, spec, nb=nb, length=length, row_off=0, col_blk=0, order=1)
    return _dft_inverse(inv, p2, z1, u, f_skip[1], nb=nb, length=length, v_off=0, v_col=0,
                        g_off=row_off, g_col=2, out_dtype=BF16)


def _gelu_tanh(x):
    return 0.5 * x * (1.0 + jnp.tanh(math.sqrt(2.0 / math.pi) * (x + 0.044715 * (x * x * x))))


def _log1p(e):
    u = 1.0 + e
    d = u - 1.0
    return jnp.where(d == 0.0, e, jnp.log(u) * (e / jnp.where(d == 0.0, 1.0, d)))


def _rglru_kernel(gate_ref, rec_ref, wa_ref, wx_ref, ba_ref, bx_ref, lam_ref, h0_ref,
                  y_ref, st_ref, a_scr, b_scr):
    length, cols = rec_ref.shape
    groups = length // V7X_SUBLANES
    rec = rec_ref[...]
    rec16 = rec.astype(BF16)
    pos = lax.broadcasted_iota(jnp.int32, (groups, V7X_SUBLANES, cols), 1)

    for d in range(2):
        r_gate = _sigmoid_tanh(_dot(rec16, wa_ref[d]) + ba_ref[d])
        i_gate = _sigmoid_tanh(_dot(rec16, wx_ref[d]) + bx_ref[d])
        nlam = -lam_ref[d]
        softplus = jnp.maximum(nlam, 0.0) + _log1p(jnp.exp(-jnp.abs(nlam)))
        log_a = (-RG_C * softplus) * r_gate
        a = jnp.exp(log_a).reshape(groups, V7X_SUBLANES, cols)
        th = jnp.tanh(log_a)
        b = (jnp.sqrt(-2.0 * th / (1.0 - th)) * (i_gate * rec)).reshape(
            groups, V7X_SUBLANES, cols)
        for s in (1, 2, 4):
            if d == 0:
                a_sh = pltpu.roll(a, s, axis=1)
                b_sh = pltpu.roll(b, s, axis=1)
                live = pos >= s
            else:
                a_sh = pltpu.roll(a, V7X_SUBLANES - s, axis=1)
                b_sh = pltpu.roll(b, V7X_SUBLANES - s, axis=1)
                live = pos < V7X_SUBLANES - s
            b = jnp.where(live, a * b_sh, 0.0) + b
            a = jnp.where(live, a * a_sh, a)
        a_scr[d] = a.reshape(length, cols)
        b_scr[d] = b.reshape(length, cols)

    def step(g, carry):
        cf, cb = carry
        rf = pl.multiple_of(g * V7X_SUBLANES, V7X_SUBLANES)
        rb = pl.multiple_of((groups - 1 - g) * V7X_SUBLANES, V7X_SUBLANES)
        hf = a_scr[0, pl.ds(rf, V7X_SUBLANES), :] * cf + b_scr[0, pl.ds(rf, V7X_SUBLANES), :]
        hb = a_scr[1, pl.ds(rb, V7X_SUBLANES), :] * cb + b_scr[1, pl.ds(rb, V7X_SUBLANES), :]
        b_scr[0, pl.ds(rf, V7X_SUBLANES), :] = hf
        b_scr[1, pl.ds(rb, V7X_SUBLANES), :] = hb
        cf = jnp.broadcast_to(hf[V7X_SUBLANES - 1:V7X_SUBLANES, :], hf.shape)
        cb = jnp.broadcast_to(hb[0:1, :], hb.shape)
        return cf, cb

    init = (jnp.broadcast_to(h0_ref[0], (V7X_SUBLANES, cols)),
            jnp.broadcast_to(h0_ref[1], (V7X_SUBLANES, cols)))
    cf, cb = lax.fori_loop(0, groups, step, init)
    st_ref[0] = cf[0:1, :]
    st_ref[1] = cb[0:1, :]
    y_ref[...] = ((b_scr[0] + b_scr[1]) * _gelu_tanh(gate_ref[...])).astype(y_ref.dtype)


def _rglru_stream(u, h0, row_off, nb, length, wa, wx, ba, bx, lam):
    dr = u.shape[1] // 2
    tc = RG_COLS
    nc = dr // tc
    vec = lambda a: a.reshape(2, 1, dr)
    vec_spec = pl.BlockSpec((2, 1, tc), lambda b, c: (0, 0, c))
    w_spec = pl.BlockSpec((2, None, tc, tc), lambda b, c: (0, c, 0, 0))
    y, st = pl.pallas_call(
        _rglru_kernel,
        out_shape=(jax.ShapeDtypeStruct((nb * length, dr), BF16),
                   jax.ShapeDtypeStruct((nb, 2, 1, dr), F32)),
        grid=(nb, nc),
        in_specs=[pl.BlockSpec((length, tc), lambda b, c: (row_off + b, c)),
                  pl.BlockSpec((length, tc), lambda b, c: (row_off + b, nc + c)),
                  w_spec, w_spec, vec_spec, vec_spec, vec_spec,
                  pl.BlockSpec((None, 2, 1, tc), lambda b, c: (b, 0, 0, c))],
        out_specs=(pl.BlockSpec((length, tc), lambda b, c: (b, c)),
                   pl.BlockSpec((None, 2, 1, tc), lambda b, c: (b, 0, 0, c))),
        scratch_shapes=[pltpu.VMEM((2, length, tc), F32), pltpu.VMEM((2, length, tc), F32)],
        compiler_params=_params(("parallel", "parallel"), 48),
        name="rglru_scan",
    )(u, u, wa, wx, vec(ba), vec(bx), vec(lam), h0.reshape(nb, 2, 1, dr))
    return y, st.reshape(nb, 2, dr)


def _router_kernel(x_ref, mod_ref, g_ref, rwt_ref, bias_ref, h_ref, idx_ref, gate_ref,
                   rank_ref, cnt_ref, tri_scr, carry_scr, *, n_experts):
    i = pl.program_id(0)
    tm = x_ref.shape[0]
    per_group = n_experts // N_GROUPS

    @pl.when(i == 0)
    def _():
        r = lax.broadcasted_iota(jnp.int32, (tm, tm), 0)
        c = lax.broadcasted_iota(jnp.int32, (tm, tm), 1)
        tri_scr[...] = jnp.where(r < c, 1.0, 0.0).astype(BF16)
        carry_scr[...] = jnp.zeros_like(carry_scr)

    h = _norm_mod(x_ref[...], g_ref[...], mod_ref[3:4, :], mod_ref[4:5, :])
    h_ref[...] = h
    logits = _dot3_nt(rwt_ref[...], h)
    p = jnp.exp(logits - jnp.max(logits, axis=0, keepdims=True))
    scores = p / jnp.sum(p, axis=0, keepdims=True)
    sel = scores + bias_ref[...]
    rows = [sel[e:e + 1, :] for e in range(n_experts)]

    best_val = None
    for gi in range(N_GROUPS):
        v = rows[gi * per_group:(gi + 1) * per_group]
        pair = None
        for a in range(per_group):
            for b in range(a + 1, per_group):
                s = v[a] + v[b]
                pair = s if pair is None else jnp.maximum(pair, s)
        if best_val is None:
            best_val, best_grp = pair, jnp.zeros_like(pair, dtype=jnp.int32)
        else:
            take = pair > best_val
            best_val = jnp.where(take, pair, best_val)
            best_grp = jnp.where(take, gi, best_grp)

    neg = jnp.float32(-jnp.inf)
    masked = [jnp.where(best_grp == e // per_group, rows[e], neg) for e in range(n_experts)]

    def argmax_first(vals):
        bv, bi = vals[0], jnp.zeros_like(best_grp)
        for e in range(1, n_experts):
            take = vals[e] > bv
            bv = jnp.where(take, vals[e], bv)
            bi = jnp.where(take, e, bi)
        return bi

    idx0 = argmax_first(masked)
    idx1 = argmax_first([jnp.where(idx0 == e, neg, masked[e]) for e in range(n_experts)])

    e_iota = lax.broadcasted_iota(jnp.int32, (n_experts, tm), 0)
    hit0 = e_iota == idx0
    hit1 = e_iota == idx1
    g0 = jnp.sum(jnp.where(hit0, scores, 0.0), axis=0, keepdims=True)
    g1 = jnp.sum(jnp.where(hit1, scores, 0.0), axis=0, keepdims=True)
    gsum = g0 + g1
    onehot = jnp.where(hit0 | hit1, 1.0, 0.0)
    before = _dot(onehot.astype(BF16), tri_scr[...]) + carry_scr[:, 0:1]
    r0 = jnp.sum(jnp.where(hit0, before, 0.0), axis=0, keepdims=True)
    r1 = jnp.sum(jnp.where(hit1, before, 0.0), axis=0, keepdims=True)
    idx_ref[0:1, :] = idx0
    idx_ref[1:2, :] = idx1
    gate_ref[0:1, :] = g0 / gsum
    gate_ref[1:2, :] = g1 / gsum
    rank_ref[0:1, :] = r0.astype(jnp.int32)
    rank_ref[1:2, :] = r1.astype(jnp.int32)
    carry_scr[...] = carry_scr[...] + jnp.sum(onehot, axis=1, keepdims=True)
    cnt_ref[...] = carry_scr[...]


def _router(x, mod, g, router_w, router_bias, *, n_ctx_tiles):
    t, d = x.shape
    ne = router_w.shape[1]
    tm = TOKEN_TILE
    lat_tiles = (t // tm - n_ctx_tiles) // (mod.shape[0] - 1)

    def cond_of(i):
        return jnp.where(i < n_ctx_tiles, 0, 1 + (i - n_ctx_tiles) // lat_tiles)

    row2 = pl.BlockSpec((TOP_K, tm), lambda i: (0, i))
    return pl.pallas_call(
        functools.partial(_router_kernel, n_experts=ne),
        out_shape=(jax.ShapeDtypeStruct((t, d), F32),
                   jax.ShapeDtypeStruct((TOP_K, t), jnp.int32),
                   jax.ShapeDtypeStruct((TOP_K, t), F32),
                   jax.ShapeDtypeStruct((TOP_K, t), jnp.int32),
                   jax.ShapeDtypeStruct((ne, V7X_LANES), F32)),
        grid=(t // tm,),
        in_specs=[pl.BlockSpec((tm, d), lambda i: (i, 0)),
                  pl.BlockSpec((None, 6, d), lambda i: (cond_of(i), 0, 0)),
                  pl.BlockSpec((1, d), lambda i: (0, 0)),
                  pl.BlockSpec((ne, d), lambda i: (0, 0)),
                  pl.BlockSpec((ne, 1), lambda i: (0, 0))],
        out_specs=(pl.BlockSpec((tm, d), lambda i: (i, 0)), row2, row2, row2,
                   pl.BlockSpec((ne, V7X_LANES), lambda i: (0, 0))),
        scratch_shapes=[pltpu.VMEM((tm, tm), BF16), pltpu.VMEM((ne, V7X_LANES), F32)],
        compiler_params=_params(("arbitrary",), 32),
        name="moe_router",
    )(x, mod, g.reshape(1, d), router_w.T, router_bias.reshape(ne, 1))


def _row_copy(src, src_row, dst, dst_row, sem):
    return pltpu.make_async_copy(src.at[pl.ds(src_row, 1)], dst.at[pl.ds(dst_row, 1)], sem)


def _dispatch_kernel(pad_row_ref, pad_len_ref, used_ref, dest_ref, h_ref, xs_out, zero_scr, sem,
                     pad_sem):
    tm = h_ref.shape[0]
    i = pl.program_id(0)
    n_experts = pad_row_ref.shape[0]
    zrows = zero_scr.shape[0]

    @pl.when(i == 0)
    def _():
        zero_scr[...] = jnp.zeros_like(zero_scr)

    def zero_copy(wanted, start, size):
        start = jnp.where(wanted, start, 0)
        if size >= V7X_SUBLANES:
            start = pl.multiple_of(start, V7X_SUBLANES)
        return wanted, pltpu.make_async_copy(zero_scr.at[pl.ds(0, size)],
                                             xs_out.at[pl.ds(start, size)], pad_sem)

    e = jnp.minimum(i, n_experts - 1)
    pad_len = jnp.where(i < n_experts, pad_len_ref[e], 0)
    pad_row = pad_row_ref[e]
    head = pad_len & (V7X_SUBLANES - 1)
    pad_copies = [zero_copy(r < head, pad_row + r, 1) for r in range(V7X_SUBLANES - 1)]
    size = zrows
    while size >= V7X_SUBLANES:
        start = pad_row + head + ((pad_len - head) & ~(2 * size - 1))
        pad_copies.append(zero_copy((pad_len & size) != 0, start, size))
        size //= 2
    spare = used_ref[0] + i
    has_spare = (i < n_experts) & (spare < xs_out.shape[0] // MOE_ROWS)
    for part in range(MOE_ROWS // zrows):
        pad_copies.append(zero_copy(has_spare, spare * MOE_ROWS + part * zrows, zrows))
    for wanted, copy in pad_copies:
        pl.when(wanted)(copy.start)

    def issue(g, c):
        for r in range(ROW_DMA_UNROLL):
            t = g * ROW_DMA_UNROLL + r
            for k in range(TOP_K):
                _row_copy(h_ref, t, xs_out, dest_ref[TOP_K * t + k], sem).start(priority=k)
        return c

    def drain(g, c):
        for _ in range(ROW_DMA_UNROLL * TOP_K):
            _row_copy(h_ref, 0, xs_out, 0, sem).wait()
        return c

    lax.fori_loop(0, tm // ROW_DMA_UNROLL, issue, 0)
    lax.fori_loop(0, tm // ROW_DMA_UNROLL, drain, 0)
    for wanted, copy in pad_copies:
        pl.when(wanted)(copy.wait)


def _dispatch(h, dest, pad_row, pad_len, n_used, n_slots):
    t, d = h.shape
    tm = TOKEN_TILE
    assert t // tm >= pad_row.shape[0] and n_slots % MOE_ROWS == 0
    return pl.pallas_call(
        _dispatch_kernel,
        out_shape=jax.ShapeDtypeStruct((n_slots, d), h.dtype),
        grid_spec=pltpu.PrefetchScalarGridSpec(
            num_scalar_prefetch=3,
            grid=(t // tm,),
            in_specs=[pl.BlockSpec((TOP_K * tm,), lambda i, pr, pn, nu: (i,),
                                   memory_space=pltpu.SMEM),
                      pl.BlockSpec((tm, d), lambda i, pr, pn, nu: (i, 0))],
            out_specs=pl.BlockSpec(memory_space=pl.ANY),
            scratch_shapes=[pltpu.VMEM((MOE_ROWS // 2, d), h.dtype),
                            pltpu.SemaphoreType.DMA, pltpu.SemaphoreType.DMA]),
        compiler_params=_params(("arbitrary",), 32),
        name="moe_dispatch",
    )(pad_row, pad_len, n_used, dest.reshape(-1), h)


def _experts_kernel(be_ref, nu_ref, xs_ref, w1_ref, w3_ref, w2_ref, ys_ref, w1_s, w3_s, w2_s):
    i = pl.program_id(0)
    e = be_ref[i]
    e_prev = be_ref[jnp.maximum(i - 1, 0)]

    @pl.when((i == 0) | (e != e_prev))
    def _():
        w1_s[...] = w1_ref[...].astype(BF16)
        w3_s[...] = w3_ref[...].astype(BF16)
        w2_s[...] = w2_ref[...].astype(BF16)

    @pl.when(i < nu_ref[0])
    def _():
        x = xs_ref[...].astype(BF16)
        h1 = _dot(x, w1_s[...])
        h3 = _dot(x, w3_s[...])
        act = (h1 * _sigmoid(h1)) * h3
        ys_ref[...] = _dot(act.astype(BF16), w2_s[...])

    @pl.when(i >= nu_ref[0])
    def _():
        ys_ref[...] = jnp.zeros_like(ys_ref)


def _experts(xs, block_e, n_used, w1, w3, w2, layer):
    n_slots, d = xs.shape
    de = w1.shape[3]
    tm = MOE_ROWS
    wspec = lambda shape: pl.BlockSpec((None, None) + shape,
                                       lambda i, be, nu: (layer, be[i], 0, 0))
    return pl.pallas_call(
        _experts_kernel,
        out_shape=jax.ShapeDtypeStruct((n_slots, d), F32),
        grid_spec=pltpu.PrefetchScalarGridSpec(
            num_scalar_prefetch=2,
            grid=(n_slots // tm,),
            in_specs=[pl.BlockSpec((tm, d), lambda i, be, nu: (jnp.minimum(i, nu[0] - 1), 0)),
                      wspec((d, de)), wspec((d, de)), wspec((de, d))],
            out_specs=pl.BlockSpec((tm, d), lambda i, be, nu: (i, 0)),
            scratch_shapes=[pltpu.VMEM((d, de), BF16), pltpu.VMEM((d, de), BF16),
                            pltpu.VMEM((de, d), BF16)]),
        compiler_params=_params(("arbitrary",), 52),
        name="moe_experts",
    )(block_e, n_used, xs, w1, w3, w2)


def _combine_kernel(dest_ref, x_ref, gate_ref, mod_ref, gf_ref, ys_hbm, *rest, n_ctx_tiles,
                    final_norm):
    buf0, buf1, sem = rest[-3:]
    tm = x_ref.shape[0]
    bufs = (buf0, buf1)

    def issue(g, c):
        for r in range(ROW_DMA_UNROLL):
            t = g * ROW_DMA_UNROLL + r
            for k in range(TOP_K):
                _row_copy(ys_hbm, dest_ref[TOP_K * t + k], bufs[k], t, sem).start(priority=k)
        return c

    def drain(g, c):
        for r in range(ROW_DMA_UNROLL):
            for k in range(TOP_K):
                _row_copy(ys_hbm, 0, bufs[k], 0, sem).wait()
        return c

    lax.fori_loop(0, tm // ROW_DMA_UNROLL, issue, 0)
    lax.fori_loop(0, tm // ROW_DMA_UNROLL, drain, 0)
    gate = gate_ref[...]
    m = gate[:, 0:1] * buf0[...] + gate[:, 1:2] * buf1[...]
    x = x_ref[...] + mod_ref[5:6, :] * m
    if not final_norm:
        rest[0][...] = x
        return
    ms = jnp.mean(x * x, axis=-1, keepdims=True)
    x = x * lax.rsqrt(ms + NORM_EPS) * gf_ref[...]
    i = pl.program_id(0)

    @pl.when(i < n_ctx_tiles)
    def _():
        rest[0][...] = x

    @pl.when(i >= n_ctx_tiles)
    def _():
        rest[1][...] = x


def _combine(x, ys, dest, gates, mod, g_final, *, n_ctx_tiles, final_norm):
    t, d = x.shape
    tm = TOKEN_TILE
    lat_tiles = (t // tm - n_ctx_tiles) // (mod.shape[0] - 1)

    def cond_of(i):
        return jnp.where(i < n_ctx_tiles, 0, 1 + (i - n_ctx_tiles) // lat_tiles)

    if final_norm:
        out_shape = (jax.ShapeDtypeStruct((n_ctx_tiles * tm, d), F32),
                     jax.ShapeDtypeStruct((t - n_ctx_tiles * tm, d), F32))
        out_specs = (pl.BlockSpec((tm, d), lambda i: (jnp.minimum(i, n_ctx_tiles - 1), 0)),
                     pl.BlockSpec((tm, d), lambda i: (jnp.maximum(i - n_ctx_tiles, 0), 0)))
        aliases = {}
    else:
        out_shape = jax.ShapeDtypeStruct((t, d), F32)
        out_specs = pl.BlockSpec((tm, d), lambda i: (i, 0))
        aliases = {1: 0}
    return pl.pallas_call(
        functools.partial(_combine_kernel, n_ctx_tiles=n_ctx_tiles, final_norm=final_norm),
        out_shape=out_shape,
        grid=(t // tm,),
        in_specs=[pl.BlockSpec((TOP_K * tm,), lambda i: (i,), memory_space=pltpu.SMEM),
                  pl.BlockSpec((tm, d), lambda i: (i, 0)),
                  pl.BlockSpec((tm, TOP_K), lambda i: (i, 0)),
                  pl.BlockSpec((None, 6, d), lambda i: (cond_of(i), 0, 0)),
                  pl.BlockSpec((1, d), lambda i: (0, 0)),
                  pl.BlockSpec(memory_space=pl.ANY)],
        out_specs=out_specs,
        scratch_shapes=[pltpu.VMEM((tm, d), F32), pltpu.VMEM((tm, d), F32),
                        pltpu.SemaphoreType.DMA],
        input_output_aliases=aliases,
        compiler_params=_params(("arbitrary",), 40),
        name="moe_combine",
    )(dest.reshape(-1), x, gates, mod, g_final.reshape(1, d), ys)


def _moe(x, mod, g, router_w, router_bias, w1, w3, w2, layer, g_final, *, n_ctx_tiles,
         final_norm):
    t, d = x.shape
    ne = router_w.shape[1]
    h, idx, gates, rank, cnt = _router(x, mod, g, router_w, router_bias, n_ctx_tiles=n_ctx_tiles)
    counts = cnt[:, 0].astype(jnp.int32)
    padded = (counts + MOE_ROWS - 1) // MOE_ROWS * MOE_ROWS
    pad_end = jnp.cumsum(padded)
    pad_start = pad_end - padded
    e_ids = jnp.arange(ne, dtype=jnp.int32)
    start_of = jnp.sum(jnp.where(idx[None] == e_ids[:, None, None],
                                 pad_start[:, None, None], 0), axis=0)
    dest = (start_of + rank).T
    n_blocks = -(-(t * TOP_K) // MOE_ROWS) + ne
    block_start = jnp.arange(n_blocks, dtype=jnp.int32) * MOE_ROWS
    block_e = jnp.minimum(jnp.sum(block_start[:, None] >= pad_end[None, :], axis=1),
                          ne - 1).astype(jnp.int32)
    n_used = (pad_end[-1:] // MOE_ROWS).astype(jnp.int32)
    xs = _dispatch(h, dest, pad_start + counts, padded - counts, n_used, n_blocks * MOE_ROWS)
    ys = _experts(xs, block_e, n_used, w1, w3, w2, layer)
    return _combine(x, ys, dest, gates.T, mod, g_final, n_ctx_tiles=n_ctx_tiles,
                    final_norm=final_norm)


def kernel(x_prompt, x_sample, state_rglru, c, c_ctx, ada_w, ada_b, norm_mix, norm_moe, norm_final, hy_w_in, hy_b_in, hy_conv_w, hy_conv_b, hy_f_w1, hy_f_b1, hy_f_freq, hy_f_w2, hy_f_b2, hy_f_w3, hy_f_skip, hy_w_out, hy_b_out, rg_w_in, rg_b_in, rg_conv_w, rg_conv_b, rg_wa, rg_ba, rg_wx, rg_bx, rg_lambda, rg_w_out, rg_b_out, router_w, router_bias, moe_w1, moe_w3, moe_w2):
    nb_ctx, len_ctx, d = x_prompt.shape
    nb_lat, len_lat, _ = x_sample.shape
    depth = ada_w.shape[0]
    n_rg = rg_w_in.shape[0]
    d_rnn = rg_w_out.shape[1]
    tok_ctx = nb_ctx * len_ctx
    tiles_ctx = tok_ctx // TOKEN_TILE
    assert tok_ctx % TOKEN_TILE == 0 and TOKEN_TILE % len_ctx == 0 and TOKEN_TILE % GRID_W == 0
    assert len_lat % TOKEN_TILE == 0 and tok_ctx % len_lat == 0
    assert V7X_SUBLANES - (nb_lat + 1) >= 0
    lat_off = tok_ctx // len_lat

    x = jnp.concatenate([x_prompt.reshape(tok_ctx, d), x_sample.reshape(nb_lat * len_lat, d)], 0)
    cond = jnp.concatenate([c_ctx[None, :], c,
                            jnp.zeros((V7X_SUBLANES - 1 - nb_lat, d), F32)], axis=0)
    mods = _ada_modulation(cond, ada_w, ada_b)[:, :1 + nb_lat]
    ctx_h0 = jnp.zeros((nb_ctx, 2, d_rnn), F32)
    states = []

    for i in range(depth):
        mod = mods[i]
        j = i // 2
        if i % 2 == 0:
            u = _inproj(x, mod, norm_mix[i], hy_w_in[j].astype(BF16), hy_b_in[j], hy_conv_w[j],
                        hy_conv_b[j], n_ctx_tiles=tiles_ctx, ctx_len=len_ctx, lat_len=GRID_W,
                        plain_cols=0)
            filt_w = (hy_f_w1[j], hy_f_b1[j], hy_f_freq[j], hy_f_w2[j], hy_f_b2[j], hy_f_w3[j])
            y_ctx = _hyena_stream(u, 0, nb_ctx, len_ctx, filt_w, hy_f_skip[j], d)
            y_lat = _hyena_stream(u, lat_off, nb_lat, len_lat, filt_w, hy_f_skip[j], d)
            x = _outproj(y_ctx, y_lat, hy_w_out[j].astype(BF16), hy_b_out[j], mod, x,
                         n_ctx_tiles=tiles_ctx)
        else:
            u = _inproj(x, mod, norm_mix[i], rg_w_in[j].astype(BF16), rg_b_in[j], rg_conv_w[j],
                        rg_conv_b[j], n_ctx_tiles=tiles_ctx, ctx_len=len_ctx, lat_len=GRID_W,
                        plain_cols=d_rnn)
            wa, wx = rg_wa[j].astype(BF16), rg_wx[j].astype(BF16)
            y_ctx, st = _rglru_stream(u, ctx_h0, 0, nb_ctx, len_ctx, wa, wx, rg_ba[j], rg_bx[j],
                                      rg_lambda[j])
            y_lat, _ = _rglru_stream(u, state_rglru[:, j], lat_off, nb_lat, len_lat, wa, wx,
                                     rg_ba[j], rg_bx[j], rg_lambda[j])
            states.append(st)
            x = _outproj(y_ctx, y_lat, rg_w_out[j].astype(BF16), rg_b_out[j], mod, x,
                         n_ctx_tiles=tiles_ctx)
        x = _moe(x, mod, norm_moe[i], router_w, router_bias, moe_w1, moe_w3, moe_w2, i,
                 norm_final, n_ctx_tiles=tiles_ctx, final_norm=(i == depth - 1))

    y_prompt = x[0].reshape(nb_ctx, len_ctx, d)
    y_sample = x[1].reshape(nb_lat, len_lat, d)
    new_state = jnp.stack(states, axis=1).astype(x_prompt.dtype)
    return (y_prompt, y_sample, new_state)
```

```python
import functools
import math

import numpy as np
import jax
import jax.numpy as jnp
from jax import lax
from jax.experimental import pallas as pl
from jax.experimental.pallas import tpu as pltpu

F32 = jnp.float32
BF16 = jnp.bfloat16

GRID_W = 64
FILTER_BANDS = 16
FILTER_EPS = 1e-6
MIN_DECAY = math.log(1e-2) / 0.3
MAX_DECAY = math.log(1e-2) / 1.5
RG_C = 8.0
N_GROUPS = 4
TOP_K = 2
NORM_EPS = 1e-6

V7X_LANES = 128
V7X_SUBLANES = 8
V7X_VMEM_BYTES = 64 * 1024 * 1024

TOKEN_TILE = 512
COL_TILE = 512
MOE_ROWS = 256
DFT_ROWS = 512
DFT_SPLIT = 64
RG_COLS = 256
RG_ROWS = 1024
ROW_DMA_UNROLL = 8


def _params(sem, vmem_mb):
    return pltpu.CompilerParams(dimension_semantics=sem,
                                vmem_limit_bytes=vmem_mb * 1024 * 1024)


def _dot(a, b):
    return jnp.dot(a, b, preferred_element_type=F32)


def _split(x):
    hi = x.astype(BF16)
    lo = (x - hi.astype(F32)).astype(BF16)
    return hi, lo


def _dot3(a, b):
    ah, al = _split(a)
    bh, bl = _split(b)
    return _dot(ah, bh) + (_dot(ah, bl) + _dot(al, bh))


def _dot3_nt(a, b):
    dn = (((1,), (1,)), ((), ()))
    d = lambda x, y: lax.dot_general(x, y, dn, preferred_element_type=F32)
    ah, al = _split(a)
    bh, bl = _split(b)
    return d(ah, bh) + (d(ah, bl) + d(al, bh))


def _sigmoid(x):
    return 1.0 / (1.0 + jnp.exp(-x))


def _sigmoid_tanh(x):
    return 0.5 * jnp.tanh(0.5 * x) + 0.5


def _norm_mod(x, g, shift, scale):
    ms = jnp.mean(x * x, axis=-1, keepdims=True)
    return (x * lax.rsqrt(ms + NORM_EPS) * g) * (1.0 + scale) + shift


def _ada_kernel(c_ref, w_ref, b_ref, o_ref):
    c = c_ref[...]
    o_ref[...] = _dot3(c * _sigmoid(c), w_ref[...]) + b_ref[...]


def _ada_modulation(cond, ada_w, ada_b):
    depth, d, n = ada_w.shape
    tn = n // 4
    out = pl.pallas_call(
        _ada_kernel,
        out_shape=jax.ShapeDtypeStruct((depth, cond.shape[0], n), F32),
        grid=(depth, n // tn),
        in_specs=[pl.BlockSpec(cond.shape, lambda i, j: (0, 0)),
                  pl.BlockSpec((None, d, tn), lambda i, j: (i, 0, j)),
                  pl.BlockSpec((None, 1, tn), lambda i, j: (i, 0, j))],
        out_specs=pl.BlockSpec((None, cond.shape[0], tn), lambda i, j: (i, 0, j)),
        compiler_params=_params(("parallel", "parallel"), 40),
        name="ada_modulation",
    )(cond, ada_w, ada_b.reshape(depth, 1, n))
    return out.reshape(depth, cond.shape[0], 6, d)


def _inproj_kernel(x_ref, mod_ref, g_ref, w_ref, b_ref, cw_ref, cb_ref, mk_ref, o_ref, h_scr, *,
                   sub, width, plain_col_tiles):
    j = pl.program_id(1)

    @pl.when(j == 0)
    def _():
        h = _norm_mod(x_ref[...], g_ref[...], mod_ref[0:1, :], mod_ref[1:2, :])
        h_scr[...] = h.astype(BF16)

    blocks = [slice(s, s + sub) for s in range(0, x_ref.shape[0], sub)]

    def project(rows):
        return _dot(h_scr[rows, :], w_ref[...]) + b_ref[...]

    def conv():
        pad_l = (width - 1) // 2
        for rows in blocks:
            u = project(rows)
            acc = cw_ref[pad_l:pad_l + 1, :] * u + cb_ref[...]
            s = 0
            for k in range(width):
                off = k - pad_l
                if off != 0:
                    shifted = pltpu.roll(u, (-off) % sub, axis=0)
                    acc = acc + cw_ref[k:k + 1, :] * (shifted * mk_ref[s])
                    s += 1
            o_ref[rows, :] = acc

    if plain_col_tiles == 0:
        conv()
    else:
        pl.when(j >= plain_col_tiles)(conv)

        @pl.when(j < plain_col_tiles)
        def _():
            for rows in blocks:
                o_ref[rows, :] = project(rows)


def _inproj(x, mod, g, w, b, conv_w, conv_b, *, n_ctx_tiles, ctx_len, lat_len, plain_cols):
    t, d = x.shape
    n = w.shape[1]
    tm, tn = TOKEN_TILE, COL_TILE
    width = conv_w.shape[0]
    plain_tiles = plain_cols // tn
    lat_tiles = (t // tm - n_ctx_tiles) // (mod.shape[0] - 1)

    def cond_of(i):
        return jnp.where(i < n_ctx_tiles, 0, 1 + (i - n_ctx_tiles) // lat_tiles)

    sub = max(ctx_len, lat_len)
    assert sub % ctx_len == 0 and sub % lat_len == 0 and tm % sub == 0
    pad_l = (width - 1) // 2
    offs = jnp.asarray([k - pad_l for k in range(width) if k != pad_l], jnp.int32)
    seg = jnp.asarray([ctx_len, lat_len], jnp.int32)[:, None, None]
    pos = jnp.arange(sub, dtype=jnp.int32)[None, None, :] % seg + offs[None, :, None]
    masks = jnp.broadcast_to(((pos >= 0) & (pos < seg)).astype(F32)[..., None],
                             (2, width - 1, sub, tn))

    kern = functools.partial(_inproj_kernel, sub=sub, width=width, plain_col_tiles=plain_tiles)
    return pl.pallas_call(
        kern,
        out_shape=jax.ShapeDtypeStruct((t, n), F32),
        grid=(t // tm, n // tn),
        in_specs=[pl.BlockSpec((tm, d), lambda i, j: (i, 0)),
                  pl.BlockSpec((None, 6, d), lambda i, j: (cond_of(i), 0, 0)),
                  pl.BlockSpec((1, d), lambda i, j: (0, 0)),
                  pl.BlockSpec((d, tn), lambda i, j: (0, j)),
                  pl.BlockSpec((1, tn), lambda i, j: (0, j)),
                  pl.BlockSpec((width, tn), lambda i, j: (0, jnp.maximum(j - plain_tiles, 0))),
                  pl.BlockSpec((1, tn), lambda i, j: (0, jnp.maximum(j - plain_tiles, 0))),
                  pl.BlockSpec((None, width - 1, sub, tn),
                               lambda i, j: (jnp.where(i < n_ctx_tiles, 0, 1), 0, 0, 0))],
        out_specs=pl.BlockSpec((tm, tn), lambda i, j: (i, j)),
        scratch_shapes=[pltpu.VMEM((tm, d), BF16)],
        compiler_params=_params(("parallel", "arbitrary"), 32),
        name="norm_inproj_conv",
    )(x, mod, g.reshape(1, d), w, b.reshape(1, n), conv_w, conv_b.reshape(1, -1), masks)


def _outproj_kernel(yc_ref, yl_ref, w_ref, b_ref, mod_ref, x_ref, o_ref, *, n_ctx_tiles):
    i = pl.program_id(0)

    def run(y_ref):
        m = _dot(y_ref[...], w_ref[...]) + b_ref[...]
        o_ref[...] = x_ref[...] + mod_ref[2:3, :] * m

    pl.when(i < n_ctx_tiles)(lambda: run(yc_ref))
    pl.when(i >= n_ctx_tiles)(lambda: run(yl_ref))


def _outproj(y_ctx, y_lat, w, b, mod, x, *, n_ctx_tiles):
    t, d = x.shape
    tm = TOKEN_TILE
    lat_tiles = (t // tm - n_ctx_tiles) // (mod.shape[0] - 1)

    def cond_of(i):
        return jnp.where(i < n_ctx_tiles, 0, 1 + (i - n_ctx_tiles) // lat_tiles)

    return pl.pallas_call(
        functools.partial(_outproj_kernel, n_ctx_tiles=n_ctx_tiles),
        out_shape=jax.ShapeDtypeStruct((t, d), F32),
        grid=(t // tm,),
        in_specs=[pl.BlockSpec((tm, w.shape[0]), lambda i: (jnp.minimum(i, n_ctx_tiles - 1), 0)),
                  pl.BlockSpec((tm, w.shape[0]), lambda i: (jnp.maximum(i - n_ctx_tiles, 0), 0)),
                  pl.BlockSpec(w.shape, lambda i: (0, 0)),
                  pl.BlockSpec((1, d), lambda i: (0, 0)),
                  pl.BlockSpec((None, 6, d), lambda i: (cond_of(i), 0, 0)),
                  pl.BlockSpec((tm, d), lambda i: (i, 0))],
        out_specs=pl.BlockSpec((tm, d), lambda i: (i, 0)),
        input_output_aliases={5: 0},
        compiler_params=_params(("parallel",), 32),
        name="outproj_residual",
    )(y_ctx, y_lat, w, b.reshape(1, d), mod, x)


def _filter_kernel(z_ref, w1_ref, b1_ref, fr_ref, w2_ref, b2_ref, w3a_ref, w3b_ref, w3c_ref,
                   w3d_ref, t_ref, dl_ref, o_ref, *, orders):
    fr = fr_ref[...]
    h = jnp.sin(fr * (_dot3(z_ref[...], w1_ref[...]) + b1_ref[...]))
    h = jnp.sin(fr * (_dot3(h, w2_ref[...]) + b2_ref[...]))
    window = jnp.exp(-t_ref[...] * dl_ref[...])
    row = lax.broadcasted_iota(jnp.int32, (h.shape[0], 1), 0)
    w3 = ((w3a_ref, w3b_ref), (w3c_ref, w3d_ref))
    for o in range(orders):
        h_fwd = _dot3(h, w3[0][o][...]) * window
        h_bwd = jnp.where(row == 0, 0.0, _dot3(h, w3[1][o][...]) * window)
        norm = (jnp.sum(jnp.abs(h_fwd), axis=0, keepdims=True)
                + jnp.sum(jnp.abs(h_bwd), axis=0, keepdims=True) + FILTER_EPS)
        o_ref[2 * o] = h_fwd / norm
        o_ref[2 * o + 1] = h_bwd / norm


def _hyena_filter(length, f_w1, f_b1, f_freq, f_w2, f_b2, f_w3, d):
    emb, hid = f_w1.shape
    orders = f_w3.shape[1] // (2 * d)
    t = jnp.linspace(0.0, 1.0, length, dtype=F32)[:, None]
    w = (2.0 * math.pi / length) * jnp.arange(length, dtype=F32)[:, None]
    bands = jnp.linspace(1e-4, FILTER_BANDS - 1, FILTER_BANDS, dtype=F32)[None, :]
    z = jnp.concatenate([t, jnp.cos(bands * w), -jnp.sin(bands * w)], axis=-1)
    emb_pad = V7X_LANES
    z = jnp.pad(z, ((0, 0), (0, emb_pad - emb)))
    w1 = jnp.pad(f_w1, ((0, emb_pad - emb), (0, 0)))
    deltas = jnp.abs(jnp.linspace(MIN_DECAY, MAX_DECAY, d, dtype=F32))[None, :]
    td = 256
    nd = d // td
    full = lambda shape: pl.BlockSpec(shape, lambda j: (0,) * len(shape))
    w3_spec = lambda side, o: pl.BlockSpec((hid, td), lambda j: (0, (side * orders + o) * nd + j))
    assert orders == 2
    return pl.pallas_call(
        functools.partial(_filter_kernel, orders=orders),
        out_shape=jax.ShapeDtypeStruct((2 * orders, length, d), F32),
        grid=(nd,),
        in_specs=[full((length, emb_pad)), full((emb_pad, hid)), full((1, hid)), full((1, hid)),
                  full((hid, hid)), full((1, hid)),
                  w3_spec(0, 0), w3_spec(0, 1), w3_spec(1, 0), w3_spec(1, 1),
                  full((length, 1)), pl.BlockSpec((1, td), lambda j: (0, j))],
        out_specs=pl.BlockSpec((2 * orders, length, td), lambda j: (0, 0, j)),
        compiler_params=_params(("parallel",), 48),
        name="hyena_filter",
    )(z, w1, f_b1.reshape(1, hid), f_freq.reshape(1, hid), f_w2, f_b2.reshape(1, hid),
      f_w3, f_w3, f_w3, f_w3, t, deltas)


def _dft_matrices(length):
    n = 2 * length
    f0n, f1n = DFT_SPLIT, length // DFT_SPLIT
    tt = np.arange(length, dtype=np.int64)[None, :]
    ang_a = 2.0 * np.pi * ((DFT_SPLIT * np.arange(f1n, dtype=np.int64)[:, None] * tt) % n) / n
    ang_b = 2.0 * np.pi * ((np.arange(f0n, dtype=np.int64)[:, None] * tt) % n) / n
    ca, sa = (jnp.asarray(f(ang_a), F32)[:, None, :] for f in (np.cos, np.sin))
    cb, sb = (jnp.asarray(f(ang_b), F32)[None, :, :] for f in (np.cos, np.sin))
    cos_m = (ca * cb - sa * sb).reshape(length, length)
    sin_m = (sa * cb + ca * sb).reshape(length, length)
    nyq = jnp.asarray(1.0 - 2.0 * (np.arange(length) % 2), F32)[None, :]
    f_is0 = (jnp.arange(length) == 0)[:, None]
    im_m = jnp.where(f_is0, nyq, -sin_m)
    th = DFT_ROWS // 2
    fwd = jnp.stack([cos_m.reshape(length // th, th, length),
                     im_m.reshape(length // th, th, length)], axis=1).reshape(n, length)
    scale = jnp.where(f_is0, 1.0 / n, 2.0 / n)
    inv = jnp.stack([(cos_m * scale).reshape(length // th, th, length),
                     (im_m * scale).reshape(length // th, th, length)], axis=1).reshape(n, length).T
    return fwd.astype(BF16), inv.astype(BF16)


def _spec_kernel(a_ref, hf_ref, hb_ref, o_ref, sum_scr, dif_scr, nyq_scr):
    m = pl.program_id(2)

    @pl.when(m == 0)
    def _():
        hf, hb = hf_ref[...], hb_ref[...]
        sum_scr[...] = (hf + hb).astype(BF16)
        dif_scr[...] = (hf - hb).astype(BF16)
        t = lax.broadcasted_iota(jnp.int32, (hb.shape[0], 1), 0)
        sign = (1 - 2 * (t & 1)).astype(F32)
        nyq_scr[...] = 2.0 * jnp.sum(sign * hb, axis=0, keepdims=True)

    th = a_ref.shape[0] // 2
    o_ref[0:th, :] = _dot(a_ref[0:th, :], sum_scr[...])
    im = _dot(a_ref[th:, :], dif_scr[...])
    row = lax.broadcasted_iota(jnp.int32, (th, 1), 0)
    o_ref[th:, :] = jnp.where((row == 0) & (m == 0), im + nyq_scr[...], im)


def _filter_spectrum(fwd, filt):
    n2, length, d = filt.shape
    orders = n2 // 2
    n = 2 * length
    tm, tc = DFT_ROWS, 512
    filt2 = filt.reshape(n2 * length, d)
    return pl.pallas_call(
        _spec_kernel,
        out_shape=jax.ShapeDtypeStruct((orders * n, d), F32),
        grid=(orders, d // tc, n // tm),
        in_specs=[pl.BlockSpec((tm, length), lambda o, c, m: (m, 0)),
                  pl.BlockSpec((length, tc), lambda o, c, m: (2 * o, c)),
                  pl.BlockSpec((length, tc), lambda o, c, m: (2 * o + 1, c))],
        out_specs=pl.BlockSpec((tm, tc), lambda o, c, m: (o * (n // tm) + m, c)),
        scratch_shapes=[pltpu.VMEM((length, tc), BF16), pltpu.VMEM((length, tc), BF16),
                        pltpu.VMEM((1, tc), F32)],
        compiler_params=_params(("parallel", "parallel", "arbitrary"), 48),
        name="hyena_filter_spectrum",
    )(fwd, filt2, filt2)


def _dft_fwd_kernel(a_ref, z_ref, k_ref, o_ref, z_scr):
    m = pl.program_id(1)

    @pl.when(m == 0)
    def _():
        z_scr[...] = z_ref[...].astype(BF16)

    acc = _dot(a_ref[...], z_scr[...])
    th = acc.shape[0] // 2
    xr, xi = acc[:th], acc[th:]
    kr, ki = k_ref[0:th, :], k_ref[th:, :]
    row = lax.broadcasted_iota(jnp.int32, (th, 1), 0)
    dc = (row == 0) & (m == 0)
    o_ref[0:th, :] = (xr * kr - jnp.where(dc, 0.0, xi * ki)).astype(o_ref.dtype)
    o_ref[th:, :] = jnp.where(dc, xi * ki, xr * ki + xi * kr).astype(o_ref.dtype)


def _dft_forward(fwd, src, spec, *, nb, length, row_off, col_blk, order):
    n = 2 * length
    d = spec.shape[1]
    tm = DFT_ROWS
    nt = n // tm
    return pl.pallas_call(
        _dft_fwd_kernel,
        out_shape=jax.ShapeDtypeStruct((nb * n, d), BF16),
        grid=(nb, nt),
        in_specs=[pl.BlockSpec((tm, length), lambda b, m: (m, 0)),
                  pl.BlockSpec((length, d), lambda b, m: (row_off + b, col_blk)),
                  pl.BlockSpec((tm, d), lambda b, m: (order * nt + m, 0))],
        out_specs=pl.BlockSpec((tm, d), lambda b, m: (b * nt + m, 0)),
        scratch_shapes=[pltpu.VMEM((length, d), BF16)],
        compiler_params=_params(("parallel", "arbitrary"), 48),
        name="hyena_dft_forward",
    )(fwd, src, spec)


def _dft_inv_kernel(a_ref, p_ref, v_ref, g_ref, s_ref, o_ref):
    y = _dot(a_ref[...], p_ref[...])
    v = v_ref[...]
    o_ref[...] = (g_ref[...] * (y + v * s_ref[...])).astype(o_ref.dtype)


def _dft_inverse(inv, prod, vsrc, gsrc, skip, *, nb, length, v_off, v_col, g_off, g_col,
                 out_dtype):
    n = 2 * length
    d = prod.shape[1]
    tm = min(DFT_ROWS, length)
    nt = length // tm
    return pl.pallas_call(
        _dft_inv_kernel,
        out_shape=jax.ShapeDtypeStruct((nb * length, d), out_dtype),
        grid=(nb, nt),
        in_specs=[pl.BlockSpec((tm, n), lambda b, m: (m, 0)),
                  pl.BlockSpec((n, d), lambda b, m: (b, 0)),
                  pl.BlockSpec((tm, d), lambda b, m: ((v_off + b) * nt + m, v_col)),
                  pl.BlockSpec((tm, d), lambda b, m: ((g_off + b) * nt + m, g_col)),
                  pl.BlockSpec((1, d), lambda b, m: (0, 0))],
        out_specs=pl.BlockSpec((tm, d), lambda b, m: (b * nt + m, 0)),
        compiler_params=_params(("parallel", "arbitrary"), 48),
        name="hyena_dft_inverse",
    )(inv, prod, vsrc, gsrc, skip.reshape(1, d))


def _spectral_product(acc, k_ref, order):
    th = acc.shape[0] // 2
    xr, xi = acc[:th], acc[th:]
    kr, ki = k_ref[order, 0:th, :], k_ref[order, th:, :]
    dc = lax.broadcasted_iota(jnp.int32, (th, 1), 0) == 0
    pr = xr * kr - jnp.where(dc, 0.0, xi * ki)
    pi = jnp.where(dc, xi * ki, xr * ki + xi * kr)
    return jnp.concatenate([pr, pi], axis=0).astype(BF16)


def _hyena_short_kernel(u_ref, f_ref, i_ref, k_ref, s_ref, o_ref):
    d = o_ref.shape[1]
    fwd, inv = f_ref[...], i_ref[...]
    z = u_ref[:, 0:d]
    for order in range(2):
        p = _spectral_product(_dot(fwd, z.astype(BF16)), k_ref, order)
        gate = u_ref[:, (order + 1) * d:(order + 2) * d]
        z = gate * (_dot(inv, p) + z * s_ref[order:order + 1, :])
    o_ref[...] = z.astype(o_ref.dtype)


def _hyena_short(u, fwd, inv, spec, f_skip, *, nb, length):
    n = 2 * length
    d = spec.shape[1]
    assert n == DFT_ROWS and spec.shape[0] == 2 * n
    return pl.pallas_call(
        _hyena_short_kernel,
        out_shape=jax.ShapeDtypeStruct((nb * length, d), BF16),
        grid=(nb,),
        in_specs=[pl.BlockSpec((length, 3 * d), lambda b: (b, 0)),
                  pl.BlockSpec((n, length), lambda b: (0, 0)),
                  pl.BlockSpec((length, n), lambda b: (0, 0)),
                  pl.BlockSpec((2, n, d), lambda b: (0, 0, 0)),
                  pl.BlockSpec((2, d), lambda b: (0, 0))],
        out_specs=pl.BlockSpec((length, d), lambda b: (b, 0)),
        compiler_params=_params(("parallel",), 48),
        name="hyena_short_sequences",
    )(u, fwd, inv, spec.reshape(2, n, d), f_skip)


def _hyena_stream(u, row_off, nb, length, filt_w, f_skip, d):
    fwd, inv = _dft_matrices(length)
    filt = _hyena_filter(length, *filt_w, d)
    spec = _filter_spectrum(fwd, filt)
    if 2 * length == DFT_ROWS:
        assert row_off == 0
        return _hyena_short(u, fwd, inv, spec, f_skip, nb=nb, length=length)
    p1 = _dft_forward(fwd, u, spec, nb=nb, length=length, row_off=row_off, col_blk=0, order=0)
    z1 = _dft_inverse(inv, p1, u, u, f_skip[0], nb=nb, length=length, v_off=row_off, v_col=0,
                      g_off=row_off, g_col=1, out_dtype=F32)
    p2 = _dft_forward(fwd, z1, spec, nb=nb, length=length, row_off=0, col_blk=0, order=1)
    return _dft_inverse(inv, p2, z1, u, f_skip[1], nb=nb, length=length, v_off=0, v_col=0,
                        g_off=row_off, g_col=2, out_dtype=BF16)


def _gelu_tanh(x):
    return 0.5 * x * (1.0 + jnp.tanh(math.sqrt(2.0 / math.pi) * (x + 0.044715 * (x * x * x))))


def _log1p(e):
    u = 1.0 + e
    d = u - 1.0
    return jnp.where(d == 0.0, e, jnp.log(u) * (e / jnp.where(d == 0.0, 1.0, d)))


def _rglru_kernel(gate_ref, rec_ref, wa_ref, wx_ref, ba_ref, bx_ref, lam_ref, h0_ref,
                  y_ref, st_ref, a_scr, b_scr):
    length, cols = rec_ref.shape
    groups = length // V7X_SUBLANES
    rec = rec_ref[...]
    rec16 = rec.astype(BF16)
    pos = lax.broadcasted_iota(jnp.int32, (groups, V7X_SUBLANES, cols), 1)

    for d in range(2):
        r_gate = _sigmoid_tanh(_dot(rec16, wa_ref[d]) + ba_ref[d])
        i_gate = _sigmoid_tanh(_dot(rec16, wx_ref[d]) + bx_ref[d])
        nlam = -lam_ref[d]
        softplus = jnp.maximum(nlam, 0.0) + _log1p(jnp.exp(-jnp.abs(nlam)))
        log_a = (-RG_C * softplus) * r_gate
        a = jnp.exp(log_a).reshape(groups, V7X_SUBLANES, cols)
        th = jnp.tanh(log_a)
        b = (jnp.sqrt(-2.0 * th / (1.0 - th)) * (i_gate * rec)).reshape(
            groups, V7X_SUBLANES, cols)
        for s in (1, 2, 4):
            if d == 0:
                a_sh = pltpu.roll(a, s, axis=1)
                b_sh = pltpu.roll(b, s, axis=1)
                live = pos >= s
            else:
                a_sh = pltpu.roll(a, V7X_SUBLANES - s, axis=1)
                b_sh = pltpu.roll(b, V7X_SUBLANES - s, axis=1)
                live = pos < V7X_SUBLANES - s
            b = jnp.where(live, a * b_sh, 0.0) + b
            a = jnp.where(live, a * a_sh, a)
        a_scr[d] = a.reshape(length, cols)
        b_scr[d] = b.reshape(length, cols)

    seqs = h0_ref.shape[0]
    seq_groups = groups // seqs

    def step(g, carry):
        out = []
        for q in range(seqs):
            cf, cb = carry[2 * q], carry[2 * q + 1]
            rf = pl.multiple_of((q * seq_groups + g) * V7X_SUBLANES, V7X_SUBLANES)
            rb = pl.multiple_of(((q + 1) * seq_groups - 1 - g) * V7X_SUBLANES, V7X_SUBLANES)
            hf = a_scr[0, pl.ds(rf, V7X_SUBLANES), :] * cf + b_scr[0, pl.ds(rf, V7X_SUBLANES), :]
            hb = a_scr[1, pl.ds(rb, V7X_SUBLANES), :] * cb + b_scr[1, pl.ds(rb, V7X_SUBLANES), :]
            b_scr[0, pl.ds(rf, V7X_SUBLANES), :] = hf
            b_scr[1, pl.ds(rb, V7X_SUBLANES), :] = hb
            out.append(jnp.broadcast_to(hf[V7X_SUBLANES - 1:V7X_SUBLANES, :], hf.shape))
            out.append(jnp.broadcast_to(hb[0:1, :], hb.shape))
        return tuple(out)

    init = tuple(jnp.broadcast_to(h0_ref[q, d], (V7X_SUBLANES, cols))
                 for q in range(seqs) for d in range(2))
    final = lax.fori_loop(0, seq_groups, step, init)
    for q in range(seqs):
        for d in range(2):
            st_ref[q, d] = final[2 * q + d][0:1, :]
    y_ref[...] = ((b_scr[0] + b_scr[1]) * _gelu_tanh(gate_ref[...])).astype(y_ref.dtype)


def _rglru_stream(u, h0, row_off, nb, length, wa, wx, ba, bx, lam):
    dr = u.shape[1] // 2
    tc = RG_COLS
    nc = dr // tc
    seqs = max(1, min(nb, RG_ROWS // length))
    assert nb % seqs == 0 and (row_off * length) % (seqs * length) == 0
    rows = seqs * length
    blk_off = row_off // seqs
    vec = lambda a: a.reshape(2, 1, dr)
    vec_spec = pl.BlockSpec((2, 1, tc), lambda b, c: (0, 0, c))
    w_spec = pl.BlockSpec((2, None, tc, tc), lambda b, c: (0, c, 0, 0))
    st_spec = pl.BlockSpec((seqs, 2, 1, tc), lambda b, c: (b, 0, 0, c))
    y, st = pl.pallas_call(
        _rglru_kernel,
        out_shape=(jax.ShapeDtypeStruct((nb * length, dr), BF16),
                   jax.ShapeDtypeStruct((nb, 2, 1, dr), F32)),
        grid=(nb // seqs, nc),
        in_specs=[pl.BlockSpec((rows, tc), lambda b, c: (blk_off + b, c)),
                  pl.BlockSpec((rows, tc), lambda b, c: (blk_off + b, nc + c)),
                  w_spec, w_spec, vec_spec, vec_spec, vec_spec, st_spec],
        out_specs=(pl.BlockSpec((rows, tc), lambda b, c: (b, c)), st_spec),
        scratch_shapes=[pltpu.VMEM((2, rows, tc), F32), pltpu.VMEM((2, rows, tc), F32)],
        compiler_params=_params(("parallel", "parallel"), 48),
        name="rglru_scan",
    )(u, u, wa, wx, vec(ba), vec(bx), vec(lam), h0.reshape(nb, 2, 1, dr))
    return y, st.reshape(nb, 2, dr)


def _router_kernel(x_ref, mod_ref, g_ref, rwt_ref, bias_ref, h_ref, idx_ref, gate_ref,
                   rank_ref, cnt_ref, tri_scr, carry_scr, *, n_experts):
    i = pl.program_id(0)
    tm = x_ref.shape[0]
    per_group = n_experts // N_GROUPS

    @pl.when(i == 0)
    def _():
        r = lax.broadcasted_iota(jnp.int32, (tm, tm), 0)
        c = lax.broadcasted_iota(jnp.int32, (tm, tm), 1)
        tri_scr[...] = jnp.where(r < c, 1.0, 0.0).astype(BF16)
        carry_scr[...] = jnp.zeros_like(carry_scr)

    h = _norm_mod(x_ref[...], g_ref[...], mod_ref[3:4, :], mod_ref[4:5, :])
    h_ref[...] = h
    logits = _dot3_nt(rwt_ref[...], h)
    p = jnp.exp(logits - jnp.max(logits, axis=0, keepdims=True))
    scores = p / jnp.sum(p, axis=0, keepdims=True)
    sel = scores + bias_ref[...]
    rows = [sel[e:e + 1, :] for e in range(n_experts)]

    best_val = None
    for gi in range(N_GROUPS):
        v = rows[gi * per_group:(gi + 1) * per_group]
        pair = None
        for a in range(per_group):
            for b in range(a + 1, per_group):
                s = v[a] + v[b]
                pair = s if pair is None else jnp.maximum(pair, s)
        if best_val is None:
            best_val, best_grp = pair, jnp.zeros_like(pair, dtype=jnp.int32)
        else:
            take = pair > best_val
            best_val = jnp.where(take, pair, best_val)
            best_grp = jnp.where(take, gi, best_grp)

    neg = jnp.float32(-jnp.inf)
    masked = [jnp.where(best_grp == e // per_group, rows[e], neg) for e in range(n_experts)]

    def argmax_first(vals):
        bv, bi = vals[0], jnp.zeros_like(best_grp)
        for e in range(1, n_experts):
            take = vals[e] > bv
            bv = jnp.where(take, vals[e], bv)
            bi = jnp.where(take, e, bi)
        return bi

    idx0 = argmax_first(masked)
    idx1 = argmax_first([jnp.where(idx0 == e, neg, masked[e]) for e in range(n_experts)])

    e_iota = lax.broadcasted_iota(jnp.int32, (n_experts, tm), 0)
    hit0 = e_iota == idx0
    hit1 = e_iota == idx1
    g0 = jnp.sum(jnp.where(hit0, scores, 0.0), axis=0, keepdims=True)
    g1 = jnp.sum(jnp.where(hit1, scores, 0.0), axis=0, keepdims=True)
    gsum = g0 + g1
    onehot = jnp.where(hit0 | hit1, 1.0, 0.0)
    before = _dot(onehot.astype(BF16), tri_scr[...]) + carry_scr[:, 0:1]
    r0 = jnp.sum(jnp.where(hit0, before, 0.0), axis=0, keepdims=True)
    r1 = jnp.sum(jnp.where(hit1, before, 0.0), axis=0, keepdims=True)
    idx_ref[0:1, :] = idx0
    idx_ref[1:2, :] = idx1
    gate_ref[0:1, :] = g0 / gsum
    gate_ref[1:2, :] = g1 / gsum
    rank_ref[0:1, :] = r0.astype(jnp.int32)
    rank_ref[1:2, :] = r1.astype(jnp.int32)
    carry_scr[...] = carry_scr[...] + jnp.sum(onehot, axis=1, keepdims=True)
    cnt_ref[...] = carry_scr[...]


def _router(x, mod, g, router_w, router_bias, *, n_ctx_tiles):
    t, d = x.shape
    ne = router_w.shape[1]
    tm = TOKEN_TILE
    lat_tiles = (t // tm - n_ctx_tiles) // (mod.shape[0] - 1)

    def cond_of(i):
        return jnp.where(i < n_ctx_tiles, 0, 1 + (i - n_ctx_tiles) // lat_tiles)

    row2 = pl.BlockSpec((TOP_K, tm), lambda i: (0, i))
    return pl.pallas_call(
        functools.partial(_router_kernel, n_experts=ne),
        out_shape=(jax.ShapeDtypeStruct((t, d), F32),
                   jax.ShapeDtypeStruct((TOP_K, t), jnp.int32),
                   jax.ShapeDtypeStruct((TOP_K, t), F32),
                   jax.ShapeDtypeStruct((TOP_K, t), jnp.int32),
                   jax.ShapeDtypeStruct((ne, V7X_LANES), F32)),
        grid=(t // tm,),
        in_specs=[pl.BlockSpec((tm, d), lambda i: (i, 0)),
                  pl.BlockSpec((None, 6, d), lambda i: (cond_of(i), 0, 0)),
                  pl.BlockSpec((1, d), lambda i: (0, 0)),
                  pl.BlockSpec((ne, d), lambda i: (0, 0)),
                  pl.BlockSpec((ne, 1), lambda i: (0, 0))],
        out_specs=(pl.BlockSpec((tm, d), lambda i: (i, 0)), row2, row2, row2,
                   pl.BlockSpec((ne, V7X_LANES), lambda i: (0, 0))),
        scratch_shapes=[pltpu.VMEM((tm, tm), BF16), pltpu.VMEM((ne, V7X_LANES), F32)],
        compiler_params=_params(("arbitrary",), 32),
        name="moe_router",
    )(x, mod, g.reshape(1, d), router_w.T, router_bias.reshape(ne, 1))


def _row_copy(src, src_row, dst, dst_row, sem):
    return pltpu.make_async_copy(src.at[pl.ds(src_row, 1)], dst.at[pl.ds(dst_row, 1)], sem)


def _dispatch_kernel(pad_row_ref, pad_len_ref, used_ref, dest_ref, h_ref, xs_out, zero_scr, sem,
                     pad_sem):
    tm = h_ref.shape[0]
    i = pl.program_id(0)
    n_experts = pad_row_ref.shape[0]
    zrows = zero_scr.shape[0]

    @pl.when(i == 0)
    def _():
        zero_scr[...] = jnp.zeros_like(zero_scr)

    def zero_copy(wanted, start, size):
        start = jnp.where(wanted, start, 0)
        if size >= V7X_SUBLANES:
            start = pl.multiple_of(start, V7X_SUBLANES)
        return wanted, pltpu.make_async_copy(zero_scr.at[pl.ds(0, size)],
                                             xs_out.at[pl.ds(start, size)], pad_sem)

    e = jnp.minimum(i, n_experts - 1)
    pad_len = jnp.where(i < n_experts, pad_len_ref[e], 0)
    pad_row = pad_row_ref[e]
    head = pad_len & (V7X_SUBLANES - 1)
    pad_copies = [zero_copy(r < head, pad_row + r, 1) for r in range(V7X_SUBLANES - 1)]
    size = zrows
    while size >= V7X_SUBLANES:
        start = pad_row + head + ((pad_len - head) & ~(2 * size - 1))
        pad_copies.append(zero_copy((pad_len & size) != 0, start, size))
        size //= 2
    spare = used_ref[0] + i
    has_spare = (i < n_experts) & (spare < xs_out.shape[0] // MOE_ROWS)
    for part in range(MOE_ROWS // zrows):
        pad_copies.append(zero_copy(has_spare, spare * MOE_ROWS + part * zrows, zrows))
    for wanted, copy in pad_copies:
        pl.when(wanted)(copy.start)

    def issue(g, c):
        for r in range(ROW_DMA_UNROLL):
            t = g * ROW_DMA_UNROLL + r
            for k in range(TOP_K):
                _row_copy(h_ref, t, xs_out, dest_ref[TOP_K * t + k], sem).start(priority=k)
        return c

    def drain(g, c):
        for _ in range(ROW_DMA_UNROLL * TOP_K):
            _row_copy(h_ref, 0, xs_out, 0, sem).wait()
        return c

    lax.fori_loop(0, tm // ROW_DMA_UNROLL, issue, 0)
    lax.fori_loop(0, tm // ROW_DMA_UNROLL, drain, 0)
    for wanted, copy in pad_copies:
        pl.when(wanted)(copy.wait)


def _dispatch(h, dest, pad_row, pad_len, n_used, n_slots):
    t, d = h.shape
    tm = TOKEN_TILE
    assert t // tm >= pad_row.shape[0] and n_slots % MOE_ROWS == 0
    return pl.pallas_call(
        _dispatch_kernel,
        out_shape=jax.ShapeDtypeStruct((n_slots, d), h.dtype),
        grid_spec=pltpu.PrefetchScalarGridSpec(
            num_scalar_prefetch=3,
            grid=(t // tm,),
            in_specs=[pl.BlockSpec((TOP_K * tm,), lambda i, pr, pn, nu: (i,),
                                   memory_space=pltpu.SMEM),
                      pl.BlockSpec((tm, d), lambda i, pr, pn, nu: (i, 0))],
            out_specs=pl.BlockSpec(memory_space=pl.ANY),
            scratch_shapes=[pltpu.VMEM((MOE_ROWS // 2, d), h.dtype),
                            pltpu.SemaphoreType.DMA, pltpu.SemaphoreType.DMA]),
        compiler_params=_params(("arbitrary",), 32),
        name="moe_dispatch",
    )(pad_row, pad_len, n_used, dest.reshape(-1), h)


def _experts_kernel(be_ref, first_ref, slot_ref, next_ref, nu_ref, xs_ref, w1_hbm, w3_hbm, w2_hbm,
                    ys_ref, wbuf, w1_s, w3_s, w2_s, sem, *, layer):
    i = pl.program_id(0)
    used = i < nu_ref[0]
    w_hbm = (w1_hbm, w3_hbm, w2_hbm)
    w_s = (w1_s, w3_s, w2_s)

    def weight_copies(expert, slot):
        return [pltpu.make_async_copy(w_hbm[m].at[layer, expert], wbuf.at[slot, m],
                                      sem.at[slot, m]) for m in range(3)]

    @pl.when(i == 0)
    def _():
        for copy in weight_copies(be_ref[0], slot_ref[0]):
            copy.start()

    @pl.when(used & (first_ref[i] != 0))
    def _():
        slot = slot_ref[i]
        for copy in weight_copies(be_ref[i], slot):
            copy.wait()
        for m in range(3):
            w_s[m][...] = wbuf[slot, m].astype(BF16)
        nxt = next_ref[i]

        @pl.when(nxt >= 0)
        def _():
            for copy in weight_copies(nxt, 1 - slot):
                copy.start()

    @pl.when(used)
    def _():
        x = xs_ref[...].astype(BF16)
        h1 = _dot(x, w1_s[...])
        h3 = _dot(x, w3_s[...])
        act = (h1 * _sigmoid(h1)) * h3
        ys_ref[...] = _dot(act.astype(BF16), w2_s[...])

    @pl.when(i >= nu_ref[0])
    def _():
        ys_ref[...] = jnp.zeros_like(ys_ref)


def _experts(xs, block_e, pad_end, n_used, w1, w3, w2, layer):
    n_slots, d = xs.shape
    de = w1.shape[3]
    assert d == de
    tm = MOE_ROWS
    n_blocks = n_slots // tm
    prev_e = jnp.concatenate([jnp.full((1,), -1, jnp.int32), block_e[:-1]])
    first = (block_e != prev_e).astype(jnp.int32)
    slot = (jnp.cumsum(first) - 1) & 1
    run_end = pad_end[block_e] // tm
    nxt = jnp.where(run_end < n_used[0], block_e[jnp.minimum(run_end, n_blocks - 1)], -1)
    any_spec = pl.BlockSpec(memory_space=pl.ANY)
    row_spec = lambda f: pl.BlockSpec((tm, d), lambda i, be, fi, sl, nx, nu: (f(i, nu), 0))
    return pl.pallas_call(
        functools.partial(_experts_kernel, layer=layer),
        out_shape=jax.ShapeDtypeStruct((n_slots, d), F32),
        grid_spec=pltpu.PrefetchScalarGridSpec(
            num_scalar_prefetch=5,
            grid=(n_blocks,),
            in_specs=[row_spec(lambda i, nu: jnp.minimum(i, nu[0] - 1)),
                      any_spec, any_spec, any_spec],
            out_specs=row_spec(lambda i, nu: i),
            scratch_shapes=[pltpu.VMEM((2, 3, d, de), F32),
                            pltpu.VMEM((d, de), BF16), pltpu.VMEM((d, de), BF16),
                            pltpu.VMEM((de, d), BF16), pltpu.SemaphoreType.DMA((2, 3))]),
        compiler_params=_params(("arbitrary",), 52),
        name="moe_experts",
    )(block_e, first, slot.astype(jnp.int32), nxt.astype(jnp.int32), n_used, xs, w1, w3, w2)


def _combine_kernel(dest_ref, x_ref, gate_ref, mod_ref, gf_ref, ys_hbm, *rest, n_ctx_tiles,
                    final_norm):
    buf0, buf1, sem = rest[-3:]
    tm = x_ref.shape[0]
    bufs = (buf0, buf1)

    def issue(g, c):
        for r in range(ROW_DMA_UNROLL):
            t = g * ROW_DMA_UNROLL + r
            for k in range(TOP_K):
                _row_copy(ys_hbm, dest_ref[TOP_K * t + k], bufs[k], t, sem).start(priority=k)
        return c

    def drain(g, c):
        for r in range(ROW_DMA_UNROLL):
            for k in range(TOP_K):
                _row_copy(ys_hbm, 0, bufs[k], 0, sem).wait()
        return c

    lax.fori_loop(0, tm // ROW_DMA_UNROLL, issue, 0)
    lax.fori_loop(0, tm // ROW_DMA_UNROLL, drain, 0)
    gate = gate_ref[...]
    m = gate[:, 0:1] * buf0[...] + gate[:, 1:2] * buf1[...]
    x = x_ref[...] + mod_ref[5:6, :] * m
    if not final_norm:
        rest[0][...] = x
        return
    ms = jnp.mean(x * x, axis=-1, keepdims=True)
    x = x * lax.rsqrt(ms + NORM_EPS) * gf_ref[...]
    i = pl.program_id(0)

    @pl.when(i < n_ctx_tiles)
    def _():
        rest[0][...] = x

    @pl.when(i >= n_ctx_tiles)
    def _():
        rest[1][...] = x


def _combine(x, ys, dest, gates, mod, g_final, *, n_ctx_tiles, final_norm):
    t, d = x.shape
    tm = TOKEN_TILE
    lat_tiles = (t // tm - n_ctx_tiles) // (mod.shape[0] - 1)

    def cond_of(i):
        return jnp.where(i < n_ctx_tiles, 0, 1 + (i - n_ctx_tiles) // lat_tiles)

    if final_norm:
        out_shape = (jax.ShapeDtypeStruct((n_ctx_tiles * tm, d), F32),
                     jax.ShapeDtypeStruct((t - n_ctx_tiles * tm, d), F32))
        out_specs = (pl.BlockSpec((tm, d), lambda i: (jnp.minimum(i, n_ctx_tiles - 1), 0)),
                     pl.BlockSpec((tm, d), lambda i: (jnp.maximum(i - n_ctx_tiles, 0), 0)))
        aliases = {}
    else:
        out_shape = jax.ShapeDtypeStruct((t, d), F32)
        out_specs = pl.BlockSpec((tm, d), lambda i: (i, 0))
        aliases = {1: 0}
    return pl.pallas_call(
        functools.partial(_combine_kernel, n_ctx_tiles=n_ctx_tiles, final_norm=final_norm),
        out_shape=out_shape,
        grid=(t // tm,),
        in_specs=[pl.BlockSpec((TOP_K * tm,), lambda i: (i,), memory_space=pltpu.SMEM),
                  pl.BlockSpec((tm, d), lambda i: (i, 0)),
                  pl.BlockSpec((tm, TOP_K), lambda i: (i, 0)),
                  pl.BlockSpec((None, 6, d), lambda i: (cond_of(i), 0, 0)),
                  pl.BlockSpec((1, d), lambda i: (0, 0)),
                  pl.BlockSpec(memory_space=pl.ANY)],
        out_specs=out_specs,
        scratch_shapes=[pltpu.VMEM((tm, d), F32), pltpu.VMEM((tm, d), F32),
                        pltpu.SemaphoreType.DMA],
        input_output_aliases=aliases,
        compiler_params=_params(("arbitrary",), 40),
        name="moe_combine",
    )(dest.reshape(-1), x, gates, mod, g_final.reshape(1, d), ys)


def _moe(x, mod, g, router_w, router_bias, w1, w3, w2, layer, g_final, *, n_ctx_tiles,
         final_norm):
    t, d = x.shape
    ne = router_w.shape[1]
    h, idx, gates, rank, cnt = _router(x, mod, g, router_w, router_bias, n_ctx_tiles=n_ctx_tiles)
    counts = cnt[:, 0].astype(jnp.int32)
    padded = (counts + MOE_ROWS - 1) // MOE_ROWS * MOE_ROWS
    pad_end = jnp.cumsum(padded)
    pad_start = pad_end - padded
    e_ids = jnp.arange(ne, dtype=jnp.int32)
    start_of = jnp.sum(jnp.where(idx[None] == e_ids[:, None, None],
                                 pad_start[:, None, None], 0), axis=0)
    dest = (start_of + rank).T
    n_blocks = -(-(t * TOP_K) // MOE_ROWS) + ne
    block_start = jnp.arange(n_blocks, dtype=jnp.int32) * MOE_ROWS
    block_e = jnp.minimum(jnp.sum(block_start[:, None] >= pad_end[None, :], axis=1),
                          ne - 1).astype(jnp.int32)
    n_used = (pad_end[-1:] // MOE_ROWS).astype(jnp.int32)
    xs = _dispatch(h, dest, pad_start + counts, padded - counts, n_used, n_blocks * MOE_ROWS)
    ys = _experts(xs, block_e, pad_end, n_used, w1, w3, w2, layer)
    return _combine(x, ys, dest, gates.T, mod, g_final, n_ctx_tiles=n_ctx_tiles,
                    final_norm=final_norm)


def kernel(x_prompt, x_sample, state_rglru, c, c_ctx, ada_w, ada_b, norm_mix, norm_moe, norm_final, hy_w_in, hy_b_in, hy_conv_w, hy_conv_b, hy_f_w1, hy_f_b1, hy_f_freq, hy_f_w2, hy_f_b2, hy_f_w3, hy_f_skip, hy_w_out, hy_b_out, rg_w_in, rg_b_in, rg_conv_w, rg_conv_b, rg_wa, rg_ba, rg_wx, rg_bx, rg_lambda, rg_w_out, rg_b_out, router_w, router_bias, moe_w1, moe_w3, moe_w2):
    nb_ctx, len_ctx, d = x_prompt.shape
    nb_lat, len_lat, _ = x_sample.shape
    depth = ada_w.shape[0]
    n_rg = rg_w_in.shape[0]
    d_rnn = rg_w_out.shape[1]
    tok_ctx = nb_ctx * len_ctx
    tiles_ctx = tok_ctx // TOKEN_TILE
    assert tok_ctx % TOKEN_TILE == 0 and TOKEN_TILE % len_ctx == 0 and TOKEN_TILE % GRID_W == 0
    assert len_lat % TOKEN_TILE == 0 and tok_ctx % len_lat == 0
    assert V7X_SUBLANES - (nb_lat + 1) >= 0
    lat_off = tok_ctx // len_lat

    x = jnp.concatenate([x_prompt.reshape(tok_ctx, d), x_sample.reshape(nb_lat * len_lat, d)], 0)
    cond = jnp.concatenate([c_ctx[None, :], c,
                            jnp.zeros((V7X_SUBLANES - 1 - nb_lat, d), F32)], axis=0)
    mods = _ada_modulation(cond, ada_w, ada_b)[:, :1 + nb_lat]
    ctx_h0 = jnp.zeros((nb_ctx, 2, d_rnn), F32)
    states = []

    for i in range(depth):
        mod = mods[i]
        j = i // 2
        if i % 2 == 0:
            u = _inproj(x, mod, norm_mix[i], hy_w_in[j].astype(BF16), hy_b_in[j], hy_conv_w[j],
                        hy_conv_b[j], n_ctx_tiles=tiles_ctx, ctx_len=len_ctx, lat_len=GRID_W,
                        plain_cols=0)
            filt_w = (hy_f_w1[j], hy_f_b1[j], hy_f_freq[j], hy_f_w2[j], hy_f_b2[j], hy_f_w3[j])
            y_ctx = _hyena_stream(u, 0, nb_ctx, len_ctx, filt_w, hy_f_skip[j], d)
            y_lat = _hyena_stream(u, lat_off, nb_lat, len_lat, filt_w, hy_f_skip[j], d)
            x = _outproj(y_ctx, y_lat, hy_w_out[j].astype(BF16), hy_b_out[j], mod, x,
                         n_ctx_tiles=tiles_ctx)
        else:
            u = _inproj(x, mod, norm_mix[i], rg_w_in[j].astype(BF16), rg_b_in[j], rg_conv_w[j],
                        rg_conv_b[j], n_ctx_tiles=tiles_ctx, ctx_len=len_ctx, lat_len=GRID_W,
                        plain_cols=d_rnn)
            wa, wx = rg_wa[j].astype(BF16), rg_wx[j].astype(BF16)
            y_ctx, st = _rglru_stream(u, ctx_h0, 0, nb_ctx, len_ctx, wa, wx, rg_ba[j], rg_bx[j],
                                      rg_lambda[j])
            y_lat, _ = _rglru_stream(u, state_rglru[:, j], lat_off, nb_lat, len_lat, wa, wx,
                                     rg_ba[j], rg_bx[j], rg_lambda[j])
            states.append(st)
            x = _outproj(y_ctx, y_lat, rg_w_out[j].astype(BF16), rg_b_out[j], mod, x,
                         n_ctx_tiles=tiles_ctx)
        x = _moe(x, mod, norm_moe[i], router_w, router_bias, moe_w1, moe_w3, moe_w2, i,
                 norm_final, n_ctx_tiles=tiles_ctx, final_norm=(i == depth - 1))

    y_prompt = x[0].reshape(nb_ctx, len_ctx, d)
    y_sample = x[1].reshape(nb_lat, len_lat, d)
    new_state = jnp.stack(states, axis=1).astype(x_prompt.dtype)
    return (y_prompt, y_sample, new_state)
```

```python
import functools
import math

import numpy as np
import jax
import jax.numpy as jnp
from jax import lax
from jax.experimental import pallas as pl
from jax.experimental.pallas import tpu as pltpu

F32 = jnp.float32
BF16 = jnp.bfloat16

GRID_W = 64
FILTER_BANDS = 16
FILTER_EPS = 1e-6
MIN_DECAY = math.log(1e-2) / 0.3
MAX_DECAY = math.log(1e-2) / 1.5
RG_C = 8.0
N_GROUPS = 4
TOP_K = 2
NORM_EPS = 1e-6

V7X_LANES = 128
V7X_SUBLANES = 8
V7X_VMEM_BYTES = 64 * 1024 * 1024

TOKEN_TILE = 512
COL_TILE = 1024
MOE_ROWS = 256
DFT_ROWS = 512
DFT_SPLIT = 64
RG_COLS = 256
RG_ROWS = 1024
ROW_DMA_UNROLL = 8


def _params(sem, vmem_mb):
    return pltpu.CompilerParams(dimension_semantics=sem,
                                vmem_limit_bytes=vmem_mb * 1024 * 1024)


def _dot(a, b):
    return jnp.dot(a, b, preferred_element_type=F32)


def _split(x):
    hi = x.astype(BF16)
    lo = (x - hi.astype(F32)).astype(BF16)
    return hi, lo


def _dot3(a, b):
    ah, al = _split(a)
    bh, bl = _split(b)
    return _dot(ah, bh) + (_dot(ah, bl) + _dot(al, bh))


def _dot3_nt(a, b):
    dn = (((1,), (1,)), ((), ()))
    d = lambda x, y: lax.dot_general(x, y, dn, preferred_element_type=F32)
    ah, al = _split(a)
    bh, bl = _split(b)
    return d(ah, bh) + (d(ah, bl) + d(al, bh))


def _sigmoid(x):
    return 1.0 / (1.0 + jnp.exp(-x))


def _sigmoid_tanh(x):
    return 0.5 * jnp.tanh(0.5 * x) + 0.5


def _norm_mod(x, g, shift, scale):
    ms = jnp.mean(x * x, axis=-1, keepdims=True)
    return (x * lax.rsqrt(ms + NORM_EPS) * g) * (1.0 + scale) + shift


def _ada_kernel(c_ref, w_ref, b_ref, o_ref):
    c = c_ref[...]
    o_ref[...] = _dot3(c * _sigmoid(c), w_ref[...]) + b_ref[...]


def _ada_modulation(cond, ada_w, ada_b):
    depth, d, n = ada_w.shape
    tn = n // 4
    out = pl.pallas_call(
        _ada_kernel,
        out_shape=jax.ShapeDtypeStruct((depth, cond.shape[0], n), F32),
        grid=(depth, n // tn),
        in_specs=[pl.BlockSpec(cond.shape, lambda i, j: (0, 0)),
                  pl.BlockSpec((None, d, tn), lambda i, j: (i, 0, j)),
                  pl.BlockSpec((None, 1, tn), lambda i, j: (i, 0, j))],
        out_specs=pl.BlockSpec((None, cond.shape[0], tn), lambda i, j: (i, 0, j)),
        compiler_params=_params(("parallel", "parallel"), 40),
        name="ada_modulation",
    )(cond, ada_w, ada_b.reshape(depth, 1, n))
    return out.reshape(depth, cond.shape[0], 6, d)


def _inproj_kernel(x_ref, mod_ref, g_ref, w_ref, b_ref, cw_ref, cb_ref, mk_ref, o_ref, h_scr, *,
                   sub, width, plain_col_tiles):
    j = pl.program_id(1)

    @pl.when(j == 0)
    def _():
        h = _norm_mod(x_ref[...], g_ref[...], mod_ref[0:1, :], mod_ref[1:2, :])
        h_scr[...] = h.astype(BF16)

    blocks = [slice(s, s + sub) for s in range(0, x_ref.shape[0], sub)]

    def project(rows):
        return _dot(h_scr[rows, :], w_ref[...]) + b_ref[...]

    def conv():
        pad_l = (width - 1) // 2
        for rows in blocks:
            u = project(rows)
            acc = cw_ref[pad_l:pad_l + 1, :] * u + cb_ref[...]
            s = 0
            for k in range(width):
                off = k - pad_l
                if off != 0:
                    shifted = pltpu.roll(u, (-off) % sub, axis=0)
                    acc = acc + cw_ref[k:k + 1, :] * (shifted * mk_ref[s])
                    s += 1
            o_ref[rows, :] = acc

    if plain_col_tiles == 0:
        conv()
    else:
        pl.when(j >= plain_col_tiles)(conv)

        @pl.when(j < plain_col_tiles)
        def _():
            for rows in blocks:
                o_ref[rows, :] = project(rows)


def _inproj(x, mod, g, w, b, conv_w, conv_b, *, n_ctx_tiles, ctx_len, lat_len, plain_cols):
    t, d = x.shape
    n = w.shape[1]
    tm, tn = TOKEN_TILE, COL_TILE
    width = conv_w.shape[0]
    plain_tiles = plain_cols // tn
    lat_tiles = (t // tm - n_ctx_tiles) // (mod.shape[0] - 1)

    def cond_of(i):
        return jnp.where(i < n_ctx_tiles, 0, 1 + (i - n_ctx_tiles) // lat_tiles)

    sub = max(ctx_len, lat_len)
    assert sub % ctx_len == 0 and sub % lat_len == 0 and tm % sub == 0
    pad_l = (width - 1) // 2
    offs = jnp.asarray([k - pad_l for k in range(width) if k != pad_l], jnp.int32)
    seg = jnp.asarray([ctx_len, lat_len], jnp.int32)[:, None, None]
    pos = jnp.arange(sub, dtype=jnp.int32)[None, None, :] % seg + offs[None, :, None]
    masks = jnp.broadcast_to(((pos >= 0) & (pos < seg)).astype(F32)[..., None],
                             (2, width - 1, sub, tn))

    kern = functools.partial(_inproj_kernel, sub=sub, width=width, plain_col_tiles=plain_tiles)
    return pl.pallas_call(
        kern,
        out_shape=jax.ShapeDtypeStruct((t, n), F32),
        grid=(t // tm, n // tn),
        in_specs=[pl.BlockSpec((tm, d), lambda i, j: (i, 0)),
                  pl.BlockSpec((None, 6, d), lambda i, j: (cond_of(i), 0, 0)),
                  pl.BlockSpec((1, d), lambda i, j: (0, 0)),
                  pl.BlockSpec((d, tn), lambda i, j: (0, j)),
                  pl.BlockSpec((1, tn), lambda i, j: (0, j)),
                  pl.BlockSpec((width, tn), lambda i, j: (0, jnp.maximum(j - plain_tiles, 0))),
                  pl.BlockSpec((1, tn), lambda i, j: (0, jnp.maximum(j - plain_tiles, 0))),
                  pl.BlockSpec((None, width - 1, sub, tn),
                               lambda i, j: (jnp.where(i < n_ctx_tiles, 0, 1), 0, 0, 0))],
        out_specs=pl.BlockSpec((tm, tn), lambda i, j: (i, j)),
        scratch_shapes=[pltpu.VMEM((tm, d), BF16)],
        compiler_params=_params(("parallel", "arbitrary"), 48),
        name="norm_inproj_conv",
    )(x, mod, g.reshape(1, d), w, b.reshape(1, n), conv_w, conv_b.reshape(1, -1), masks)


def _outproj_kernel(yc_ref, yl_ref, w_ref, b_ref, mod_ref, x_ref, o_ref, *, n_ctx_tiles):
    i = pl.program_id(0)

    def run(y_ref):
        m = _dot(y_ref[...], w_ref[...]) + b_ref[...]
        o_ref[...] = x_ref[...] + mod_ref[2:3, :] * m

    pl.when(i < n_ctx_tiles)(lambda: run(yc_ref))
    pl.when(i >= n_ctx_tiles)(lambda: run(yl_ref))


def _outproj(y_ctx, y_lat, w, b, mod, x, *, n_ctx_tiles):
    t, d = x.shape
    tm = TOKEN_TILE
    lat_tiles = (t // tm - n_ctx_tiles) // (mod.shape[0] - 1)

    def cond_of(i):
        return jnp.where(i < n_ctx_tiles, 0, 1 + (i - n_ctx_tiles) // lat_tiles)

    return pl.pallas_call(
        functools.partial(_outproj_kernel, n_ctx_tiles=n_ctx_tiles),
        out_shape=jax.ShapeDtypeStruct((t, d), F32),
        grid=(t // tm,),
        in_specs=[pl.BlockSpec((tm, w.shape[0]), lambda i: (jnp.minimum(i, n_ctx_tiles - 1), 0)),
                  pl.BlockSpec((tm, w.shape[0]), lambda i: (jnp.maximum(i - n_ctx_tiles, 0), 0)),
                  pl.BlockSpec(w.shape, lambda i: (0, 0)),
                  pl.BlockSpec((1, d), lambda i: (0, 0)),
                  pl.BlockSpec((None, 6, d), lambda i: (cond_of(i), 0, 0)),
                  pl.BlockSpec((tm, d), lambda i: (i, 0))],
        out_specs=pl.BlockSpec((tm, d), lambda i: (i, 0)),
        input_output_aliases={5: 0},
        compiler_params=_params(("parallel",), 32),
        name="outproj_residual",
    )(y_ctx, y_lat, w, b.reshape(1, d), mod, x)


def _filter_kernel(z_ref, w1_ref, b1_ref, fr_ref, w2_ref, b2_ref, w3a_ref, w3b_ref, w3c_ref,
                   w3d_ref, t_ref, dl_ref, o_ref, h_scr, *, orders):
    @pl.when(pl.program_id(0) == 0)
    def _():
        fr = fr_ref[...]
        h1 = jnp.sin(fr * (_dot3(z_ref[...], w1_ref[...]) + b1_ref[...]))
        h_scr[...] = jnp.sin(fr * (_dot3(h1, w2_ref[...]) + b2_ref[...]))

    h = h_scr[...]
    window = jnp.exp(-t_ref[...] * dl_ref[...])
    row = lax.broadcasted_iota(jnp.int32, (h.shape[0], 1), 0)
    w3 = ((w3a_ref, w3b_ref), (w3c_ref, w3d_ref))
    for o in range(orders):
        h_fwd = _dot3(h, w3[0][o][...]) * window
        h_bwd = jnp.where(row == 0, 0.0, _dot3(h, w3[1][o][...]) * window)
        norm = (jnp.sum(jnp.abs(h_fwd), axis=0, keepdims=True)
                + jnp.sum(jnp.abs(h_bwd), axis=0, keepdims=True) + FILTER_EPS)
        o_ref[2 * o] = h_fwd / norm
        o_ref[2 * o + 1] = h_bwd / norm


def _hyena_filter(length, f_w1, f_b1, f_freq, f_w2, f_b2, f_w3, d):
    emb, hid = f_w1.shape
    orders = f_w3.shape[1] // (2 * d)
    t = jnp.linspace(0.0, 1.0, length, dtype=F32)[:, None]
    w = (2.0 * math.pi / length) * jnp.arange(length, dtype=F32)[:, None]
    bands = jnp.linspace(1e-4, FILTER_BANDS - 1, FILTER_BANDS, dtype=F32)[None, :]
    z = jnp.concatenate([t, jnp.cos(bands * w), -jnp.sin(bands * w)], axis=-1)
    emb_pad = V7X_LANES
    z = jnp.pad(z, ((0, 0), (0, emb_pad - emb)))
    w1 = jnp.pad(f_w1, ((0, emb_pad - emb), (0, 0)))
    deltas = jnp.abs(jnp.linspace(MIN_DECAY, MAX_DECAY, d, dtype=F32))[None, :]
    td = 256
    nd = d // td
    full = lambda shape: pl.BlockSpec(shape, lambda j: (0,) * len(shape))
    w3_spec = lambda side, o: pl.BlockSpec((hid, td), lambda j: (0, (side * orders + o) * nd + j))
    assert orders == 2
    return pl.pallas_call(
        functools.partial(_filter_kernel, orders=orders),
        out_shape=jax.ShapeDtypeStruct((2 * orders, length, d), F32),
        grid=(nd,),
        in_specs=[full((length, emb_pad)), full((emb_pad, hid)), full((1, hid)), full((1, hid)),
                  full((hid, hid)), full((1, hid)),
                  w3_spec(0, 0), w3_spec(0, 1), w3_spec(1, 0), w3_spec(1, 1),
                  full((length, 1)), pl.BlockSpec((1, td), lambda j: (0, j))],
        out_specs=pl.BlockSpec((2 * orders, length, td), lambda j: (0, 0, j)),
        scratch_shapes=[pltpu.VMEM((length, hid), F32)],
        compiler_params=_params(("arbitrary",), 48),
        name="hyena_filter",
    )(z, w1, f_b1.reshape(1, hid), f_freq.reshape(1, hid), f_w2, f_b2.reshape(1, hid),
      f_w3, f_w3, f_w3, f_w3, t, deltas)


def _dft_matrices(length):
    n = 2 * length
    f0n, f1n = DFT_SPLIT, length // DFT_SPLIT
    tt = np.arange(length, dtype=np.int64)[None, :]
    ang_a = 2.0 * np.pi * ((DFT_SPLIT * np.arange(f1n, dtype=np.int64)[:, None] * tt) % n) / n
    ang_b = 2.0 * np.pi * ((np.arange(f0n, dtype=np.int64)[:, None] * tt) % n) / n
    ca, sa = (jnp.asarray(f(ang_a), F32)[:, None, :] for f in (np.cos, np.sin))
    cb, sb = (jnp.asarray(f(ang_b), F32)[None, :, :] for f in (np.cos, np.sin))
    cos_m = (ca * cb - sa * sb).reshape(length, length)
    sin_m = (sa * cb + ca * sb).reshape(length, length)
    nyq = jnp.asarray(1.0 - 2.0 * (np.arange(length) % 2), F32)[None, :]
    f_is0 = (jnp.arange(length) == 0)[:, None]
    im_m = jnp.where(f_is0, nyq, -sin_m)
    th = DFT_ROWS // 2
    fwd = jnp.stack([cos_m.reshape(length // th, th, length),
                     im_m.reshape(length // th, th, length)], axis=1).reshape(n, length)
    cat, sat = (jnp.asarray(f(ang_a).T, F32)[:, :, None] for f in (np.cos, np.sin))
    cbt, sbt = (jnp.asarray(f(ang_b).T, F32)[:, None, :] for f in (np.cos, np.sin))
    f_is0_t = f_is0.T
    scale = jnp.where(f_is0_t, 1.0 / n, 2.0 / n)
    cos_t = (cat * cbt - sat * sbt).reshape(length, length) * scale
    im_t = jnp.where(f_is0_t, nyq.T, -(sat * cbt + cat * sbt).reshape(length, length)) * scale
    inv = jnp.stack([cos_t.reshape(length, length // th, th),
                     im_t.reshape(length, length // th, th)], axis=2).reshape(length, n)
    return fwd.astype(BF16), inv.astype(BF16)


def _spec_kernel(a_ref, hf_ref, hb_ref, o_ref, sum_scr, dif_scr, nyq_scr):
    m = pl.program_id(2)

    @pl.when(m == 0)
    def _():
        hf, hb = hf_ref[...], hb_ref[...]
        sum_scr[...] = (hf + hb).astype(BF16)
        dif_scr[...] = (hf - hb).astype(BF16)
        t = lax.broadcasted_iota(jnp.int32, (hb.shape[0], 1), 0)
        sign = (1 - 2 * (t & 1)).astype(F32)
        nyq_scr[...] = 2.0 * jnp.sum(sign * hb, axis=0, keepdims=True)

    th = a_ref.shape[0] // 2
    o_ref[0:th, :] = _dot(a_ref[0:th, :], sum_scr[...])
    im = _dot(a_ref[th:, :], dif_scr[...])
    row = lax.broadcasted_iota(jnp.int32, (th, 1), 0)
    o_ref[th:, :] = jnp.where((row == 0) & (m == 0), im + nyq_scr[...], im)


def _filter_spectrum(fwd, filt):
    n2, length, d = filt.shape
    orders = n2 // 2
    n = 2 * length
    tm, tc = DFT_ROWS, 512
    filt2 = filt.reshape(n2 * length, d)
    return pl.pallas_call(
        _spec_kernel,
        out_shape=jax.ShapeDtypeStruct((orders * n, d), F32),
        grid=(orders, d // tc, n // tm),
        in_specs=[pl.BlockSpec((tm, length), lambda o, c, m: (m, 0)),
                  pl.BlockSpec((length, tc), lambda o, c, m: (2 * o, c)),
                  pl.BlockSpec((length, tc), lambda o, c, m: (2 * o + 1, c))],
        out_specs=pl.BlockSpec((tm, tc), lambda o, c, m: (o * (n // tm) + m, c)),
        scratch_shapes=[pltpu.VMEM((length, tc), BF16), pltpu.VMEM((length, tc), BF16),
                        pltpu.VMEM((1, tc), F32)],
        compiler_params=_params(("parallel", "parallel", "arbitrary"), 48),
        name="hyena_filter_spectrum",
    )(fwd, filt2, filt2)


def _dft_fwd_kernel(a_ref, z_ref, k_ref, o_ref, z_scr):
    m = pl.program_id(1)

    @pl.when(m == 0)
    def _():
        z_scr[...] = z_ref[...].astype(BF16)

    acc = _dot(a_ref[...], z_scr[...])
    th = acc.shape[0] // 2
    xr, xi = acc[:th], acc[th:]
    kr, ki = k_ref[0:th, :], k_ref[th:, :]
    row = lax.broadcasted_iota(jnp.int32, (th, 1), 0)
    dc = (row == 0) & (m == 0)
    o_ref[0:th, :] = (xr * kr - jnp.where(dc, 0.0, xi * ki)).astype(o_ref.dtype)
    o_ref[th:, :] = jnp.where(dc, xi * ki, xr * ki + xi * kr).astype(o_ref.dtype)


def _dft_forward(fwd, src, spec, *, nb, length, row_off, col_blk, order):
    n = 2 * length
    d = spec.shape[1]
    tm = DFT_ROWS
    nt = n // tm
    return pl.pallas_call(
        _dft_fwd_kernel,
        out_shape=jax.ShapeDtypeStruct((nb * n, d), BF16),
        grid=(nb, nt),
        in_specs=[pl.BlockSpec((tm, length), lambda b, m: (m, 0)),
                  pl.BlockSpec((length, d), lambda b, m: (row_off + b, col_blk)),
                  pl.BlockSpec((tm, d), lambda b, m: (order * nt + m, 0))],
        out_specs=pl.BlockSpec((tm, d), lambda b, m: (b * nt + m, 0)),
        scratch_shapes=[pltpu.VMEM((length, d), BF16)],
        compiler_params=_params(("parallel", "arbitrary"), 48),
        name="hyena_dft_forward",
    )(fwd, src, spec)


def _dft_inv_kernel(a_ref, p_ref, v_ref, g_ref, s_ref, o_ref):
    y = _dot(a_ref[...], p_ref[...])
    v = v_ref[...]
    o_ref[...] = (g_ref[...] * (y + v * s_ref[...])).astype(o_ref.dtype)


def _dft_inverse(inv, prod, vsrc, gsrc, skip, *, nb, length, v_off, v_col, g_off, g_col,
                 out_dtype):
    n = 2 * length
    d = prod.shape[1]
    tm = min(DFT_ROWS, length)
    nt = length // tm
    return pl.pallas_call(
        _dft_inv_kernel,
        out_shape=jax.ShapeDtypeStruct((nb * length, d), out_dtype),
        grid=(nb, nt),
        in_specs=[pl.BlockSpec((tm, n), lambda b, m: (m, 0)),
                  pl.BlockSpec((n, d), lambda b, m: (b, 0)),
                  pl.BlockSpec((tm, d), lambda b, m: ((v_off + b) * nt + m, v_col)),
                  pl.BlockSpec((tm, d), lambda b, m: ((g_off + b) * nt + m, g_col)),
                  pl.BlockSpec((1, d), lambda b, m: (0, 0))],
        out_specs=pl.BlockSpec((tm, d), lambda b, m: (b * nt + m, 0)),
        compiler_params=_params(("parallel", "arbitrary"), 48),
        name="hyena_dft_inverse",
    )(inv, prod, vsrc, gsrc, skip.reshape(1, d))


def _spectral_product(acc, k_ref, order):
    th = acc.shape[0] // 2
    xr, xi = acc[:th], acc[th:]
    kr, ki = k_ref[order, 0:th, :], k_ref[order, th:, :]
    dc = lax.broadcasted_iota(jnp.int32, (th, 1), 0) == 0
    pr = xr * kr - jnp.where(dc, 0.0, xi * ki)
    pi = jnp.where(dc, xi * ki, xr * ki + xi * kr)
    return jnp.concatenate([pr, pi], axis=0).astype(BF16)


def _hyena_short_kernel(u_ref, f_ref, i_ref, k_ref, s_ref, o_ref):
    d = o_ref.shape[1]
    fwd, inv = f_ref[...], i_ref[...]
    z = u_ref[:, 0:d]
    for order in range(2):
        p = _spectral_product(_dot(fwd, z.astype(BF16)), k_ref, order)
        gate = u_ref[:, (order + 1) * d:(order + 2) * d]
        z = gate * (_dot(inv, p) + z * s_ref[order:order + 1, :])
    o_ref[...] = z.astype(o_ref.dtype)


def _hyena_short(u, fwd, inv, spec, f_skip, *, nb, length):
    n = 2 * length
    d = spec.shape[1]
    assert n == DFT_ROWS and spec.shape[0] == 2 * n
    return pl.pallas_call(
        _hyena_short_kernel,
        out_shape=jax.ShapeDtypeStruct((nb * length, d), BF16),
        grid=(nb,),
        in_specs=[pl.BlockSpec((length, 3 * d), lambda b: (b, 0)),
                  pl.BlockSpec((n, length), lambda b: (0, 0)),
                  pl.BlockSpec((length, n), lambda b: (0, 0)),
                  pl.BlockSpec((2, n, d), lambda b: (0, 0, 0)),
                  pl.BlockSpec((2, d), lambda b: (0, 0))],
        out_specs=pl.BlockSpec((length, d), lambda b: (b, 0)),
        compiler_params=_params(("parallel",), 48),
        name="hyena_short_sequences",
    )(u, fwd, inv, spec.reshape(2, n, d), f_skip)


def _hyena_stream(u, row_off, nb, length, filt_w, f_skip, d):
    fwd, inv = _dft_matrices(length)
    filt = _hyena_filter(length, *filt_w, d)
    spec = _filter_spectrum(fwd, filt)
    if 2 * length == DFT_ROWS:
        assert row_off == 0
        return _hyena_short(u, fwd, inv, spec, f_skip, nb=nb, length=length)
    p1 = _dft_forward(fwd, u, spec, nb=nb, length=length, row_off=row_off, col_blk=0, order=0)
    z1 = _dft_inverse(inv, p1, u, u, f_skip[0], nb=nb, length=length, v_off=row_off, v_col=0,
                      g_off=row_off, g_col=1, out_dtype=F32)
    p2 = _dft_forward(fwd, z1, spec, nb=nb, length=length, row_off=0, col_blk=0, order=1)
    return _dft_inverse(inv, p2, z1, u, f_skip[1], nb=nb, length=length, v_off=0, v_col=0,
                        g_off=row_off, g_col=2, out_dtype=BF16)


def _gelu_tanh(x):
    return 0.5 * x * (1.0 + jnp.tanh(math.sqrt(2.0 / math.pi) * (x + 0.044715 * (x * x * x))))


def _log1p(e):
    u = 1.0 + e
    d = u - 1.0
    return jnp.where(d == 0.0, e, jnp.log(u) * (e / jnp.where(d == 0.0, 1.0, d)))


def _rglru_kernel(gate_ref, rec_ref, wa_ref, wx_ref, ba_ref, bx_ref, lam_ref, h0_ref,
                  y_ref, st_ref, a_scr, b_scr):
    length, cols = rec_ref.shape
    groups = length // V7X_SUBLANES
    rec = rec_ref[...]
    rec16 = rec.astype(BF16)
    pos = lax.broadcasted_iota(jnp.int32, (groups, V7X_SUBLANES, cols), 1)

    for d in range(2):
        r_gate = _sigmoid_tanh(_dot(rec16, wa_ref[d]) + ba_ref[d])
        i_gate = _sigmoid_tanh(_dot(rec16, wx_ref[d]) + bx_ref[d])
        nlam = -lam_ref[d]
        softplus = jnp.maximum(nlam, 0.0) + _log1p(jnp.exp(-jnp.abs(nlam)))
        log_a = (-RG_C * softplus) * r_gate
        a = jnp.exp(log_a).reshape(groups, V7X_SUBLANES, cols)
        th = jnp.tanh(log_a)
        b = (jnp.sqrt(-2.0 * th / (1.0 - th)) * (i_gate * rec)).reshape(
            groups, V7X_SUBLANES, cols)
        for s in (1, 2, 4):
            if d == 0:
                a_sh = pltpu.roll(a, s, axis=1)
                b_sh = pltpu.roll(b, s, axis=1)
                live = pos >= s
            else:
                a_sh = pltpu.roll(a, V7X_SUBLANES - s, axis=1)
                b_sh = pltpu.roll(b, V7X_SUBLANES - s, axis=1)
                live = pos < V7X_SUBLANES - s
            b = jnp.where(live, a * b_sh, 0.0) + b
            a = jnp.where(live, a * a_sh, a)
        a_scr[d] = a.reshape(length, cols)
        b_scr[d] = b.reshape(length, cols)

    seqs = h0_ref.shape[0]
    seq_groups = groups // seqs

    def step(g, carry):
        out = []
        for q in range(seqs):
            cf, cb = carry[2 * q], carry[2 * q + 1]
            rf = pl.multiple_of((q * seq_groups + g) * V7X_SUBLANES, V7X_SUBLANES)
            rb = pl.multiple_of(((q + 1) * seq_groups - 1 - g) * V7X_SUBLANES, V7X_SUBLANES)
            hf = a_scr[0, pl.ds(rf, V7X_SUBLANES), :] * cf + b_scr[0, pl.ds(rf, V7X_SUBLANES), :]
            hb = a_scr[1, pl.ds(rb, V7X_SUBLANES), :] * cb + b_scr[1, pl.ds(rb, V7X_SUBLANES), :]
            b_scr[0, pl.ds(rf, V7X_SUBLANES), :] = hf
            b_scr[1, pl.ds(rb, V7X_SUBLANES), :] = hb
            out.append(jnp.broadcast_to(hf[V7X_SUBLANES - 1:V7X_SUBLANES, :], hf.shape))
            out.append(jnp.broadcast_to(hb[0:1, :], hb.shape))
        return tuple(out)

    init = tuple(jnp.broadcast_to(h0_ref[q, d], (V7X_SUBLANES, cols))
                 for q in range(seqs) for d in range(2))
    final = lax.fori_loop(0, seq_groups, step, init, unroll=max(1, 4 // seqs))
    for q in range(seqs):
        for d in range(2):
            st_ref[q, d] = final[2 * q + d][0:1, :]
    y_ref[...] = ((b_scr[0] + b_scr[1]) * _gelu_tanh(gate_ref[...])).astype(y_ref.dtype)


def _rglru_stream(u, h0, row_off, nb, length, wa, wx, ba, bx, lam):
    dr = u.shape[1] // 2
    tc = RG_COLS
    nc = dr // tc
    seqs = max(1, min(nb, RG_ROWS // length))
    assert nb % seqs == 0 and (row_off * length) % (seqs * length) == 0
    rows = seqs * length
    blk_off = row_off // seqs
    vec = lambda a: a.reshape(2, 1, dr)
    vec_spec = pl.BlockSpec((2, 1, tc), lambda b, c: (0, 0, c))
    w_spec = pl.BlockSpec((2, None, tc, tc), lambda b, c: (0, c, 0, 0))
    st_spec = pl.BlockSpec((seqs, 2, 1, tc), lambda b, c: (b, 0, 0, c))
    y, st = pl.pallas_call(
        _rglru_kernel,
        out_shape=(jax.ShapeDtypeStruct((nb * length, dr), BF16),
                   jax.ShapeDtypeStruct((nb, 2, 1, dr), F32)),
        grid=(nb // seqs, nc),
        in_specs=[pl.BlockSpec((rows, tc), lambda b, c: (blk_off + b, c)),
                  pl.BlockSpec((rows, tc), lambda b, c: (blk_off + b, nc + c)),
                  w_spec, w_spec, vec_spec, vec_spec, vec_spec, st_spec],
        out_specs=(pl.BlockSpec((rows, tc), lambda b, c: (b, c)), st_spec),
        scratch_shapes=[pltpu.VMEM((2, rows, tc), F32), pltpu.VMEM((2, rows, tc), F32)],
        compiler_params=_params(("parallel", "parallel"), 48),
        name="rglru_scan",
    )(u, u, wa, wx, vec(ba), vec(bx), vec(lam), h0.reshape(nb, 2, 1, dr))
    return y, st.reshape(nb, 2, dr)


def _router_kernel(x_ref, mod_ref, g_ref, rwt_ref, bias_ref, h_ref, idx_ref, gate_ref,
                   rank_ref, cnt_ref, tri_scr, carry_scr, *, n_experts):
    i = pl.program_id(0)
    tm = x_ref.shape[0]
    per_group = n_experts // N_GROUPS

    @pl.when(i == 0)
    def _():
        r = lax.broadcasted_iota(jnp.int32, (tm, tm), 0)
        c = lax.broadcasted_iota(jnp.int32, (tm, tm), 1)
        tri_scr[...] = jnp.where(r < c, 1.0, 0.0).astype(BF16)
        carry_scr[...] = jnp.zeros_like(carry_scr)

    h = _norm_mod(x_ref[...], g_ref[...], mod_ref[3:4, :], mod_ref[4:5, :])
    h_ref[...] = h
    logits = _dot3_nt(rwt_ref[...], h)
    p = jnp.exp(logits - jnp.max(logits, axis=0, keepdims=True))
    scores = p / jnp.sum(p, axis=0, keepdims=True)
    sel = scores + bias_ref[...]
    rows = [sel[e:e + 1, :] for e in range(n_experts)]

    best_val = None
    for gi in range(N_GROUPS):
        v = rows[gi * per_group:(gi + 1) * per_group]
        pair = None
        for a in range(per_group):
            for b in range(a + 1, per_group):
                s = v[a] + v[b]
                pair = s if pair is None else jnp.maximum(pair, s)
        if best_val is None:
            best_val, best_grp = pair, jnp.zeros_like(pair, dtype=jnp.int32)
        else:
            take = pair > best_val
            best_val = jnp.where(take, pair, best_val)
            best_grp = jnp.where(take, gi, best_grp)

    neg = jnp.float32(-jnp.inf)
    masked = [jnp.where(best_grp == e // per_group, rows[e], neg) for e in range(n_experts)]

    def argmax_first(vals):
        bv, bi = vals[0], jnp.zeros_like(best_grp)
        for e in range(1, n_experts):
            take = vals[e] > bv
            bv = jnp.where(take, vals[e], bv)
            bi = jnp.where(take, e, bi)
        return bi

    idx0 = argmax_first(masked)
    idx1 = argmax_first([jnp.where(idx0 == e, neg, masked[e]) for e in range(n_experts)])

    e_iota = lax.broadcasted_iota(jnp.int32, (n_experts, tm), 0)
    hit0 = e_iota == idx0
    hit1 = e_iota == idx1
    g0 = jnp.sum(jnp.where(hit0, scores, 0.0), axis=0, keepdims=True)
    g1 = jnp.sum(jnp.where(hit1, scores, 0.0), axis=0, keepdims=True)
    gsum = g0 + g1
    onehot = jnp.where(hit0 | hit1, 1.0, 0.0)
    before = _dot(onehot.astype(BF16), tri_scr[...]) + carry_scr[:, 0:1]
    r0 = jnp.sum(jnp.where(hit0, before, 0.0), axis=0, keepdims=True)
    r1 = jnp.sum(jnp.where(hit1, before, 0.0), axis=0, keepdims=True)
    idx_ref[0:1, :] = idx0
    idx_ref[1:2, :] = idx1
    gate_ref[0:1, :] = g0 / gsum
    gate_ref[1:2, :] = g1 / gsum
    rank_ref[0:1, :] = r0.astype(jnp.int32)
    rank_ref[1:2, :] = r1.astype(jnp.int32)
    carry_scr[...] = carry_scr[...] + jnp.sum(onehot, axis=1, keepdims=True)
    cnt_ref[...] = carry_scr[...]


def _router(x, mod, g, router_w, router_bias, *, n_ctx_tiles):
    t, d = x.shape
    ne = router_w.shape[1]
    tm = TOKEN_TILE
    lat_tiles = (t // tm - n_ctx_tiles) // (mod.shape[0] - 1)

    def cond_of(i):
        return jnp.where(i < n_ctx_tiles, 0, 1 + (i - n_ctx_tiles) // lat_tiles)

    row2 = pl.BlockSpec((TOP_K, tm), lambda i: (0, i))
    return pl.pallas_call(
        functools.partial(_router_kernel, n_experts=ne),
        out_shape=(jax.ShapeDtypeStruct((t, d), F32),
                   jax.ShapeDtypeStruct((TOP_K, t), jnp.int32),
                   jax.ShapeDtypeStruct((TOP_K, t), F32),
                   jax.ShapeDtypeStruct((TOP_K, t), jnp.int32),
                   jax.ShapeDtypeStruct((ne, V7X_LANES), F32)),
        grid=(t // tm,),
        in_specs=[pl.BlockSpec((tm, d), lambda i: (i, 0)),
                  pl.BlockSpec((None, 6, d), lambda i: (cond_of(i), 0, 0)),
                  pl.BlockSpec((1, d), lambda i: (0, 0)),
                  pl.BlockSpec((ne, d), lambda i: (0, 0)),
                  pl.BlockSpec((ne, 1), lambda i: (0, 0))],
        out_specs=(pl.BlockSpec((tm, d), lambda i: (i, 0)), row2, row2, row2,
                   pl.BlockSpec((ne, V7X_LANES), lambda i: (0, 0))),
        scratch_shapes=[pltpu.VMEM((tm, tm), BF16), pltpu.VMEM((ne, V7X_LANES), F32)],
        compiler_params=_params(("arbitrary",), 32),
        name="moe_router",
    )(x, mod, g.reshape(1, d), router_w.T, router_bias.reshape(ne, 1))


def _row_copy(src, src_row, dst, dst_row, sem):
    return pltpu.make_async_copy(src.at[pl.ds(src_row, 1)], dst.at[pl.ds(dst_row, 1)], sem)


def _dispatch_kernel(pad_row_ref, pad_len_ref, used_ref, dest_ref, h_ref, xs_out, zero_scr, sem,
                     pad_sem):
    tm = h_ref.shape[0]
    i = pl.program_id(0)
    n_experts = pad_row_ref.shape[0]
    zrows = zero_scr.shape[0]

    @pl.when(i == 0)
    def _():
        zero_scr[...] = jnp.zeros_like(zero_scr)

    def zero_copy(wanted, start, size):
        start = jnp.where(wanted, start, 0)
        if size >= V7X_SUBLANES:
            start = pl.multiple_of(start, V7X_SUBLANES)
        return wanted, pltpu.make_async_copy(zero_scr.at[pl.ds(0, size)],
                                             xs_out.at[pl.ds(start, size)], pad_sem)

    e = jnp.minimum(i, n_experts - 1)
    pad_len = jnp.where(i < n_experts, pad_len_ref[e], 0)
    pad_row = pad_row_ref[e]
    head = pad_len & (V7X_SUBLANES - 1)
    pad_copies = [zero_copy(r < head, pad_row + r, 1) for r in range(V7X_SUBLANES - 1)]
    size = zrows
    while size >= V7X_SUBLANES:
        start = pad_row + head + ((pad_len - head) & ~(2 * size - 1))
        pad_copies.append(zero_copy((pad_len & size) != 0, start, size))
        size //= 2
    spare = used_ref[0] + i
    has_spare = (i < n_experts) & (spare < xs_out.shape[0] // MOE_ROWS)
    for part in range(MOE_ROWS // zrows):
        pad_copies.append(zero_copy(has_spare, spare * MOE_ROWS + part * zrows, zrows))
    for wanted, copy in pad_copies:
        pl.when(wanted)(copy.start)

    def issue(g, c):
        for r in range(ROW_DMA_UNROLL):
            t = g * ROW_DMA_UNROLL + r
            for k in range(TOP_K):
                _row_copy(h_ref, t, xs_out, dest_ref[TOP_K * t + k], sem).start(priority=k)
        return c

    def drain(g, c):
        for _ in range(ROW_DMA_UNROLL * TOP_K):
            _row_copy(h_ref, 0, xs_out, 0, sem).wait()
        return c

    lax.fori_loop(0, tm // ROW_DMA_UNROLL, issue, 0)
    lax.fori_loop(0, tm // ROW_DMA_UNROLL, drain, 0)
    for wanted, copy in pad_copies:
        pl.when(wanted)(copy.wait)


def _dispatch(h, dest, pad_row, pad_len, n_used, n_slots):
    t, d = h.shape
    tm = TOKEN_TILE
    assert t // tm >= pad_row.shape[0] and n_slots % MOE_ROWS == 0
    return pl.pallas_call(
        _dispatch_kernel,
        out_shape=jax.ShapeDtypeStruct((n_slots, d), h.dtype),
        grid_spec=pltpu.PrefetchScalarGridSpec(
            num_scalar_prefetch=3,
            grid=(t // tm,),
            in_specs=[pl.BlockSpec((TOP_K * tm,), lambda i, pr, pn, nu: (i,),
                                   memory_space=pltpu.SMEM),
                      pl.BlockSpec((tm, d), lambda i, pr, pn, nu: (i, 0))],
            out_specs=pl.BlockSpec(memory_space=pl.ANY),
            scratch_shapes=[pltpu.VMEM((MOE_ROWS // 2, d), h.dtype),
                            pltpu.SemaphoreType.DMA, pltpu.SemaphoreType.DMA]),
        compiler_params=_params(("arbitrary",), 32),
        name="moe_dispatch",
    )(pad_row, pad_len, n_used, dest.reshape(-1), h)


def _experts_kernel(be_ref, first_ref, slot_ref, next_ref, nu_ref, xs_ref, w1_hbm, w3_hbm, w2_hbm,
                    ys_ref, wbuf, w1_s, w3_s, w2_s, sem, *, layer):
    i = pl.program_id(0)
    used = i < nu_ref[0]
    w_hbm = (w1_hbm, w3_hbm, w2_hbm)
    w_s = (w1_s, w3_s, w2_s)

    def weight_copies(expert, slot):
        return [pltpu.make_async_copy(w_hbm[m].at[layer, expert], wbuf.at[slot, m],
                                      sem.at[slot, m]) for m in range(3)]

    @pl.when(i == 0)
    def _():
        for copy in weight_copies(be_ref[0], slot_ref[0]):
            copy.start()

    @pl.when(used & (first_ref[i] != 0))
    def _():
        slot = slot_ref[i]
        for copy in weight_copies(be_ref[i], slot):
            copy.wait()
        for m in range(3):
            w_s[m][...] = wbuf[slot, m].astype(BF16)
        nxt = next_ref[i]

        @pl.when(nxt >= 0)
        def _():
            for copy in weight_copies(nxt, 1 - slot):
                copy.start()

    @pl.when(used)
    def _():
        x = xs_ref[...].astype(BF16)
        h1 = _dot(x, w1_s[...])
        h3 = _dot(x, w3_s[...])
        act = (h1 * _sigmoid(h1)) * h3
        ys_ref[...] = _dot(act.astype(BF16), w2_s[...])

    @pl.when(i >= nu_ref[0])
    def _():
        ys_ref[...] = jnp.zeros_like(ys_ref)


def _experts(xs, block_e, pad_end, n_used, w1, w3, w2, layer):
    n_slots, d = xs.shape
    de = w1.shape[3]
    assert d == de
    tm = MOE_ROWS
    n_blocks = n_slots // tm
    prev_e = jnp.concatenate([jnp.full((1,), -1, jnp.int32), block_e[:-1]])
    first = (block_e != prev_e).astype(jnp.int32)
    slot = (jnp.cumsum(first) - 1) & 1
    run_end = pad_end[block_e] // tm
    nxt = jnp.where(run_end < n_used[0], block_e[jnp.minimum(run_end, n_blocks - 1)], -1)
    any_spec = pl.BlockSpec(memory_space=pl.ANY)
    row_spec = lambda f: pl.BlockSpec((tm, d), lambda i, be, fi, sl, nx, nu: (f(i, nu), 0))
    return pl.pallas_call(
        functools.partial(_experts_kernel, layer=layer),
        out_shape=jax.ShapeDtypeStruct((n_slots, d), F32),
        grid_spec=pltpu.PrefetchScalarGridSpec(
            num_scalar_prefetch=5,
            grid=(n_blocks,),
            in_specs=[row_spec(lambda i, nu: jnp.minimum(i, nu[0] - 1)),
                      any_spec, any_spec, any_spec],
            out_specs=row_spec(lambda i, nu: i),
            scratch_shapes=[pltpu.VMEM((2, 3, d, de), F32),
                            pltpu.VMEM((d, de), BF16), pltpu.VMEM((d, de), BF16),
                            pltpu.VMEM((de, d), BF16), pltpu.SemaphoreType.DMA((2, 3))]),
        compiler_params=_params(("arbitrary",), 52),
        name="moe_experts",
    )(block_e, first, slot.astype(jnp.int32), nxt.astype(jnp.int32), n_used, xs, w1, w3, w2)


def _combine_kernel(dest_ref, next_ref, x_ref, gate_ref, mod_ref, gf_ref, ys_hbm, *rest,
                    n_ctx_tiles, final_norm):
    buf, sem = rest[-2:]
    tm = x_ref.shape[0]
    i = pl.program_id(0)
    slot = i & 1

    def gather(idx_ref, into):
        def issue(g, c):
            for r in range(ROW_DMA_UNROLL):
                t = g * ROW_DMA_UNROLL + r
                for k in range(TOP_K):
                    _row_copy(ys_hbm, idx_ref[TOP_K * t + k], buf.at[into, k], t,
                              sem.at[into]).start(priority=k)
            return c
        lax.fori_loop(0, tm // ROW_DMA_UNROLL, issue, 0)

    pl.when(i == 0)(lambda: gather(dest_ref, slot))
    pl.when(i + 1 < pl.num_programs(0))(lambda: gather(next_ref, 1 - slot))

    def drain(g, c):
        for _ in range(ROW_DMA_UNROLL * TOP_K):
            _row_copy(ys_hbm, 0, buf.at[slot, 0], 0, sem.at[slot]).wait()
        return c

    lax.fori_loop(0, tm // ROW_DMA_UNROLL, drain, 0)
    gate = gate_ref[...]
    m = gate[:, 0:1] * buf[slot, 0] + gate[:, 1:2] * buf[slot, 1]
    x = x_ref[...] + mod_ref[5:6, :] * m
    if not final_norm:
        rest[0][...] = x
        return
    ms = jnp.mean(x * x, axis=-1, keepdims=True)
    x = x * lax.rsqrt(ms + NORM_EPS) * gf_ref[...]
    @pl.when(i < n_ctx_tiles)
    def _():
        rest[0][...] = x

    @pl.when(i >= n_ctx_tiles)
    def _():
        rest[1][...] = x


def _combine(x, ys, dest, gates, mod, g_final, *, n_ctx_tiles, final_norm):
    t, d = x.shape
    tm = TOKEN_TILE
    lat_tiles = (t // tm - n_ctx_tiles) // (mod.shape[0] - 1)

    def cond_of(i):
        return jnp.where(i < n_ctx_tiles, 0, 1 + (i - n_ctx_tiles) // lat_tiles)

    if final_norm:
        out_shape = (jax.ShapeDtypeStruct((n_ctx_tiles * tm, d), F32),
                     jax.ShapeDtypeStruct((t - n_ctx_tiles * tm, d), F32))
        out_specs = (pl.BlockSpec((tm, d), lambda i: (jnp.minimum(i, n_ctx_tiles - 1), 0)),
                     pl.BlockSpec((tm, d), lambda i: (jnp.maximum(i - n_ctx_tiles, 0), 0)))
        aliases = {}
    else:
        out_shape = jax.ShapeDtypeStruct((t, d), F32)
        out_specs = pl.BlockSpec((tm, d), lambda i: (i, 0))
        aliases = {2: 0}
    n_tiles = t // tm
    return pl.pallas_call(
        functools.partial(_combine_kernel, n_ctx_tiles=n_ctx_tiles, final_norm=final_norm),
        out_shape=out_shape,
        grid=(n_tiles,),
        in_specs=[pl.BlockSpec((TOP_K * tm,), lambda i: (i,), memory_space=pltpu.SMEM),
                  pl.BlockSpec((TOP_K * tm,), lambda i: (jnp.minimum(i + 1, n_tiles - 1),),
                               memory_space=pltpu.SMEM),
                  pl.BlockSpec((tm, d), lambda i: (i, 0)),
                  pl.BlockSpec((tm, TOP_K), lambda i: (i, 0)),
                  pl.BlockSpec((None, 6, d), lambda i: (cond_of(i), 0, 0)),
                  pl.BlockSpec((1, d), lambda i: (0, 0)),
                  pl.BlockSpec(memory_space=pl.ANY)],
        out_specs=out_specs,
        scratch_shapes=[pltpu.VMEM((2, TOP_K, tm, d), F32), pltpu.SemaphoreType.DMA((2,))],
        input_output_aliases=aliases,
        compiler_params=_params(("arbitrary",), 40),
        name="moe_combine",
    )(dest.reshape(-1), dest.reshape(-1), x, gates, mod, g_final.reshape(1, d), ys)


def _moe(x, mod, g, router_w, router_bias, w1, w3, w2, layer, g_final, *, n_ctx_tiles,
         final_norm):
    t, d = x.shape
    ne = router_w.shape[1]
    h, idx, gates, rank, cnt = _router(x, mod, g, router_w, router_bias, n_ctx_tiles=n_ctx_tiles)
    counts = cnt[:, 0].astype(jnp.int32)
    padded = (counts + MOE_ROWS - 1) // MOE_ROWS * MOE_ROWS
    pad_end = jnp.cumsum(padded)
    pad_start = pad_end - padded
    e_ids = jnp.arange(ne, dtype=jnp.int32)
    start_of = jnp.sum(jnp.where(idx[None] == e_ids[:, None, None],
                                 pad_start[:, None, None], 0), axis=0)
    dest = (start_of + rank).T
    n_blocks = -(-(t * TOP_K) // MOE_ROWS) + ne
    block_start = jnp.arange(n_blocks, dtype=jnp.int32) * MOE_ROWS
    block_e = jnp.minimum(jnp.sum(block_start[:, None] >= pad_end[None, :], axis=1),
                          ne - 1).astype(jnp.int32)
    n_used = (pad_end[-1:] // MOE_ROWS).astype(jnp.int32)
    xs = _dispatch(h, dest, pad_start + counts, padded - counts, n_used, n_blocks * MOE_ROWS)
    ys = _experts(xs, block_e, pad_end, n_used, w1, w3, w2, layer)
    return _combine(x, ys, dest, gates.T, mod, g_final, n_ctx_tiles=n_ctx_tiles,
                    final_norm=final_norm)


def kernel(x_prompt, x_sample, state_rglru, c, c_ctx, ada_w, ada_b, norm_mix, norm_moe, norm_final, hy_w_in, hy_b_in, hy_conv_w, hy_conv_b, hy_f_w1, hy_f_b1, hy_f_freq, hy_f_w2, hy_f_b2, hy_f_w3, hy_f_skip, hy_w_out, hy_b_out, rg_w_in, rg_b_in, rg_conv_w, rg_conv_b, rg_wa, rg_ba, rg_wx, rg_bx, rg_lambda, rg_w_out, rg_b_out, router_w, router_bias, moe_w1, moe_w3, moe_w2):
    nb_ctx, len_ctx, d = x_prompt.shape
    nb_lat, len_lat, _ = x_sample.shape
    depth = ada_w.shape[0]
    n_rg = rg_w_in.shape[0]
    d_rnn = rg_w_out.shape[1]
    tok_ctx = nb_ctx * len_ctx
    tiles_ctx = tok_ctx // TOKEN_TILE
    assert tok_ctx % TOKEN_TILE == 0 and TOKEN_TILE % len_ctx == 0 and TOKEN_TILE % GRID_W == 0
    assert len_lat % TOKEN_TILE == 0 and tok_ctx % len_lat == 0
    assert V7X_SUBLANES - (nb_lat + 1) >= 0
    lat_off = tok_ctx // len_lat

    x = jnp.concatenate([x_prompt.reshape(tok_ctx, d), x_sample.reshape(nb_lat * len_lat, d)], 0)
    cond = jnp.concatenate([c_ctx[None, :], c,
                            jnp.zeros((V7X_SUBLANES - 1 - nb_lat, d), F32)], axis=0)
    mods = _ada_modulation(cond, ada_w, ada_b)[:, :1 + nb_lat]
    ctx_h0 = jnp.zeros((nb_ctx, 2, d_rnn), F32)
    states = []

    for i in range(depth):
        mod = mods[i]
        j = i // 2
        if i % 2 == 0:
            u = _inproj(x, mod, norm_mix[i], hy_w_in[j].astype(BF16), hy_b_in[j], hy_conv_w[j],
                        hy_conv_b[j], n_ctx_tiles=tiles_ctx, ctx_len=len_ctx, lat_len=GRID_W,
                        plain_cols=0)
            filt_w = (hy_f_w1[j], hy_f_b1[j], hy_f_freq[j], hy_f_w2[j], hy_f_b2[j], hy_f_w3[j])
            y_ctx = _hyena_stream(u, 0, nb_ctx, len_ctx, filt_w, hy_f_skip[j], d)
            y_lat = _hyena_stream(u, lat_off, nb_lat, len_lat, filt_w, hy_f_skip[j], d)
            x = _outproj(y_ctx, y_lat, hy_w_out[j].astype(BF16), hy_b_out[j], mod, x,
                         n_ctx_tiles=tiles_ctx)
        else:
            u = _inproj(x, mod, norm_mix[i], rg_w_in[j].astype(BF16), rg_b_in[j], rg_conv_w[j],
                        rg_conv_b[j], n_ctx_tiles=tiles_ctx, ctx_len=len_ctx, lat_len=GRID_W,
                        plain_cols=d_rnn)
            wa, wx = rg_wa[j].astype(BF16), rg_wx[j].astype(BF16)
            y_ctx, st = _rglru_stream(u, ctx_h0, 0, nb_ctx, len_ctx, wa, wx, rg_ba[j], rg_bx[j],
                                      rg_lambda[j])
            y_lat, _ = _rglru_stream(u, state_rglru[:, j], lat_off, nb_lat, len_lat, wa, wx,
                                     rg_ba[j], rg_bx[j], rg_lambda[j])
            states.append(st)
            x = _outproj(y_ctx, y_lat, rg_w_out[j].astype(BF16), rg_b_out[j], mod, x,
                         n_ctx_tiles=tiles_ctx)
        x = _moe(x, mod, norm_moe[i], router_w, router_bias, moe_w1, moe_w3, moe_w2, i,
                 norm_final, n_ctx_tiles=tiles_ctx, final_norm=(i == depth - 1))

    y_prompt = x[0].reshape(nb_ctx, len_ctx, d)
    y_sample = x[1].reshape(nb_lat, len_lat, d)
    new_state = jnp.stack(states, axis=1).astype(x_prompt.dtype)
    return (y_prompt, y_sample, new_state)
```

```python
import functools
import math

import numpy as np
import jax
import jax.numpy as jnp
from jax import lax
from jax.experimental import pallas as pl
from jax.experimental.pallas import tpu as pltpu

F32 = jnp.float32
BF16 = jnp.bfloat16

GRID_W = 64
FILTER_BANDS = 16
FILTER_EPS = 1e-6
MIN_DECAY = math.log(1e-2) / 0.3
MAX_DECAY = math.log(1e-2) / 1.5
RG_C = 8.0
N_GROUPS = 4
TOP_K = 2
NORM_EPS = 1e-6

V7X_LANES = 128
V7X_SUBLANES = 8
V7X_VMEM_BYTES = 64 * 1024 * 1024

TOKEN_TILE = 512
COL_TILE = 1024
MOE_ROWS = 256
DFT_ROWS = 512
DFT_SPLIT = 64
RG_COLS = 256
RG_ROWS = 1024
ROW_DMA_UNROLL = 8


def _params(sem, vmem_mb):
    return pltpu.CompilerParams(dimension_semantics=sem,
                                vmem_limit_bytes=vmem_mb * 1024 * 1024)


def _dot(a, b):
    return jnp.dot(a, b, preferred_element_type=F32)


def _split(x):
    hi = x.astype(BF16)
    lo = (x - hi.astype(F32)).astype(BF16)
    return hi, lo


def _dot3(a, b):
    ah, al = _split(a)
    bh, bl = _split(b)
    return _dot(ah, bh) + (_dot(ah, bl) + _dot(al, bh))


def _dot3_nt(a, b):
    dn = (((1,), (1,)), ((), ()))
    d = lambda x, y: lax.dot_general(x, y, dn, preferred_element_type=F32)
    ah, al = _split(a)
    bh, bl = _split(b)
    return d(ah, bh) + (d(ah, bl) + d(al, bh))


def _sigmoid(x):
    return 1.0 / (1.0 + jnp.exp(-x))


def _sigmoid_tanh(x):
    return 0.5 * jnp.tanh(0.5 * x) + 0.5


def _norm_mod(x, g, shift, scale):
    ms = jnp.mean(x * x, axis=-1, keepdims=True)
    return (x * lax.rsqrt(ms + NORM_EPS) * g) * (1.0 + scale) + shift


def _ada_kernel(c_ref, w_ref, b_ref, o_ref):
    c = c_ref[...]
    o_ref[...] = _dot3(c * _sigmoid(c), w_ref[...]) + b_ref[...]


def _ada_modulation(cond, ada_w, ada_b):
    depth, d, n = ada_w.shape
    tn = n // 4
    out = pl.pallas_call(
        _ada_kernel,
        out_shape=jax.ShapeDtypeStruct((depth, cond.shape[0], n), F32),
        grid=(depth, n // tn),
        in_specs=[pl.BlockSpec(cond.shape, lambda i, j: (0, 0)),
                  pl.BlockSpec((None, d, tn), lambda i, j: (i, 0, j)),
                  pl.BlockSpec((None, 1, tn), lambda i, j: (i, 0, j))],
        out_specs=pl.BlockSpec((None, cond.shape[0], tn), lambda i, j: (i, 0, j)),
        compiler_params=_params(("parallel", "parallel"), 40),
        name="ada_modulation",
    )(cond, ada_w, ada_b.reshape(depth, 1, n))
    return out.reshape(depth, cond.shape[0], 6, d)


def _inproj_kernel(x_ref, mod_ref, g_ref, w_ref, b_ref, cw_ref, cb_ref, mk_ref, o_ref, h_scr,
                   u_scr, *, sub, width, plain_col_tiles, col_tiles):
    s = pl.program_id(0)
    last = pl.num_programs(0) - 1
    j_cur = lax.rem(jnp.minimum(s, last - 1), col_tiles)
    j_prev = lax.rem(jnp.maximum(s - 1, 0), col_tiles)

    @pl.when(s == 0)
    def _():
        u_scr[...] = jnp.zeros_like(u_scr)

    @pl.when((j_cur == 0) & (s < last))
    def _():
        h = _norm_mod(x_ref[...], g_ref[...], mod_ref[0:1, :], mod_ref[1:2, :])
        h_scr[...] = h.astype(BF16)

    blocks = [slice(r, r + sub) for r in range(0, x_ref.shape[0], sub)]
    pad_l = (width - 1) // 2

    def step(with_conv):
        new = [_dot(h_scr[rows, :], w_ref[...]) + b_ref[...] for rows in blocks]
        for rows in blocks:
            u = u_scr[rows, :]
            if with_conv:
                acc = cw_ref[pad_l:pad_l + 1, :] * u + cb_ref[...]
                m = 0
                for k in range(width):
                    off = k - pad_l
                    if off != 0:
                        shifted = pltpu.roll(u, (-off) % sub, axis=0)
                        acc = acc + cw_ref[k:k + 1, :] * (shifted * mk_ref[m])
                        m += 1
                u = acc
            o_ref[rows, :] = u
        for rows, u in zip(blocks, new):
            u_scr[rows, :] = u

    if plain_col_tiles == 0:
        step(True)
    else:
        pl.when(j_prev >= plain_col_tiles)(lambda: step(True))
        pl.when(j_prev < plain_col_tiles)(lambda: step(False))


def _inproj(x, mod, g, w, b, conv_w, conv_b, *, n_ctx_tiles, ctx_len, lat_len, plain_cols):
    t, d = x.shape
    n = w.shape[1]
    tm, tn = TOKEN_TILE, COL_TILE
    width = conv_w.shape[0]
    plain_tiles = plain_cols // tn
    lat_tiles = (t // tm - n_ctx_tiles) // (mod.shape[0] - 1)

    def cond_of(i):
        return jnp.where(i < n_ctx_tiles, 0, 1 + (i - n_ctx_tiles) // lat_tiles)

    sub = max(ctx_len, lat_len)
    assert sub % ctx_len == 0 and sub % lat_len == 0 and tm % sub == 0
    pad_l = (width - 1) // 2
    offs = jnp.asarray([k - pad_l for k in range(width) if k != pad_l], jnp.int32)
    seg = jnp.asarray([ctx_len, lat_len], jnp.int32)[:, None, None]
    pos = jnp.arange(sub, dtype=jnp.int32)[None, None, :] % seg + offs[None, :, None]
    masks = jnp.broadcast_to(((pos >= 0) & (pos < seg)).astype(F32)[..., None],
                             (2, width - 1, sub, tn))

    nj = n // tn
    n_tiles = (t // tm) * nj

    def cur(s):
        s = jnp.minimum(s, n_tiles - 1)
        return lax.div(s, nj), lax.rem(s, nj)

    def prev(s):
        s = jnp.maximum(s - 1, 0)
        return lax.div(s, nj), lax.rem(s, nj)

    conv_col = lambda s: jnp.maximum(prev(s)[1] - plain_tiles, 0)
    kern = functools.partial(_inproj_kernel, sub=sub, width=width, plain_col_tiles=plain_tiles,
                             col_tiles=nj)
    return pl.pallas_call(
        kern,
        out_shape=jax.ShapeDtypeStruct((t, n), F32),
        grid=(n_tiles + 1,),
        in_specs=[pl.BlockSpec((tm, d), lambda s: (cur(s)[0], 0)),
                  pl.BlockSpec((None, 6, d), lambda s: (cond_of(cur(s)[0]), 0, 0)),
                  pl.BlockSpec((1, d), lambda s: (0, 0)),
                  pl.BlockSpec((d, tn), lambda s: (0, cur(s)[1])),
                  pl.BlockSpec((1, tn), lambda s: (0, cur(s)[1])),
                  pl.BlockSpec((width, tn), lambda s: (0, conv_col(s))),
                  pl.BlockSpec((1, tn), lambda s: (0, conv_col(s))),
                  pl.BlockSpec((None, width - 1, sub, tn),
                               lambda s: (jnp.where(prev(s)[0] < n_ctx_tiles, 0, 1), 0, 0, 0))],
        out_specs=pl.BlockSpec((tm, tn), prev),
        scratch_shapes=[pltpu.VMEM((tm, d), BF16), pltpu.VMEM((tm, tn), F32)],
        compiler_params=_params(("arbitrary",), 48),
        name="norm_inproj_conv",
    )(x, mod, g.reshape(1, d), w, b.reshape(1, n), conv_w, conv_b.reshape(1, -1), masks)


def _outproj_kernel(yc_ref, yl_ref, w_ref, b_ref, mod_ref, x_ref, o_ref, *, n_ctx_tiles):
    i = pl.program_id(0)

    def run(y_ref):
        m = _dot(y_ref[...], w_ref[...]) + b_ref[...]
        o_ref[...] = x_ref[...] + mod_ref[2:3, :] * m

    pl.when(i < n_ctx_tiles)(lambda: run(yc_ref))
    pl.when(i >= n_ctx_tiles)(lambda: run(yl_ref))


def _outproj(y_ctx, y_lat, w, b, mod, x, *, n_ctx_tiles):
    t, d = x.shape
    tm = TOKEN_TILE
    lat_tiles = (t // tm - n_ctx_tiles) // (mod.shape[0] - 1)

    def cond_of(i):
        return jnp.where(i < n_ctx_tiles, 0, 1 + (i - n_ctx_tiles) // lat_tiles)

    return pl.pallas_call(
        functools.partial(_outproj_kernel, n_ctx_tiles=n_ctx_tiles),
        out_shape=jax.ShapeDtypeStruct((t, d), F32),
        grid=(t // tm,),
        in_specs=[pl.BlockSpec((tm, w.shape[0]), lambda i: (jnp.minimum(i, n_ctx_tiles - 1), 0)),
                  pl.BlockSpec((tm, w.shape[0]), lambda i: (jnp.maximum(i - n_ctx_tiles, 0), 0)),
                  pl.BlockSpec(w.shape, lambda i: (0, 0)),
                  pl.BlockSpec((1, d), lambda i: (0, 0)),
                  pl.BlockSpec((None, 6, d), lambda i: (cond_of(i), 0, 0)),
                  pl.BlockSpec((tm, d), lambda i: (i, 0))],
        out_specs=pl.BlockSpec((tm, d), lambda i: (i, 0)),
        input_output_aliases={5: 0},
        compiler_params=_params(("parallel",), 32),
        name="outproj_residual",
    )(y_ctx, y_lat, w, b.reshape(1, d), mod, x)


def _filter_kernel(z_ref, w1_ref, b1_ref, fr_ref, w2_ref, b2_ref, w3a_ref, w3b_ref, w3c_ref,
                   w3d_ref, t_ref, dl_ref, o_ref, h_scr, *, orders):
    @pl.when(pl.program_id(0) == 0)
    def _():
        fr = fr_ref[...]
        h1 = jnp.sin(fr * (_dot3(z_ref[...], w1_ref[...]) + b1_ref[...]))
        h_scr[...] = jnp.sin(fr * (_dot3(h1, w2_ref[...]) + b2_ref[...]))

    h = h_scr[...]
    window = jnp.exp(-t_ref[...] * dl_ref[...])
    row = lax.broadcasted_iota(jnp.int32, (h.shape[0], 1), 0)
    w3 = ((w3a_ref, w3b_ref), (w3c_ref, w3d_ref))
    for o in range(orders):
        h_fwd = _dot3(h, w3[0][o][...]) * window
        h_bwd = jnp.where(row == 0, 0.0, _dot3(h, w3[1][o][...]) * window)
        norm = (jnp.sum(jnp.abs(h_fwd), axis=0, keepdims=True)
                + jnp.sum(jnp.abs(h_bwd), axis=0, keepdims=True) + FILTER_EPS)
        o_ref[2 * o] = h_fwd / norm
        o_ref[2 * o + 1] = h_bwd / norm


def _hyena_filter(length, f_w1, f_b1, f_freq, f_w2, f_b2, f_w3, d):
    emb, hid = f_w1.shape
    orders = f_w3.shape[1] // (2 * d)
    t = jnp.linspace(0.0, 1.0, length, dtype=F32)[:, None]
    w = (2.0 * math.pi / length) * jnp.arange(length, dtype=F32)[:, None]
    bands = jnp.linspace(1e-4, FILTER_BANDS - 1, FILTER_BANDS, dtype=F32)[None, :]
    z = jnp.concatenate([t, jnp.cos(bands * w), -jnp.sin(bands * w)], axis=-1)
    emb_pad = V7X_LANES
    z = jnp.pad(z, ((0, 0), (0, emb_pad - emb)))
    w1 = jnp.pad(f_w1, ((0, emb_pad - emb), (0, 0)))
    deltas = jnp.abs(jnp.linspace(MIN_DECAY, MAX_DECAY, d, dtype=F32))[None, :]
    td = 256
    nd = d // td
    full = lambda shape: pl.BlockSpec(shape, lambda j: (0,) * len(shape))
    w3_spec = lambda side, o: pl.BlockSpec((hid, td), lambda j: (0, (side * orders + o) * nd + j))
    assert orders == 2
    return pl.pallas_call(
        functools.partial(_filter_kernel, orders=orders),
        out_shape=jax.ShapeDtypeStruct((2 * orders, length, d), F32),
        grid=(nd,),
        in_specs=[full((length, emb_pad)), full((emb_pad, hid)), full((1, hid)), full((1, hid)),
                  full((hid, hid)), full((1, hid)),
                  w3_spec(0, 0), w3_spec(0, 1), w3_spec(1, 0), w3_spec(1, 1),
                  full((length, 1)), pl.BlockSpec((1, td), lambda j: (0, j))],
        out_specs=pl.BlockSpec((2 * orders, length, td), lambda j: (0, 0, j)),
        scratch_shapes=[pltpu.VMEM((length, hid), F32)],
        compiler_params=_params(("arbitrary",), 48),
        name="hyena_filter",
    )(z, w1, f_b1.reshape(1, hid), f_freq.reshape(1, hid), f_w2, f_b2.reshape(1, hid),
      f_w3, f_w3, f_w3, f_w3, t, deltas)


def _dft_matrices(length):
    n = 2 * length
    f0n, f1n = DFT_SPLIT, length // DFT_SPLIT
    tt = np.arange(length, dtype=np.int64)[None, :]
    ang_a = 2.0 * np.pi * ((DFT_SPLIT * np.arange(f1n, dtype=np.int64)[:, None] * tt) % n) / n
    ang_b = 2.0 * np.pi * ((np.arange(f0n, dtype=np.int64)[:, None] * tt) % n) / n
    ca, sa = (jnp.asarray(f(ang_a), F32)[:, None, :] for f in (np.cos, np.sin))
    cb, sb = (jnp.asarray(f(ang_b), F32)[None, :, :] for f in (np.cos, np.sin))
    cos_m = (ca * cb - sa * sb).reshape(length, length)
    sin_m = (sa * cb + ca * sb).reshape(length, length)
    nyq = jnp.asarray(1.0 - 2.0 * (np.arange(length) % 2), F32)[None, :]
    f_is0 = (jnp.arange(length) == 0)[:, None]
    fwd = jnp.stack([cos_m, jnp.where(f_is0, nyq, -sin_m)], axis=0)
    cat, sat = (jnp.asarray(f(ang_a).T, F32)[:, :, None] for f in (np.cos, np.sin))
    cbt, sbt = (jnp.asarray(f(ang_b).T, F32)[:, None, :] for f in (np.cos, np.sin))
    f_is0_t = f_is0.T
    scale = jnp.where(f_is0_t, 1.0 / n, 2.0 / n)
    cos_t = (cat * cbt - sat * sbt).reshape(length, length) * scale
    im_t = jnp.where(f_is0_t, nyq.T, -(sat * cbt + cat * sbt).reshape(length, length)) * scale
    inv = jnp.concatenate([cos_t, im_t], axis=1)
    return fwd.astype(BF16), inv.astype(BF16)


def _spec_kernel(a_ref, hf_ref, hb_ref, o_ref, sum_scr, dif_scr, nyq_scr):
    m = pl.program_id(2)

    @pl.when(m == 0)
    def _():
        hf, hb = hf_ref[...], hb_ref[...]
        sum_scr[...] = (hf + hb).astype(BF16)
        dif_scr[...] = (hf - hb).astype(BF16)
        t = lax.broadcasted_iota(jnp.int32, (hb.shape[0], 1), 0)
        sign = (1 - 2 * (t & 1)).astype(F32)
        nyq_scr[...] = 2.0 * jnp.sum(sign * hb, axis=0, keepdims=True)

    o_ref[0] = _dot(a_ref[0], sum_scr[...])
    im = _dot(a_ref[1], dif_scr[...])
    row = lax.broadcasted_iota(jnp.int32, (im.shape[0], 1), 0)
    o_ref[1] = jnp.where((row == 0) & (m == 0), im + nyq_scr[...], im)


def _filter_spectrum(fwd, filt):
    n2, length, d = filt.shape
    orders = n2 // 2
    th, tc = DFT_ROWS // 2, 512
    filt2 = filt.reshape(n2 * length, d)
    return pl.pallas_call(
        _spec_kernel,
        out_shape=jax.ShapeDtypeStruct((orders, 2, length, d), F32),
        grid=(orders, d // tc, length // th),
        in_specs=[pl.BlockSpec((2, th, length), lambda o, c, m: (0, m, 0)),
                  pl.BlockSpec((length, tc), lambda o, c, m: (2 * o, c)),
                  pl.BlockSpec((length, tc), lambda o, c, m: (2 * o + 1, c))],
        out_specs=pl.BlockSpec((None, 2, th, tc), lambda o, c, m: (o, 0, m, c)),
        scratch_shapes=[pltpu.VMEM((length, tc), BF16), pltpu.VMEM((length, tc), BF16),
                        pltpu.VMEM((1, tc), F32)],
        compiler_params=_params(("parallel", "parallel", "arbitrary"), 48),
        name="hyena_filter_spectrum",
    )(fwd, filt2, filt2)


def _dft_fwd_kernel(a_ref, z_ref, k_ref, o_ref, z_scr):
    m = pl.program_id(1)

    @pl.when(m == 0)
    def _():
        z_scr[...] = z_ref[...].astype(BF16)

    xr = _dot(a_ref[0], z_scr[...])
    xi = _dot(a_ref[1], z_scr[...])
    kr, ki = k_ref[0], k_ref[1]
    row = lax.broadcasted_iota(jnp.int32, (xr.shape[0], 1), 0)
    dc = (row == 0) & (m == 0)
    o_ref[0] = (xr * kr - jnp.where(dc, 0.0, xi * ki)).astype(o_ref.dtype)
    o_ref[1] = jnp.where(dc, xi * ki, xr * ki + xi * kr).astype(o_ref.dtype)


def _dft_forward(fwd, src, spec, *, nb, length, row_off, col_blk, order):
    d = spec.shape[3]
    th = DFT_ROWS // 2
    return pl.pallas_call(
        _dft_fwd_kernel,
        out_shape=jax.ShapeDtypeStruct((nb, 2, length, d), BF16),
        grid=(nb, length // th),
        in_specs=[pl.BlockSpec((2, th, length), lambda b, m: (0, m, 0)),
                  pl.BlockSpec((length, d), lambda b, m: (row_off + b, col_blk)),
                  pl.BlockSpec((None, 2, th, d), lambda b, m: (order, 0, m, 0))],
        out_specs=pl.BlockSpec((None, 2, th, d), lambda b, m: (b, 0, m, 0)),
        scratch_shapes=[pltpu.VMEM((length, d), BF16)],
        compiler_params=_params(("parallel", "arbitrary"), 48),
        name="hyena_dft_forward",
    )(fwd, src, spec)


def _dft_inv_kernel(a_ref, p_ref, v_ref, g_ref, s_ref, o_ref):
    y = _dot(a_ref[...], p_ref[...])
    v = v_ref[...]
    o_ref[...] = (g_ref[...] * (y + v * s_ref[...])).astype(o_ref.dtype)


def _dft_inverse(inv, prod, vsrc, gsrc, skip, *, nb, length, v_off, v_col, g_off, g_col,
                 out_dtype):
    n = 2 * length
    d = prod.shape[3]
    prod = prod.reshape(nb * n, d)
    tm = min(DFT_ROWS, length)
    nt = length // tm
    return pl.pallas_call(
        _dft_inv_kernel,
        out_shape=jax.ShapeDtypeStruct((nb * length, d), out_dtype),
        grid=(nb, nt),
        in_specs=[pl.BlockSpec((tm, n), lambda b, m: (m, 0)),
                  pl.BlockSpec((n, d), lambda b, m: (b, 0)),
                  pl.BlockSpec((tm, d), lambda b, m: ((v_off + b) * nt + m, v_col)),
                  pl.BlockSpec((tm, d), lambda b, m: ((g_off + b) * nt + m, g_col)),
                  pl.BlockSpec((1, d), lambda b, m: (0, 0))],
        out_specs=pl.BlockSpec((tm, d), lambda b, m: (b * nt + m, 0)),
        compiler_params=_params(("parallel", "arbitrary"), 48),
        name="hyena_dft_inverse",
    )(inv, prod, vsrc, gsrc, skip.reshape(1, d))


def _spectral_product(acc, k_ref, order):
    th = acc.shape[0] // 2
    xr, xi = acc[:th], acc[th:]
    kr, ki = k_ref[order, 0:th, :], k_ref[order, th:, :]
    dc = lax.broadcasted_iota(jnp.int32, (th, 1), 0) == 0
    pr = xr * kr - jnp.where(dc, 0.0, xi * ki)
    pi = jnp.where(dc, xi * ki, xr * ki + xi * kr)
    return jnp.concatenate([pr, pi], axis=0).astype(BF16)


def _hyena_short_kernel(u_ref, f_ref, i_ref, k_ref, s_ref, o_ref):
    d = o_ref.shape[1]
    fwd, inv = f_ref[...], i_ref[...]
    z = u_ref[:, 0:d]
    for order in range(2):
        p = _spectral_product(_dot(fwd, z.astype(BF16)), k_ref, order)
        gate = u_ref[:, (order + 1) * d:(order + 2) * d]
        z = gate * (_dot(inv, p) + z * s_ref[order:order + 1, :])
    o_ref[...] = z.astype(o_ref.dtype)


def _hyena_short(u, fwd, inv, spec, f_skip, *, nb, length):
    n = 2 * length
    d = spec.shape[3]
    assert n == DFT_ROWS and spec.shape[:3] == (2, 2, length)
    fwd = fwd.reshape(n, length)
    return pl.pallas_call(
        _hyena_short_kernel,
        out_shape=jax.ShapeDtypeStruct((nb * length, d), BF16),
        grid=(nb,),
        in_specs=[pl.BlockSpec((length, 3 * d), lambda b: (b, 0)),
                  pl.BlockSpec((n, length), lambda b: (0, 0)),
                  pl.BlockSpec((length, n), lambda b: (0, 0)),
                  pl.BlockSpec((2, n, d), lambda b: (0, 0, 0)),
                  pl.BlockSpec((2, d), lambda b: (0, 0))],
        out_specs=pl.BlockSpec((length, d), lambda b: (b, 0)),
        compiler_params=_params(("parallel",), 48),
        name="hyena_short_sequences",
    )(u, fwd, inv, spec.reshape(2, n, d), f_skip)


def _hyena_stream(u, row_off, nb, length, filt_w, f_skip, d):
    fwd, inv = _dft_matrices(length)
    filt = _hyena_filter(length, *filt_w, d)
    spec = _filter_spectrum(fwd, filt)
    if 2 * length == DFT_ROWS:
        assert row_off == 0
        return _hyena_short(u, fwd, inv, spec, f_skip, nb=nb, length=length)
    p1 = _dft_forward(fwd, u, spec, nb=nb, length=length, row_off=row_off, col_blk=0, order=0)
    z1 = _dft_inverse(inv, p1, u, u, f_skip[0], nb=nb, length=length, v_off=row_off, v_col=0,
                      g_off=row_off, g_col=1, out_dtype=F32)
    p2 = _dft_forward(fwd, z1, spec, nb=nb, length=length, row_off=0, col_blk=0, order=1)
    return _dft_inverse(inv, p2, z1, u, f_skip[1], nb=nb, length=length, v_off=0, v_col=0,
                        g_off=row_off, g_col=2, out_dtype=BF16)


def _gelu_tanh(x):
    return 0.5 * x * (1.0 + jnp.tanh(math.sqrt(2.0 / math.pi) * (x + 0.044715 * (x * x * x))))


def _log1p(e):
    u = 1.0 + e
    d = u - 1.0
    return jnp.where(d == 0.0, e, jnp.log(u) * (e / jnp.where(d == 0.0, 1.0, d)))


def _rglru_kernel(gate_ref, rec_ref, wa_ref, wx_ref, ba_ref, bx_ref, lam_ref, h0_ref,
                  y_ref, st_ref, a_scr, b_scr):
    length, cols = rec_ref.shape
    groups = length // V7X_SUBLANES
    rec = rec_ref[...]
    rec16 = rec.astype(BF16)
    pos = lax.broadcasted_iota(jnp.int32, (groups, V7X_SUBLANES, cols), 1)

    for d in range(2):
        r_gate = _sigmoid_tanh(_dot(rec16, wa_ref[d]) + ba_ref[d])
        i_gate = _sigmoid_tanh(_dot(rec16, wx_ref[d]) + bx_ref[d])
        nlam = -lam_ref[d]
        softplus = jnp.maximum(nlam, 0.0) + _log1p(jnp.exp(-jnp.abs(nlam)))
        log_a = (-RG_C * softplus) * r_gate
        a = jnp.exp(log_a).reshape(groups, V7X_SUBLANES, cols)
        th = jnp.tanh(log_a)
        b = (jnp.sqrt(-2.0 * th / (1.0 - th)) * (i_gate * rec)).reshape(
            groups, V7X_SUBLANES, cols)
        for s in (1, 2, 4):
            if d == 0:
                a_sh = pltpu.roll(a, s, axis=1)
                b_sh = pltpu.roll(b, s, axis=1)
                live = pos >= s
            else:
                a_sh = pltpu.roll(a, V7X_SUBLANES - s, axis=1)
                b_sh = pltpu.roll(b, V7X_SUBLANES - s, axis=1)
                live = pos < V7X_SUBLANES - s
            b = jnp.where(live, a * b_sh, 0.0) + b
            a = jnp.where(live, a * a_sh, a)
        a_scr[d] = a.reshape(length, cols)
        b_scr[d] = b.reshape(length, cols)

    seqs = h0_ref.shape[0]
    seq_groups = groups // seqs

    def step(g, carry):
        out = []
        for q in range(seqs):
            cf, cb = carry[2 * q], carry[2 * q + 1]
            rf = pl.multiple_of((q * seq_groups + g) * V7X_SUBLANES, V7X_SUBLANES)
            rb = pl.multiple_of(((q + 1) * seq_groups - 1 - g) * V7X_SUBLANES, V7X_SUBLANES)
            hf = a_scr[0, pl.ds(rf, V7X_SUBLANES), :] * cf + b_scr[0, pl.ds(rf, V7X_SUBLANES), :]
            hb = a_scr[1, pl.ds(rb, V7X_SUBLANES), :] * cb + b_scr[1, pl.ds(rb, V7X_SUBLANES), :]
            b_scr[0, pl.ds(rf, V7X_SUBLANES), :] = hf
            b_scr[1, pl.ds(rb, V7X_SUBLANES), :] = hb
            out.append(jnp.broadcast_to(hf[V7X_SUBLANES - 1:V7X_SUBLANES, :], hf.shape))
            out.append(jnp.broadcast_to(hb[0:1, :], hb.shape))
        return tuple(out)

    init = tuple(jnp.broadcast_to(h0_ref[q, d], (V7X_SUBLANES, cols))
                 for q in range(seqs) for d in range(2))
    final = lax.fori_loop(0, seq_groups, step, init, unroll=max(1, 4 // seqs))
    for q in range(seqs):
        for d in range(2):
            st_ref[q, d] = final[2 * q + d][0:1, :]
    y_ref[...] = ((b_scr[0] + b_scr[1]) * _gelu_tanh(gate_ref[...])).astype(y_ref.dtype)


def _rglru_stream(u, h0, row_off, nb, length, wa, wx, ba, bx, lam):
    dr = u.shape[1] // 2
    tc = RG_COLS
    nc = dr // tc
    seqs = max(1, min(nb, RG_ROWS // length))
    assert nb % seqs == 0 and (row_off * length) % (seqs * length) == 0
    rows = seqs * length
    blk_off = row_off // seqs
    vec = lambda a: a.reshape(2, 1, dr)
    vec_spec = pl.BlockSpec((2, 1, tc), lambda b, c: (0, 0, c))
    w_spec = pl.BlockSpec((2, None, tc, tc), lambda b, c: (0, c, 0, 0))
    st_spec = pl.BlockSpec((seqs, 2, 1, tc), lambda b, c: (b, 0, 0, c))
    y, st = pl.pallas_call(
        _rglru_kernel,
        out_shape=(jax.ShapeDtypeStruct((nb * length, dr), BF16),
                   jax.ShapeDtypeStruct((nb, 2, 1, dr), F32)),
        grid=(nb // seqs, nc),
        in_specs=[pl.BlockSpec((rows, tc), lambda b, c: (blk_off + b, c)),
                  pl.BlockSpec((rows, tc), lambda b, c: (blk_off + b, nc + c)),
                  w_spec, w_spec, vec_spec, vec_spec, vec_spec, st_spec],
        out_specs=(pl.BlockSpec((rows, tc), lambda b, c: (b, c)), st_spec),
        scratch_shapes=[pltpu.VMEM((2, rows, tc), F32), pltpu.VMEM((2, rows, tc), F32)],
        compiler_params=_params(("parallel", "parallel"), 48),
        name="rglru_scan",
    )(u, u, wa, wx, vec(ba), vec(bx), vec(lam), h0.reshape(nb, 2, 1, dr))
    return y, st.reshape(nb, 2, dr)


def _router_kernel(x_ref, mod_ref, g_ref, rwt_ref, bias_ref, h_ref, idx_ref, gate_ref,
                   rank_ref, cnt_ref, tri_scr, carry_scr, *, n_experts):
    i = pl.program_id(0)
    tm = x_ref.shape[0]
    per_group = n_experts // N_GROUPS

    @pl.when(i == 0)
    def _():
        r = lax.broadcasted_iota(jnp.int32, (tm, tm), 0)
        c = lax.broadcasted_iota(jnp.int32, (tm, tm), 1)
        tri_scr[...] = jnp.where(r < c, 1.0, 0.0).astype(BF16)
        carry_scr[...] = jnp.zeros_like(carry_scr)

    h = _norm_mod(x_ref[...], g_ref[...], mod_ref[3:4, :], mod_ref[4:5, :])
    h_ref[...] = h
    logits = _dot3_nt(rwt_ref[...], h)
    p = jnp.exp(logits - jnp.max(logits, axis=0, keepdims=True))
    scores = p / jnp.sum(p, axis=0, keepdims=True)
    sel = scores + bias_ref[...]
    rows = [sel[e:e + 1, :] for e in range(n_experts)]

    best_val = None
    for gi in range(N_GROUPS):
        v = rows[gi * per_group:(gi + 1) * per_group]
        pair = None
        for a in range(per_group):
            for b in range(a + 1, per_group):
                s = v[a] + v[b]
                pair = s if pair is None else jnp.maximum(pair, s)
        if best_val is None:
            best_val, best_grp = pair, jnp.zeros_like(pair, dtype=jnp.int32)
        else:
            take = pair > best_val
            best_val = jnp.where(take, pair, best_val)
            best_grp = jnp.where(take, gi, best_grp)

    neg = jnp.float32(-jnp.inf)
    masked = [jnp.where(best_grp == e // per_group, rows[e], neg) for e in range(n_experts)]

    def argmax_first(vals):
        bv, bi = vals[0], jnp.zeros_like(best_grp)
        for e in range(1, n_experts):
            take = vals[e] > bv
            bv = jnp.where(take, vals[e], bv)
            bi = jnp.where(take, e, bi)
        return bi

    idx0 = argmax_first(masked)
    idx1 = argmax_first([jnp.where(idx0 == e, neg, masked[e]) for e in range(n_experts)])

    e_iota = lax.broadcasted_iota(jnp.int32, (n_experts, tm), 0)
    hit0 = e_iota == idx0
    hit1 = e_iota == idx1
    g0 = jnp.sum(jnp.where(hit0, scores, 0.0), axis=0, keepdims=True)
    g1 = jnp.sum(jnp.where(hit1, scores, 0.0), axis=0, keepdims=True)
    gsum = g0 + g1
    onehot = jnp.where(hit0 | hit1, 1.0, 0.0)
    before = _dot(onehot.astype(BF16), tri_scr[...]) + carry_scr[:, 0:1]
    r0 = jnp.sum(jnp.where(hit0, before, 0.0), axis=0, keepdims=True)
    r1 = jnp.sum(jnp.where(hit1, before, 0.0), axis=0, keepdims=True)
    idx_ref[0:1, :] = idx0
    idx_ref[1:2, :] = idx1
    gate_ref[0:1, :] = g0 / gsum
    gate_ref[1:2, :] = g1 / gsum
    rank_ref[0:1, :] = r0.astype(jnp.int32)
    rank_ref[1:2, :] = r1.astype(jnp.int32)
    carry_scr[...] = carry_scr[...] + jnp.sum(onehot, axis=1, keepdims=True)
    cnt_ref[...] = carry_scr[...]


def _router(x, mod, g, router_w, router_bias, *, n_ctx_tiles):
    t, d = x.shape
    ne = router_w.shape[1]
    tm = TOKEN_TILE
    lat_tiles = (t // tm - n_ctx_tiles) // (mod.shape[0] - 1)

    def cond_of(i):
        return jnp.where(i < n_ctx_tiles, 0, 1 + (i - n_ctx_tiles) // lat_tiles)

    row2 = pl.BlockSpec((TOP_K, tm), lambda i: (0, i))
    return pl.pallas_call(
        functools.partial(_router_kernel, n_experts=ne),
        out_shape=(jax.ShapeDtypeStruct((t, d), F32),
                   jax.ShapeDtypeStruct((TOP_K, t), jnp.int32),
                   jax.ShapeDtypeStruct((TOP_K, t), F32),
                   jax.ShapeDtypeStruct((TOP_K, t), jnp.int32),
                   jax.ShapeDtypeStruct((ne, V7X_LANES), F32)),
        grid=(t // tm,),
        in_specs=[pl.BlockSpec((tm, d), lambda i: (i, 0)),
                  pl.BlockSpec((None, 6, d), lambda i: (cond_of(i), 0, 0)),
                  pl.BlockSpec((1, d), lambda i: (0, 0)),
                  pl.BlockSpec((ne, d), lambda i: (0, 0)),
                  pl.BlockSpec((ne, 1), lambda i: (0, 0))],
        out_specs=(pl.BlockSpec((tm, d), lambda i: (i, 0)), row2, row2, row2,
                   pl.BlockSpec((ne, V7X_LANES), lambda i: (0, 0))),
        scratch_shapes=[pltpu.VMEM((tm, tm), BF16), pltpu.VMEM((ne, V7X_LANES), F32)],
        compiler_params=_params(("arbitrary",), 32),
        name="moe_router",
    )(x, mod, g.reshape(1, d), router_w.T, router_bias.reshape(ne, 1))


def _row_copy(src, src_row, dst, dst_row, sem):
    return pltpu.make_async_copy(src.at[pl.ds(src_row, 1)], dst.at[pl.ds(dst_row, 1)], sem)


def _dispatch_kernel(pad_row_ref, pad_len_ref, used_ref, dest_ref, h_ref, xs_out, zero_scr, sem,
                     pad_sem):
    tm = h_ref.shape[0]
    i = pl.program_id(0)
    n_experts = pad_row_ref.shape[0]
    zrows = zero_scr.shape[0]

    @pl.when(i == 0)
    def _():
        zero_scr[...] = jnp.zeros_like(zero_scr)

    def zero_copy(wanted, start, size):
        start = jnp.where(wanted, start, 0)
        if size >= V7X_SUBLANES:
            start = pl.multiple_of(start, V7X_SUBLANES)
        return wanted, pltpu.make_async_copy(zero_scr.at[pl.ds(0, size)],
                                             xs_out.at[pl.ds(start, size)], pad_sem)

    e = jnp.minimum(i, n_experts - 1)
    pad_len = jnp.where(i < n_experts, pad_len_ref[e], 0)
    pad_row = pad_row_ref[e]
    head = pad_len & (V7X_SUBLANES - 1)
    pad_copies = [zero_copy(r < head, pad_row + r, 1) for r in range(V7X_SUBLANES - 1)]
    size = zrows
    while size >= V7X_SUBLANES:
        start = pad_row + head + ((pad_len - head) & ~(2 * size - 1))
        pad_copies.append(zero_copy((pad_len & size) != 0, start, size))
        size //= 2
    spare = used_ref[0] + i
    has_spare = (i < n_experts) & (spare < xs_out.shape[0] // MOE_ROWS)
    for part in range(MOE_ROWS // zrows):
        pad_copies.append(zero_copy(has_spare, spare * MOE_ROWS + part * zrows, zrows))
    for wanted, copy in pad_copies:
        pl.when(wanted)(copy.start)

    def issue(g, c):
        for r in range(ROW_DMA_UNROLL):
            t = g * ROW_DMA_UNROLL + r
            for k in range(TOP_K):
                _row_copy(h_ref, t, xs_out, dest_ref[TOP_K * t + k], sem).start(priority=k)
        return c

    def drain(g, c):
        for _ in range(ROW_DMA_UNROLL * TOP_K):
            _row_copy(h_ref, 0, xs_out, 0, sem).wait()
        return c

    lax.fori_loop(0, tm // ROW_DMA_UNROLL, issue, 0)
    lax.fori_loop(0, tm // ROW_DMA_UNROLL, drain, 0)
    for wanted, copy in pad_copies:
        pl.when(wanted)(copy.wait)


def _dispatch(h, dest, pad_row, pad_len, n_used, n_slots):
    t, d = h.shape
    tm = TOKEN_TILE
    assert t // tm >= pad_row.shape[0] and n_slots % MOE_ROWS == 0
    return pl.pallas_call(
        _dispatch_kernel,
        out_shape=jax.ShapeDtypeStruct((n_slots, d), h.dtype),
        grid_spec=pltpu.PrefetchScalarGridSpec(
            num_scalar_prefetch=3,
            grid=(t // tm,),
            in_specs=[pl.BlockSpec((TOP_K * tm,), lambda i, pr, pn, nu: (i,),
                                   memory_space=pltpu.SMEM),
                      pl.BlockSpec((tm, d), lambda i, pr, pn, nu: (i, 0))],
            out_specs=pl.BlockSpec(memory_space=pl.ANY),
            scratch_shapes=[pltpu.VMEM((MOE_ROWS // 2, d), h.dtype),
                            pltpu.SemaphoreType.DMA, pltpu.SemaphoreType.DMA]),
        compiler_params=_params(("arbitrary",), 32),
        name="moe_dispatch",
    )(pad_row, pad_len, n_used, dest.reshape(-1), h)


def _experts_kernel(be_ref, first_ref, slot_ref, next_ref, nu_ref, xs_ref, w1_hbm, w3_hbm, w2_hbm,
                    ys_ref, wbuf, w1_s, w3_s, w2_s, sem, *, layer):
    i = pl.program_id(0)
    used = i < nu_ref[0]
    w_hbm = (w1_hbm, w3_hbm, w2_hbm)
    w_s = (w1_s, w3_s, w2_s)

    def weight_copies(expert, slot):
        return [pltpu.make_async_copy(w_hbm[m].at[layer, expert], wbuf.at[slot, m],
                                      sem.at[slot, m]) for m in range(3)]

    @pl.when(i == 0)
    def _():
        for copy in weight_copies(be_ref[0], slot_ref[0]):
            copy.start()

    @pl.when(used & (first_ref[i] != 0))
    def _():
        slot = slot_ref[i]
        for copy in weight_copies(be_ref[i], slot):
            copy.wait()
        for m in range(3):
            w_s[m][...] = wbuf[slot, m].astype(BF16)
        nxt = next_ref[i]

        @pl.when(nxt >= 0)
        def _():
            for copy in weight_copies(nxt, 1 - slot):
                copy.start()

    @pl.when(used)
    def _():
        x = xs_ref[...].astype(BF16)
        h1 = _dot(x, w1_s[...])
        h3 = _dot(x, w3_s[...])
        act = (h1 * _sigmoid(h1)) * h3
        ys_ref[...] = _dot(act.astype(BF16), w2_s[...])

    @pl.when(i >= nu_ref[0])
    def _():
        ys_ref[...] = jnp.zeros_like(ys_ref)


def _experts(xs, block_e, pad_end, n_used, w1, w3, w2, layer):
    n_slots, d = xs.shape
    de = w1.shape[3]
    assert d == de
    tm = MOE_ROWS
    n_blocks = n_slots // tm
    prev_e = jnp.concatenate([jnp.full((1,), -1, jnp.int32), block_e[:-1]])
    first = (block_e != prev_e).astype(jnp.int32)
    slot = (jnp.cumsum(first) - 1) & 1
    run_end = pad_end[block_e] // tm
    nxt = jnp.where(run_end < n_used[0], block_e[jnp.minimum(run_end, n_blocks - 1)], -1)
    any_spec = pl.BlockSpec(memory_space=pl.ANY)
    row_spec = lambda f: pl.BlockSpec((tm, d), lambda i, be, fi, sl, nx, nu: (f(i, nu), 0))
    return pl.pallas_call(
        functools.partial(_experts_kernel, layer=layer),
        out_shape=jax.ShapeDtypeStruct((n_slots, d), F32),
        grid_spec=pltpu.PrefetchScalarGridSpec(
            num_scalar_prefetch=5,
            grid=(n_blocks,),
            in_specs=[row_spec(lambda i, nu: jnp.minimum(i, nu[0] - 1)),
                      any_spec, any_spec, any_spec],
            out_specs=row_spec(lambda i, nu: i),
            scratch_shapes=[pltpu.VMEM((2, 3, d, de), F32),
                            pltpu.VMEM((d, de), BF16), pltpu.VMEM((d, de), BF16),
                            pltpu.VMEM((de, d), BF16), pltpu.SemaphoreType.DMA((2, 3))]),
        compiler_params=_params(("arbitrary",), 52),
        name="moe_experts",
    )(block_e, first, slot.astype(jnp.int32), nxt.astype(jnp.int32), n_used, xs, w1, w3, w2)


def _combine_kernel(dest_ref, next_ref, x_ref, gate_ref, mod_ref, gf_ref, ys_hbm, *rest,
                    n_ctx_tiles, final_norm):
    buf, sem = rest[-2:]
    tm = x_ref.shape[0]
    i = pl.program_id(0)
    slot = i & 1

    def gather(idx_ref, into):
        def issue(g, c):
            for r in range(ROW_DMA_UNROLL):
                t = g * ROW_DMA_UNROLL + r
                for k in range(TOP_K):
                    _row_copy(ys_hbm, idx_ref[TOP_K * t + k], buf.at[into, k], t,
                              sem.at[into]).start(priority=k)
            return c
        lax.fori_loop(0, tm // ROW_DMA_UNROLL, issue, 0)

    pl.when(i == 0)(lambda: gather(dest_ref, slot))
    pl.when(i + 1 < pl.num_programs(0))(lambda: gather(next_ref, 1 - slot))

    def drain(g, c):
        for _ in range(ROW_DMA_UNROLL * TOP_K):
            _row_copy(ys_hbm, 0, buf.at[slot, 0], 0, sem.at[slot]).wait()
        return c

    lax.fori_loop(0, tm // ROW_DMA_UNROLL, drain, 0)
    gate = gate_ref[...]
    m = gate[:, 0:1] * buf[slot, 0] + gate[:, 1:2] * buf[slot, 1]
    x = x_ref[...] + mod_ref[5:6, :] * m
    if not final_norm:
        rest[0][...] = x
        return
    ms = jnp.mean(x * x, axis=-1, keepdims=True)
    x = x * lax.rsqrt(ms + NORM_EPS) * gf_ref[...]
    @pl.when(i < n_ctx_tiles)
    def _():
        rest[0][...] = x

    @pl.when(i >= n_ctx_tiles)
    def _():
        rest[1][...] = x


def _combine(x, ys, dest, gates, mod, g_final, *, n_ctx_tiles, final_norm):
    t, d = x.shape
    tm = TOKEN_TILE
    lat_tiles = (t // tm - n_ctx_tiles) // (mod.shape[0] - 1)

    def cond_of(i):
        return jnp.where(i < n_ctx_tiles, 0, 1 + (i - n_ctx_tiles) // lat_tiles)

    if final_norm:
        out_shape = (jax.ShapeDtypeStruct((n_ctx_tiles * tm, d), F32),
                     jax.ShapeDtypeStruct((t - n_ctx_tiles * tm, d), F32))
        out_specs = (pl.BlockSpec((tm, d), lambda i: (jnp.minimum(i, n_ctx_tiles - 1), 0)),
                     pl.BlockSpec((tm, d), lambda i: (jnp.maximum(i - n_ctx_tiles, 0), 0)))
        aliases = {}
    else:
        out_shape = jax.ShapeDtypeStruct((t, d), F32)
        out_specs = pl.BlockSpec((tm, d), lambda i: (i, 0))
        aliases = {2: 0}
    n_tiles = t // tm
    return pl.pallas_call(
        functools.partial(_combine_kernel, n_ctx_tiles=n_ctx_tiles, final_norm=final_norm),
        out_shape=out_shape,
        grid=(n_tiles,),
        in_specs=[pl.BlockSpec((TOP_K * tm,), lambda i: (i,), memory_space=pltpu.SMEM),
                  pl.BlockSpec((TOP_K * tm,), lambda i: (jnp.minimum(i + 1, n_tiles - 1),),
                               memory_space=pltpu.SMEM),
                  pl.BlockSpec((tm, d), lambda i: (i, 0)),
                  pl.BlockSpec((tm, TOP_K), lambda i: (i, 0)),
                  pl.BlockSpec((None, 6, d), lambda i: (cond_of(i), 0, 0)),
                  pl.BlockSpec((1, d), lambda i: (0, 0)),
                  pl.BlockSpec(memory_space=pl.ANY)],
        out_specs=out_specs,
        scratch_shapes=[pltpu.VMEM((2, TOP_K, tm, d), F32), pltpu.SemaphoreType.DMA((2,))],
        input_output_aliases=aliases,
        compiler_params=_params(("arbitrary",), 40),
        name="moe_combine",
    )(dest.reshape(-1), dest.reshape(-1), x, gates, mod, g_final.reshape(1, d), ys)


def _moe(x, mod, g, router_w, router_bias, w1, w3, w2, layer, g_final, *, n_ctx_tiles,
         final_norm):
    t, d = x.shape
    ne = router_w.shape[1]
    h, idx, gates, rank, cnt = _router(x, mod, g, router_w, router_bias, n_ctx_tiles=n_ctx_tiles)
    counts = cnt[:, 0].astype(jnp.int32)
    padded = (counts + MOE_ROWS - 1) // MOE_ROWS * MOE_ROWS
    pad_end = jnp.cumsum(padded)
    pad_start = pad_end - padded
    e_ids = jnp.arange(ne, dtype=jnp.int32)
    start_of = jnp.sum(jnp.where(idx[None] == e_ids[:, None, None],
                                 pad_start[:, None, None], 0), axis=0)
    dest = (start_of + rank).T
    n_blocks = -(-(t * TOP_K) // MOE_ROWS) + ne
    block_start = jnp.arange(n_blocks, dtype=jnp.int32) * MOE_ROWS
    block_e = jnp.minimum(jnp.sum(block_start[:, None] >= pad_end[None, :], axis=1),
                          ne - 1).astype(jnp.int32)
    n_used = (pad_end[-1:] // MOE_ROWS).astype(jnp.int32)
    xs = _dispatch(h, dest, pad_start + counts, padded - counts, n_used, n_blocks * MOE_ROWS)
    ys = _experts(xs, block_e, pad_end, n_used, w1, w3, w2, layer)
    return _combine(x, ys, dest, gates.T, mod, g_final, n_ctx_tiles=n_ctx_tiles,
                    final_norm=final_norm)


def kernel(x_prompt, x_sample, state_rglru, c, c_ctx, ada_w, ada_b, norm_mix, norm_moe, norm_final, hy_w_in, hy_b_in, hy_conv_w, hy_conv_b, hy_f_w1, hy_f_b1, hy_f_freq, hy_f_w2, hy_f_b2, hy_f_w3, hy_f_skip, hy_w_out, hy_b_out, rg_w_in, rg_b_in, rg_conv_w, rg_conv_b, rg_wa, rg_ba, rg_wx, rg_bx, rg_lambda, rg_w_out, rg_b_out, router_w, router_bias, moe_w1, moe_w3, moe_w2):
    nb_ctx, len_ctx, d = x_prompt.shape
    nb_lat, len_lat, _ = x_sample.shape
    depth = ada_w.shape[0]
    n_rg = rg_w_in.shape[0]
    d_rnn = rg_w_out.shape[1]
    tok_ctx = nb_ctx * len_ctx
    tiles_ctx = tok_ctx // TOKEN_TILE
    assert tok_ctx % TOKEN_TILE == 0 and TOKEN_TILE % len_ctx == 0 and TOKEN_TILE % GRID_W == 0
    assert len_lat % TOKEN_TILE == 0 and tok_ctx % len_lat == 0
    assert V7X_SUBLANES - (nb_lat + 1) >= 0
    lat_off = tok_ctx // len_lat

    x = jnp.concatenate([x_prompt.reshape(tok_ctx, d), x_sample.reshape(nb_lat * len_lat, d)], 0)
    cond = jnp.concatenate([c_ctx[None, :], c,
                            jnp.zeros((V7X_SUBLANES - 1 - nb_lat, d), F32)], axis=0)
    mods = _ada_modulation(cond, ada_w, ada_b)[:, :1 + nb_lat]
    ctx_h0 = jnp.zeros((nb_ctx, 2, d_rnn), F32)
    states = []

    for i in range(depth):
        mod = mods[i]
        j = i // 2
        if i % 2 == 0:
            u = _inproj(x, mod, norm_mix[i], hy_w_in[j].astype(BF16), hy_b_in[j], hy_conv_w[j],
                        hy_conv_b[j], n_ctx_tiles=tiles_ctx, ctx_len=len_ctx, lat_len=GRID_W,
                        plain_cols=0)
            filt_w = (hy_f_w1[j], hy_f_b1[j], hy_f_freq[j], hy_f_w2[j], hy_f_b2[j], hy_f_w3[j])
            y_ctx = _hyena_stream(u, 0, nb_ctx, len_ctx, filt_w, hy_f_skip[j], d)
            y_lat = _hyena_stream(u, lat_off, nb_lat, len_lat, filt_w, hy_f_skip[j], d)
            x = _outproj(y_ctx, y_lat, hy_w_out[j].astype(BF16), hy_b_out[j], mod, x,
                         n_ctx_tiles=tiles_ctx)
        else:
            u = _inproj(x, mod, norm_mix[i], rg_w_in[j].astype(BF16), rg_b_in[j], rg_conv_w[j],
                        rg_conv_b[j], n_ctx_tiles=tiles_ctx, ctx_len=len_ctx, lat_len=GRID_W,
                        plain_cols=d_rnn)
            wa, wx = rg_wa[j].astype(BF16), rg_wx[j].astype(BF16)
            y_ctx, st = _rglru_stream(u, ctx_h0, 0, nb_ctx, len_ctx, wa, wx, rg_ba[j], rg_bx[j],
                                      rg_lambda[j])
            y_lat, _ = _rglru_stream(u, state_rglru[:, j], lat_off, nb_lat, len_lat, wa, wx,
                                     rg_ba[j], rg_bx[j], rg_lambda[j])
            states.append(st)
            x = _outproj(y_ctx, y_lat, rg_w_out[j].astype(BF16), rg_b_out[j], mod, x,
                         n_ctx_tiles=tiles_ctx)
        x = _moe(x, mod, norm_moe[i], router_w, router_bias, moe_w1, moe_w3, moe_w2, i,
                 norm_final, n_ctx_tiles=tiles_ctx, final_norm=(i == depth - 1))

    y_prompt = x[0].reshape(nb_ctx, len_ctx, d)
    y_sample = x[1].reshape(nb_lat, len_lat, d)
    new_state = jnp.stack(states, axis=1).astype(x_prompt.dtype)
    return (y_prompt, y_sample, new_state)
```

```python
import functools
import math

import numpy as np
import jax
import jax.numpy as jnp
from jax import lax
from jax.experimental import pallas as pl
from jax.experimental.pallas import tpu as pltpu

F32 = jnp.float32
BF16 = jnp.bfloat16

GRID_W = 64
FILTER_BANDS = 16
FILTER_EPS = 1e-6
MIN_DECAY = math.log(1e-2) / 0.3
MAX_DECAY = math.log(1e-2) / 1.5
RG_C = 8.0
N_GROUPS = 4
TOP_K = 2
NORM_EPS = 1e-6

V7X_LANES = 128
V7X_SUBLANES = 8
V7X_VMEM_BYTES = 64 * 1024 * 1024

TOKEN_TILE = 512
COL_TILE = 1024
MOE_ROWS = 256
DFT_ROWS = 512
DFT_SPLIT = 64
RG_COLS = 256
RG_ROWS = 1024
ROW_DMA_UNROLL = 8


def _params(sem, vmem_mb):
    return pltpu.CompilerParams(dimension_semantics=sem,
                                vmem_limit_bytes=vmem_mb * 1024 * 1024)


def _dot(a, b):
    return jnp.dot(a, b, preferred_element_type=F32)


def _split(x):
    hi = x.astype(BF16)
    lo = (x - hi.astype(F32)).astype(BF16)
    return hi, lo


def _dot3(a, b):
    ah, al = _split(a)
    bh, bl = _split(b)
    return _dot(ah, bh) + (_dot(ah, bl) + _dot(al, bh))


def _dot3_nt(a, b):
    dn = (((1,), (1,)), ((), ()))
    d = lambda x, y: lax.dot_general(x, y, dn, preferred_element_type=F32)
    ah, al = _split(a)
    bh, bl = _split(b)
    return d(ah, bh) + (d(ah, bl) + d(al, bh))


def _sigmoid(x):
    return 1.0 / (1.0 + jnp.exp(-x))


def _pack_rows(x):
    c = x.shape[1] // 2
    bits = lax.bitcast_convert_type(x.astype(BF16).astype(F32), jnp.uint32)
    return (bits[:, c:] & jnp.uint32(0xFFFF0000)) | (bits[:, :c] >> 16)


def _unpack_rows(p):
    lo = lax.bitcast_convert_type(p << 16, F32)
    hi = lax.bitcast_convert_type(p & jnp.uint32(0xFFFF0000), F32)
    return jnp.concatenate([lo, hi], axis=1)


def _sigmoid_tanh(x):
    return 0.5 * jnp.tanh(0.5 * x) + 0.5


def _norm_mod(x, g, shift, scale):
    ms = jnp.mean(x * x, axis=-1, keepdims=True)
    return (x * lax.rsqrt(ms + NORM_EPS) * g) * (1.0 + scale) + shift


def _ada_kernel(c_ref, w_ref, b_ref, o_ref):
    c = c_ref[...]
    o_ref[...] = _dot3(c * _sigmoid(c), w_ref[...]) + b_ref[...]


def _ada_modulation(cond, ada_w, ada_b):
    depth, d, n = ada_w.shape
    tn = n // 4
    out = pl.pallas_call(
        _ada_kernel,
        out_shape=jax.ShapeDtypeStruct((depth, cond.shape[0], n), F32),
        grid=(depth, n // tn),
        in_specs=[pl.BlockSpec(cond.shape, lambda i, j: (0, 0)),
                  pl.BlockSpec((None, d, tn), lambda i, j: (i, 0, j)),
                  pl.BlockSpec((None, 1, tn), lambda i, j: (i, 0, j))],
        out_specs=pl.BlockSpec((None, cond.shape[0], tn), lambda i, j: (i, 0, j)),
        compiler_params=_params(("parallel", "parallel"), 40),
        name="ada_modulation",
    )(cond, ada_w, ada_b.reshape(depth, 1, n))
    return out.reshape(depth, cond.shape[0], 6, d)


def _inproj_kernel(x_ref, mod_ref, g_ref, w_ref, b_ref, cw_ref, cb_ref, mk_ref, o_ref, h_scr,
                   u_scr, *, sub, width, plain_col_tiles, col_tiles):
    s = pl.program_id(0)
    last = pl.num_programs(0) - 1
    j_cur = lax.rem(jnp.minimum(s, last - 1), col_tiles)
    j_prev = lax.rem(jnp.maximum(s - 1, 0), col_tiles)

    @pl.when(s == 0)
    def _():
        u_scr[...] = jnp.zeros_like(u_scr)

    @pl.when((j_cur == 0) & (s < last))
    def _():
        h = _norm_mod(x_ref[...], g_ref[...], mod_ref[0:1, :], mod_ref[1:2, :])
        h_scr[...] = h.astype(BF16)

    blocks = [slice(r, r + sub) for r in range(0, x_ref.shape[0], sub)]
    pad_l = (width - 1) // 2

    def step(with_conv):
        new = [_dot(h_scr[rows, :], w_ref[...]) + b_ref[...] for rows in blocks]
        for rows in blocks:
            u = u_scr[rows, :]
            if with_conv:
                acc = cw_ref[pad_l:pad_l + 1, :] * u + cb_ref[...]
                m = 0
                for k in range(width):
                    off = k - pad_l
                    if off != 0:
                        shifted = pltpu.roll(u, (-off) % sub, axis=0)
                        acc = acc + cw_ref[k:k + 1, :] * (shifted * mk_ref[m])
                        m += 1
                u = acc
            o_ref[rows, :] = u
        for rows, u in zip(blocks, new):
            u_scr[rows, :] = u

    if plain_col_tiles == 0:
        step(True)
    else:
        pl.when(j_prev >= plain_col_tiles)(lambda: step(True))
        pl.when(j_prev < plain_col_tiles)(lambda: step(False))


def _inproj(x, mod, g, w, b, conv_w, conv_b, *, n_ctx_tiles, ctx_len, lat_len, plain_cols):
    t, d = x.shape
    n = w.shape[1]
    tm, tn = TOKEN_TILE, COL_TILE
    width = conv_w.shape[0]
    plain_tiles = plain_cols // tn
    lat_tiles = (t // tm - n_ctx_tiles) // (mod.shape[0] - 1)

    def cond_of(i):
        return jnp.where(i < n_ctx_tiles, 0, 1 + (i - n_ctx_tiles) // lat_tiles)

    sub = max(ctx_len, lat_len)
    assert sub % ctx_len == 0 and sub % lat_len == 0 and tm % sub == 0
    pad_l = (width - 1) // 2
    offs = jnp.asarray([k - pad_l for k in range(width) if k != pad_l], jnp.int32)
    seg = jnp.asarray([ctx_len, lat_len], jnp.int32)[:, None, None]
    pos = jnp.arange(sub, dtype=jnp.int32)[None, None, :] % seg + offs[None, :, None]
    masks = jnp.broadcast_to(((pos >= 0) & (pos < seg)).astype(F32)[..., None],
                             (2, width - 1, sub, tn))

    nj = n // tn
    n_tiles = (t // tm) * nj

    def cur(s):
        s = jnp.minimum(s, n_tiles - 1)
        return lax.div(s, nj), lax.rem(s, nj)

    def prev(s):
        s = jnp.maximum(s - 1, 0)
        return lax.div(s, nj), lax.rem(s, nj)

    conv_col = lambda s: jnp.maximum(prev(s)[1] - plain_tiles, 0)
    kern = functools.partial(_inproj_kernel, sub=sub, width=width, plain_col_tiles=plain_tiles,
                             col_tiles=nj)
    return pl.pallas_call(
        kern,
        out_shape=jax.ShapeDtypeStruct((t, n), F32),
        grid=(n_tiles + 1,),
        in_specs=[pl.BlockSpec((tm, d), lambda s: (cur(s)[0], 0)),
                  pl.BlockSpec((None, 6, d), lambda s: (cond_of(cur(s)[0]), 0, 0)),
                  pl.BlockSpec((1, d), lambda s: (0, 0)),
                  pl.BlockSpec((d, tn), lambda s: (0, cur(s)[1])),
                  pl.BlockSpec((1, tn), lambda s: (0, cur(s)[1])),
                  pl.BlockSpec((width, tn), lambda s: (0, conv_col(s))),
                  pl.BlockSpec((1, tn), lambda s: (0, conv_col(s))),
                  pl.BlockSpec((None, width - 1, sub, tn),
                               lambda s: (jnp.where(prev(s)[0] < n_ctx_tiles, 0, 1), 0, 0, 0))],
        out_specs=pl.BlockSpec((tm, tn), prev),
        scratch_shapes=[pltpu.VMEM((tm, d), BF16), pltpu.VMEM((tm, tn), F32)],
        compiler_params=_params(("arbitrary",), 48),
        name="norm_inproj_conv",
    )(x, mod, g.reshape(1, d), w, b.reshape(1, n), conv_w, conv_b.reshape(1, -1), masks)


def _outproj_kernel(yc_ref, yl_ref, w_ref, b_ref, mod_ref, x_ref, o_ref, *, n_ctx_tiles):
    i = pl.program_id(0)

    def run(y_ref):
        m = _dot(y_ref[...], w_ref[...]) + b_ref[...]
        o_ref[...] = x_ref[...] + mod_ref[2:3, :] * m

    pl.when(i < n_ctx_tiles)(lambda: run(yc_ref))
    pl.when(i >= n_ctx_tiles)(lambda: run(yl_ref))


def _outproj(y_ctx, y_lat, w, b, mod, x, *, n_ctx_tiles):
    t, d = x.shape
    tm = TOKEN_TILE
    lat_tiles = (t // tm - n_ctx_tiles) // (mod.shape[0] - 1)

    def cond_of(i):
        return jnp.where(i < n_ctx_tiles, 0, 1 + (i - n_ctx_tiles) // lat_tiles)

    return pl.pallas_call(
        functools.partial(_outproj_kernel, n_ctx_tiles=n_ctx_tiles),
        out_shape=jax.ShapeDtypeStruct((t, d), F32),
        grid=(t // tm,),
        in_specs=[pl.BlockSpec((tm, w.shape[0]), lambda i: (jnp.minimum(i, n_ctx_tiles - 1), 0)),
                  pl.BlockSpec((tm, w.shape[0]), lambda i: (jnp.maximum(i - n_ctx_tiles, 0), 0)),
                  pl.BlockSpec(w.shape, lambda i: (0, 0)),
                  pl.BlockSpec((1, d), lambda i: (0, 0)),
                  pl.BlockSpec((None, 6, d), lambda i: (cond_of(i), 0, 0)),
                  pl.BlockSpec((tm, d), lambda i: (i, 0))],
        out_specs=pl.BlockSpec((tm, d), lambda i: (i, 0)),
        input_output_aliases={5: 0},
        compiler_params=_params(("parallel",), 32),
        name="outproj_residual",
    )(y_ctx, y_lat, w, b.reshape(1, d), mod, x)


def _filter_kernel(z_ref, w1_ref, b1_ref, fr_ref, w2_ref, b2_ref, w3a_ref, w3b_ref, w3c_ref,
                   w3d_ref, t_ref, dl_ref, o_ref, h_scr, *, orders):
    @pl.when(pl.program_id(0) == 0)
    def _():
        fr = fr_ref[...]
        h1 = jnp.sin(fr * (_dot3(z_ref[...], w1_ref[...]) + b1_ref[...]))
        h_scr[...] = jnp.sin(fr * (_dot3(h1, w2_ref[...]) + b2_ref[...]))

    h = h_scr[...]
    window = jnp.exp(-t_ref[...] * dl_ref[...])
    row = lax.broadcasted_iota(jnp.int32, (h.shape[0], 1), 0)
    w3 = ((w3a_ref, w3b_ref), (w3c_ref, w3d_ref))
    for o in range(orders):
        h_fwd = _dot3(h, w3[0][o][...]) * window
        h_bwd = jnp.where(row == 0, 0.0, _dot3(h, w3[1][o][...]) * window)
        norm = (jnp.sum(jnp.abs(h_fwd), axis=0, keepdims=True)
                + jnp.sum(jnp.abs(h_bwd), axis=0, keepdims=True) + FILTER_EPS)
        o_ref[2 * o] = h_fwd / norm
        o_ref[2 * o + 1] = h_bwd / norm


def _hyena_filter(length, f_w1, f_b1, f_freq, f_w2, f_b2, f_w3, d):
    emb, hid = f_w1.shape
    orders = f_w3.shape[1] // (2 * d)
    t = jnp.linspace(0.0, 1.0, length, dtype=F32)[:, None]
    w = (2.0 * math.pi / length) * jnp.arange(length, dtype=F32)[:, None]
    bands = jnp.linspace(1e-4, FILTER_BANDS - 1, FILTER_BANDS, dtype=F32)[None, :]
    z = jnp.concatenate([t, jnp.cos(bands * w), -jnp.sin(bands * w)], axis=-1)
    emb_pad = V7X_LANES
    z = jnp.pad(z, ((0, 0), (0, emb_pad - emb)))
    w1 = jnp.pad(f_w1, ((0, emb_pad - emb), (0, 0)))
    deltas = jnp.abs(jnp.linspace(MIN_DECAY, MAX_DECAY, d, dtype=F32))[None, :]
    td = 256
    nd = d // td
    full = lambda shape: pl.BlockSpec(shape, lambda j: (0,) * len(shape))
    w3_spec = lambda side, o: pl.BlockSpec((hid, td), lambda j: (0, (side * orders + o) * nd + j))
    assert orders == 2
    return pl.pallas_call(
        functools.partial(_filter_kernel, orders=orders),
        out_shape=jax.ShapeDtypeStruct((2 * orders, length, d), F32),
        grid=(nd,),
        in_specs=[full((length, emb_pad)), full((emb_pad, hid)), full((1, hid)), full((1, hid)),
                  full((hid, hid)), full((1, hid)),
                  w3_spec(0, 0), w3_spec(0, 1), w3_spec(1, 0), w3_spec(1, 1),
                  full((length, 1)), pl.BlockSpec((1, td), lambda j: (0, j))],
        out_specs=pl.BlockSpec((2 * orders, length, td), lambda j: (0, 0, j)),
        scratch_shapes=[pltpu.VMEM((length, hid), F32)],
        compiler_params=_params(("arbitrary",), 48),
        name="hyena_filter",
    )(z, w1, f_b1.reshape(1, hid), f_freq.reshape(1, hid), f_w2, f_b2.reshape(1, hid),
      f_w3, f_w3, f_w3, f_w3, t, deltas)


def _dft_matrices(length):
    n = 2 * length
    f0n, f1n = DFT_SPLIT, length // DFT_SPLIT
    tt = np.arange(length, dtype=np.int64)[None, :]
    ang_a = 2.0 * np.pi * ((DFT_SPLIT * np.arange(f1n, dtype=np.int64)[:, None] * tt) % n) / n
    ang_b = 2.0 * np.pi * ((np.arange(f0n, dtype=np.int64)[:, None] * tt) % n) / n
    ca, sa = (jnp.asarray(f(ang_a), F32)[:, None, :] for f in (np.cos, np.sin))
    cb, sb = (jnp.asarray(f(ang_b), F32)[None, :, :] for f in (np.cos, np.sin))
    cos_m = (ca * cb - sa * sb).reshape(length, length)
    sin_m = (sa * cb + ca * sb).reshape(length, length)
    nyq = jnp.asarray(1.0 - 2.0 * (np.arange(length) % 2), F32)[None, :]
    f_is0 = (jnp.arange(length) == 0)[:, None]
    fwd = jnp.stack([cos_m, jnp.where(f_is0, nyq, -sin_m)], axis=0)
    cat, sat = (jnp.asarray(f(ang_a).T, F32)[:, :, None] for f in (np.cos, np.sin))
    cbt, sbt = (jnp.asarray(f(ang_b).T, F32)[:, None, :] for f in (np.cos, np.sin))
    f_is0_t = f_is0.T
    scale = jnp.where(f_is0_t, 1.0 / n, 2.0 / n)
    cos_t = (cat * cbt - sat * sbt).reshape(length, length) * scale
    im_t = jnp.where(f_is0_t, nyq.T, -(sat * cbt + cat * sbt).reshape(length, length)) * scale
    inv = jnp.concatenate([cos_t, im_t], axis=1)
    return fwd.astype(BF16), inv.astype(BF16)


def _spec_kernel(a_ref, hf_ref, hb_ref, o_ref, sum_scr, dif_scr, nyq_scr):
    m = pl.program_id(2)

    @pl.when(m == 0)
    def _():
        hf, hb = hf_ref[...], hb_ref[...]
        sum_scr[...] = (hf + hb).astype(BF16)
        dif_scr[...] = (hf - hb).astype(BF16)
        t = lax.broadcasted_iota(jnp.int32, (hb.shape[0], 1), 0)
        sign = (1 - 2 * (t & 1)).astype(F32)
        nyq_scr[...] = 2.0 * jnp.sum(sign * hb, axis=0, keepdims=True)

    o_ref[0] = _dot(a_ref[0], sum_scr[...])
    im = _dot(a_ref[1], dif_scr[...])
    row = lax.broadcasted_iota(jnp.int32, (im.shape[0], 1), 0)
    o_ref[1] = jnp.where((row == 0) & (m == 0), im + nyq_scr[...], im)


def _filter_spectrum(fwd, filt):
    n2, length, d = filt.shape
    orders = n2 // 2
    th, tc = DFT_ROWS // 2, 512
    filt2 = filt.reshape(n2 * length, d)
    return pl.pallas_call(
        _spec_kernel,
        out_shape=jax.ShapeDtypeStruct((orders, 2, length, d), F32),
        grid=(orders, d // tc, length // th),
        in_specs=[pl.BlockSpec((2, th, length), lambda o, c, m: (0, m, 0)),
                  pl.BlockSpec((length, tc), lambda o, c, m: (2 * o, c)),
                  pl.BlockSpec((length, tc), lambda o, c, m: (2 * o + 1, c))],
        out_specs=pl.BlockSpec((None, 2, th, tc), lambda o, c, m: (o, 0, m, c)),
        scratch_shapes=[pltpu.VMEM((length, tc), BF16), pltpu.VMEM((length, tc), BF16),
                        pltpu.VMEM((1, tc), F32)],
        compiler_params=_params(("parallel", "parallel", "arbitrary"), 48),
        name="hyena_filter_spectrum",
    )(fwd, filt2, filt2)


def _dft_fwd_kernel(a_ref, z_ref, k_ref, o_ref, z_scr):
    m = pl.program_id(1)

    @pl.when(m == 0)
    def _():
        z_scr[...] = z_ref[...].astype(BF16)

    xr = _dot(a_ref[0], z_scr[...])
    xi = _dot(a_ref[1], z_scr[...])
    kr, ki = k_ref[0], k_ref[1]
    row = lax.broadcasted_iota(jnp.int32, (xr.shape[0], 1), 0)
    dc = (row == 0) & (m == 0)
    o_ref[0] = (xr * kr - jnp.where(dc, 0.0, xi * ki)).astype(o_ref.dtype)
    o_ref[1] = jnp.where(dc, xi * ki, xr * ki + xi * kr).astype(o_ref.dtype)


def _dft_forward(fwd, src, spec, *, nb, length, row_off, col_blk, order):
    d = spec.shape[3]
    th = DFT_ROWS // 2
    return pl.pallas_call(
        _dft_fwd_kernel,
        out_shape=jax.ShapeDtypeStruct((nb, 2, length, d), BF16),
        grid=(nb, length // th),
        in_specs=[pl.BlockSpec((2, th, length), lambda b, m: (0, m, 0)),
                  pl.BlockSpec((length, d), lambda b, m: (row_off + b, col_blk)),
                  pl.BlockSpec((None, 2, th, d), lambda b, m: (order, 0, m, 0))],
        out_specs=pl.BlockSpec((None, 2, th, d), lambda b, m: (b, 0, m, 0)),
        scratch_shapes=[pltpu.VMEM((length, d), BF16)],
        compiler_params=_params(("parallel", "arbitrary"), 48),
        name="hyena_dft_forward",
    )(fwd, src, spec)


def _dft_inv_kernel(a_ref, p_ref, v_ref, g_ref, s_ref, o_ref):
    y = _dot(a_ref[...], p_ref[...])
    v = v_ref[...]
    o_ref[...] = (g_ref[...] * (y + v * s_ref[...])).astype(o_ref.dtype)


def _dft_inverse(inv, prod, vsrc, gsrc, skip, *, nb, length, v_off, v_col, g_off, g_col,
                 out_dtype):
    n = 2 * length
    d = prod.shape[3]
    prod = prod.reshape(nb * n, d)
    tm = min(DFT_ROWS, length)
    nt = length // tm
    return pl.pallas_call(
        _dft_inv_kernel,
        out_shape=jax.ShapeDtypeStruct((nb * length, d), out_dtype),
        grid=(nb, nt),
        in_specs=[pl.BlockSpec((tm, n), lambda b, m: (m, 0)),
                  pl.BlockSpec((n, d), lambda b, m: (b, 0)),
                  pl.BlockSpec((tm, d), lambda b, m: ((v_off + b) * nt + m, v_col)),
                  pl.BlockSpec((tm, d), lambda b, m: ((g_off + b) * nt + m, g_col)),
                  pl.BlockSpec((1, d), lambda b, m: (0, 0))],
        out_specs=pl.BlockSpec((tm, d), lambda b, m: (b * nt + m, 0)),
        compiler_params=_params(("parallel", "arbitrary"), 48),
        name="hyena_dft_inverse",
    )(inv, prod, vsrc, gsrc, skip.reshape(1, d))


def _spectral_product(acc, k_ref, order):
    th = acc.shape[0] // 2
    xr, xi = acc[:th], acc[th:]
    kr, ki = k_ref[order, 0:th, :], k_ref[order, th:, :]
    dc = lax.broadcasted_iota(jnp.int32, (th, 1), 0) == 0
    pr = xr * kr - jnp.where(dc, 0.0, xi * ki)
    pi = jnp.where(dc, xi * ki, xr * ki + xi * kr)
    return jnp.concatenate([pr, pi], axis=0).astype(BF16)


def _hyena_short_kernel(u_ref, f_ref, i_ref, k_ref, s_ref, o_ref):
    d = o_ref.shape[1]
    fwd, inv = f_ref[...], i_ref[...]
    z = u_ref[:, 0:d]
    for order in range(2):
        p = _spectral_product(_dot(fwd, z.astype(BF16)), k_ref, order)
        gate = u_ref[:, (order + 1) * d:(order + 2) * d]
        z = gate * (_dot(inv, p) + z * s_ref[order:order + 1, :])
    o_ref[...] = z.astype(o_ref.dtype)


def _hyena_short(u, fwd, inv, spec, f_skip, *, nb, length):
    n = 2 * length
    d = spec.shape[3]
    assert n == DFT_ROWS and spec.shape[:3] == (2, 2, length)
    fwd = fwd.reshape(n, length)
    return pl.pallas_call(
        _hyena_short_kernel,
        out_shape=jax.ShapeDtypeStruct((nb * length, d), BF16),
        grid=(nb,),
        in_specs=[pl.BlockSpec((length, 3 * d), lambda b: (b, 0)),
                  pl.BlockSpec((n, length), lambda b: (0, 0)),
                  pl.BlockSpec((length, n), lambda b: (0, 0)),
                  pl.BlockSpec((2, n, d), lambda b: (0, 0, 0)),
                  pl.BlockSpec((2, d), lambda b: (0, 0))],
        out_specs=pl.BlockSpec((length, d), lambda b: (b, 0)),
        compiler_params=_params(("parallel",), 48),
        name="hyena_short_sequences",
    )(u, fwd, inv, spec.reshape(2, n, d), f_skip)


def _hyena_stream(u, row_off, nb, length, filt_w, f_skip, d):
    fwd, inv = _dft_matrices(length)
    filt = _hyena_filter(length, *filt_w, d)
    spec = _filter_spectrum(fwd, filt)
    if 2 * length == DFT_ROWS:
        assert row_off == 0
        return _hyena_short(u, fwd, inv, spec, f_skip, nb=nb, length=length)
    p1 = _dft_forward(fwd, u, spec, nb=nb, length=length, row_off=row_off, col_blk=0, order=0)
    z1 = _dft_inverse(inv, p1, u, u, f_skip[0], nb=nb, length=length, v_off=row_off, v_col=0,
                      g_off=row_off, g_col=1, out_dtype=F32)
    p2 = _dft_forward(fwd, z1, spec, nb=nb, length=length, row_off=0, col_blk=0, order=1)
    return _dft_inverse(inv, p2, z1, u, f_skip[1], nb=nb, length=length, v_off=0, v_col=0,
                        g_off=row_off, g_col=2, out_dtype=BF16)


def _gelu_tanh(x):
    return 0.5 * x * (1.0 + jnp.tanh(math.sqrt(2.0 / math.pi) * (x + 0.044715 * (x * x * x))))


def _log1p(e):
    u = 1.0 + e
    d = u - 1.0
    return jnp.where(d == 0.0, e, jnp.log(u) * (e / jnp.where(d == 0.0, 1.0, d)))


def _rglru_kernel(gate_ref, rec_ref, wa_ref, wx_ref, ba_ref, bx_ref, lam_ref, h0_ref,
                  y_ref, st_ref, a_scr, b_scr):
    length, cols = rec_ref.shape
    groups = length // V7X_SUBLANES
    rec = rec_ref[...]
    rec16 = rec.astype(BF16)
    pos = lax.broadcasted_iota(jnp.int32, (groups, V7X_SUBLANES, cols), 1)

    for d in range(2):
        r_gate = _sigmoid_tanh(_dot(rec16, wa_ref[d]) + ba_ref[d])
        i_gate = _sigmoid_tanh(_dot(rec16, wx_ref[d]) + bx_ref[d])
        nlam = -lam_ref[d]
        softplus = jnp.maximum(nlam, 0.0) + _log1p(jnp.exp(-jnp.abs(nlam)))
        log_a = (-RG_C * softplus) * r_gate
        a = jnp.exp(log_a).reshape(groups, V7X_SUBLANES, cols)
        th = jnp.tanh(log_a)
        b = (jnp.sqrt(-2.0 * th / (1.0 - th)) * (i_gate * rec)).reshape(
            groups, V7X_SUBLANES, cols)
        for s in (1, 2, 4):
            if d == 0:
                a_sh = pltpu.roll(a, s, axis=1)
                b_sh = pltpu.roll(b, s, axis=1)
                live = pos >= s
            else:
                a_sh = pltpu.roll(a, V7X_SUBLANES - s, axis=1)
                b_sh = pltpu.roll(b, V7X_SUBLANES - s, axis=1)
                live = pos < V7X_SUBLANES - s
            b = jnp.where(live, a * b_sh, 0.0) + b
            a = jnp.where(live, a * a_sh, a)
        a_scr[d] = a.reshape(length, cols)
        b_scr[d] = b.reshape(length, cols)

    seqs = h0_ref.shape[0]
    seq_groups = groups // seqs

    def step(g, carry):
        out = []
        for q in range(seqs):
            cf, cb = carry[2 * q], carry[2 * q + 1]
            rf = pl.multiple_of((q * seq_groups + g) * V7X_SUBLANES, V7X_SUBLANES)
            rb = pl.multiple_of(((q + 1) * seq_groups - 1 - g) * V7X_SUBLANES, V7X_SUBLANES)
            hf = a_scr[0, pl.ds(rf, V7X_SUBLANES), :] * cf + b_scr[0, pl.ds(rf, V7X_SUBLANES), :]
            hb = a_scr[1, pl.ds(rb, V7X_SUBLANES), :] * cb + b_scr[1, pl.ds(rb, V7X_SUBLANES), :]
            b_scr[0, pl.ds(rf, V7X_SUBLANES), :] = hf
            b_scr[1, pl.ds(rb, V7X_SUBLANES), :] = hb
            out.append(jnp.broadcast_to(hf[V7X_SUBLANES - 1:V7X_SUBLANES, :], hf.shape))
            out.append(jnp.broadcast_to(hb[0:1, :], hb.shape))
        return tuple(out)

    init = tuple(jnp.broadcast_to(h0_ref[q, d], (V7X_SUBLANES, cols))
                 for q in range(seqs) for d in range(2))
    final = lax.fori_loop(0, seq_groups, step, init, unroll=max(1, 4 // seqs))
    for q in range(seqs):
        for d in range(2):
            st_ref[q, d] = final[2 * q + d][0:1, :]
    y_ref[...] = ((b_scr[0] + b_scr[1]) * _gelu_tanh(gate_ref[...])).astype(y_ref.dtype)


def _rglru_stream(u, h0, row_off, nb, length, wa, wx, ba, bx, lam):
    dr = u.shape[1] // 2
    tc = RG_COLS
    nc = dr // tc
    seqs = max(1, min(nb, RG_ROWS // length))
    assert nb % seqs == 0 and (row_off * length) % (seqs * length) == 0
    rows = seqs * length
    blk_off = row_off // seqs
    vec = lambda a: a.reshape(2, 1, dr)
    vec_spec = pl.BlockSpec((2, 1, tc), lambda b, c: (0, 0, c))
    w_spec = pl.BlockSpec((2, None, tc, tc), lambda b, c: (0, c, 0, 0))
    st_spec = pl.BlockSpec((seqs, 2, 1, tc), lambda b, c: (b, 0, 0, c))
    y, st = pl.pallas_call(
        _rglru_kernel,
        out_shape=(jax.ShapeDtypeStruct((nb * length, dr), BF16),
                   jax.ShapeDtypeStruct((nb, 2, 1, dr), F32)),
        grid=(nb // seqs, nc),
        in_specs=[pl.BlockSpec((rows, tc), lambda b, c: (blk_off + b, c)),
                  pl.BlockSpec((rows, tc), lambda b, c: (blk_off + b, nc + c)),
                  w_spec, w_spec, vec_spec, vec_spec, vec_spec, st_spec],
        out_specs=(pl.BlockSpec((rows, tc), lambda b, c: (b, c)), st_spec),
        scratch_shapes=[pltpu.VMEM((2, rows, tc), F32), pltpu.VMEM((2, rows, tc), F32)],
        compiler_params=_params(("parallel", "parallel"), 48),
        name="rglru_scan",
    )(u, u, wa, wx, vec(ba), vec(bx), vec(lam), h0.reshape(nb, 2, 1, dr))
    return y, st.reshape(nb, 2, dr)


def _router_kernel(x_ref, mod_ref, g_ref, rwt_ref, bias_ref, h_ref, idx_ref, gate_ref,
                   rank_ref, cnt_ref, tri_scr, carry_scr, *, n_experts):
    i = pl.program_id(0)
    tm = x_ref.shape[0]
    per_group = n_experts // N_GROUPS

    @pl.when(i == 0)
    def _():
        r = lax.broadcasted_iota(jnp.int32, (tm, tm), 0)
        c = lax.broadcasted_iota(jnp.int32, (tm, tm), 1)
        tri_scr[...] = jnp.where(r < c, 1.0, 0.0).astype(BF16)
        carry_scr[...] = jnp.zeros_like(carry_scr)

    h = _norm_mod(x_ref[...], g_ref[...], mod_ref[3:4, :], mod_ref[4:5, :])
    h_ref[...] = _pack_rows(h)
    logits = _dot3_nt(rwt_ref[...], h)
    p = jnp.exp(logits - jnp.max(logits, axis=0, keepdims=True))
    scores = p / jnp.sum(p, axis=0, keepdims=True)
    sel = scores + bias_ref[...]
    rows = [sel[e:e + 1, :] for e in range(n_experts)]

    best_val = None
    for gi in range(N_GROUPS):
        v = rows[gi * per_group:(gi + 1) * per_group]
        pair = None
        for a in range(per_group):
            for b in range(a + 1, per_group):
                s = v[a] + v[b]
                pair = s if pair is None else jnp.maximum(pair, s)
        if best_val is None:
            best_val, best_grp = pair, jnp.zeros_like(pair, dtype=jnp.int32)
        else:
            take = pair > best_val
            best_val = jnp.where(take, pair, best_val)
            best_grp = jnp.where(take, gi, best_grp)

    neg = jnp.float32(-jnp.inf)
    masked = [jnp.where(best_grp == e // per_group, rows[e], neg) for e in range(n_experts)]

    def argmax_first(vals):
        bv, bi = vals[0], jnp.zeros_like(best_grp)
        for e in range(1, n_experts):
            take = vals[e] > bv
            bv = jnp.where(take, vals[e], bv)
            bi = jnp.where(take, e, bi)
        return bi

    idx0 = argmax_first(masked)
    idx1 = argmax_first([jnp.where(idx0 == e, neg, masked[e]) for e in range(n_experts)])

    e_iota = lax.broadcasted_iota(jnp.int32, (n_experts, tm), 0)
    hit0 = e_iota == idx0
    hit1 = e_iota == idx1
    g0 = jnp.sum(jnp.where(hit0, scores, 0.0), axis=0, keepdims=True)
    g1 = jnp.sum(jnp.where(hit1, scores, 0.0), axis=0, keepdims=True)
    gsum = g0 + g1
    onehot = jnp.where(hit0 | hit1, 1.0, 0.0)
    before = _dot(onehot.astype(BF16), tri_scr[...]) + carry_scr[:, 0:1]
    r0 = jnp.sum(jnp.where(hit0, before, 0.0), axis=0, keepdims=True)
    r1 = jnp.sum(jnp.where(hit1, before, 0.0), axis=0, keepdims=True)
    idx_ref[0:1, :] = idx0
    idx_ref[1:2, :] = idx1
    gate_ref[0:1, :] = g0 / gsum
    gate_ref[1:2, :] = g1 / gsum
    rank_ref[0:1, :] = r0.astype(jnp.int32)
    rank_ref[1:2, :] = r1.astype(jnp.int32)
    carry_scr[...] = carry_scr[...] + jnp.sum(onehot, axis=1, keepdims=True)
    cnt_ref[...] = carry_scr[...]


def _router(x, mod, g, router_w, router_bias, *, n_ctx_tiles):
    t, d = x.shape
    ne = router_w.shape[1]
    tm = TOKEN_TILE
    lat_tiles = (t // tm - n_ctx_tiles) // (mod.shape[0] - 1)

    def cond_of(i):
        return jnp.where(i < n_ctx_tiles, 0, 1 + (i - n_ctx_tiles) // lat_tiles)

    row2 = pl.BlockSpec((TOP_K, tm), lambda i: (0, i))
    return pl.pallas_call(
        functools.partial(_router_kernel, n_experts=ne),
        out_shape=(jax.ShapeDtypeStruct((t, d // 2), jnp.uint32),
                   jax.ShapeDtypeStruct((TOP_K, t), jnp.int32),
                   jax.ShapeDtypeStruct((TOP_K, t), F32),
                   jax.ShapeDtypeStruct((TOP_K, t), jnp.int32),
                   jax.ShapeDtypeStruct((ne, V7X_LANES), F32)),
        grid=(t // tm,),
        in_specs=[pl.BlockSpec((tm, d), lambda i: (i, 0)),
                  pl.BlockSpec((None, 6, d), lambda i: (cond_of(i), 0, 0)),
                  pl.BlockSpec((1, d), lambda i: (0, 0)),
                  pl.BlockSpec((ne, d), lambda i: (0, 0)),
                  pl.BlockSpec((ne, 1), lambda i: (0, 0))],
        out_specs=(pl.BlockSpec((tm, d // 2), lambda i: (i, 0)), row2, row2, row2,
                   pl.BlockSpec((ne, V7X_LANES), lambda i: (0, 0))),
        scratch_shapes=[pltpu.VMEM((tm, tm), BF16), pltpu.VMEM((ne, V7X_LANES), F32)],
        compiler_params=_params(("arbitrary",), 32),
        name="moe_router",
    )(x, mod, g.reshape(1, d), router_w.T, router_bias.reshape(ne, 1))


def _row_copy(src, src_row, dst, dst_row, sem):
    return pltpu.make_async_copy(src.at[pl.ds(src_row, 1)], dst.at[pl.ds(dst_row, 1)], sem)


def _dispatch_kernel(pad_row_ref, pad_len_ref, used_ref, dest_ref, h_ref, xs_out, zero_scr, sem,
                     pad_sem):
    tm = h_ref.shape[0]
    i = pl.program_id(0)
    n_experts = pad_row_ref.shape[0]
    zrows = zero_scr.shape[0]

    @pl.when(i == 0)
    def _():
        zero_scr[...] = jnp.zeros_like(zero_scr)

    def zero_copy(wanted, start, size):
        start = jnp.where(wanted, start, 0)
        if size >= V7X_SUBLANES:
            start = pl.multiple_of(start, V7X_SUBLANES)
        return wanted, pltpu.make_async_copy(zero_scr.at[pl.ds(0, size)],
                                             xs_out.at[pl.ds(start, size)], pad_sem)

    e = jnp.minimum(i, n_experts - 1)
    pad_len = jnp.where(i < n_experts, pad_len_ref[e], 0)
    pad_row = pad_row_ref[e]
    head = pad_len & (V7X_SUBLANES - 1)
    pad_copies = [zero_copy(r < head, pad_row + r, 1) for r in range(V7X_SUBLANES - 1)]
    size = zrows
    while size >= V7X_SUBLANES:
        start = pad_row + head + ((pad_len - head) & ~(2 * size - 1))
        pad_copies.append(zero_copy((pad_len & size) != 0, start, size))
        size //= 2
    spare = used_ref[0] + i
    has_spare = (i < n_experts) & (spare < xs_out.shape[0] // MOE_ROWS)
    for part in range(MOE_ROWS // zrows):
        pad_copies.append(zero_copy(has_spare, spare * MOE_ROWS + part * zrows, zrows))
    for wanted, copy in pad_copies:
        pl.when(wanted)(copy.start)

    def issue(g, c):
        for r in range(ROW_DMA_UNROLL):
            t = g * ROW_DMA_UNROLL + r
            for k in range(TOP_K):
                _row_copy(h_ref, t, xs_out, dest_ref[TOP_K * t + k], sem).start(priority=k)
        return c

    def drain(g, c):
        for _ in range(ROW_DMA_UNROLL * TOP_K):
            _row_copy(h_ref, 0, xs_out, 0, sem).wait()
        return c

    lax.fori_loop(0, tm // ROW_DMA_UNROLL, issue, 0)
    lax.fori_loop(0, tm // ROW_DMA_UNROLL, drain, 0)
    for wanted, copy in pad_copies:
        pl.when(wanted)(copy.wait)


def _dispatch(h, dest, pad_row, pad_len, n_used, n_slots):
    t, d = h.shape
    tm = TOKEN_TILE
    assert t // tm >= pad_row.shape[0] and n_slots % MOE_ROWS == 0
    return pl.pallas_call(
        _dispatch_kernel,
        out_shape=jax.ShapeDtypeStruct((n_slots, d), h.dtype),
        grid_spec=pltpu.PrefetchScalarGridSpec(
            num_scalar_prefetch=3,
            grid=(t // tm,),
            in_specs=[pl.BlockSpec((TOP_K * tm,), lambda i, pr, pn, nu: (i,),
                                   memory_space=pltpu.SMEM),
                      pl.BlockSpec((tm, d), lambda i, pr, pn, nu: (i, 0))],
            out_specs=pl.BlockSpec(memory_space=pl.ANY),
            scratch_shapes=[pltpu.VMEM((MOE_ROWS // 2, d), h.dtype),
                            pltpu.SemaphoreType.DMA, pltpu.SemaphoreType.DMA]),
        compiler_params=_params(("arbitrary",), 32),
        name="moe_dispatch",
    )(pad_row, pad_len, n_used, dest.reshape(-1), h)


def _experts_kernel(be_ref, first_ref, slot_ref, next_ref, nu_ref, xs_ref, w1_hbm, w3_hbm, w2_hbm,
                    ys_ref, wbuf, w1_s, w3_s, w2_s, sem, *, layer):
    i = pl.program_id(0)
    used = i < nu_ref[0]
    w_hbm = (w1_hbm, w3_hbm, w2_hbm)
    w_s = (w1_s, w3_s, w2_s)

    def weight_copies(expert, slot):
        return [pltpu.make_async_copy(w_hbm[m].at[layer, expert], wbuf.at[slot, m],
                                      sem.at[slot, m]) for m in range(3)]

    @pl.when(i == 0)
    def _():
        for copy in weight_copies(be_ref[0], slot_ref[0]):
            copy.start()

    @pl.when(used & (first_ref[i] != 0))
    def _():
        slot = slot_ref[i]
        for copy in weight_copies(be_ref[i], slot):
            copy.wait()
        for m in range(3):
            w_s[m][...] = wbuf[slot, m].astype(BF16)
        nxt = next_ref[i]

        @pl.when(nxt >= 0)
        def _():
            for copy in weight_copies(nxt, 1 - slot):
                copy.start()

    @pl.when(used)
    def _():
        x = _unpack_rows(xs_ref[...]).astype(BF16)
        h1 = _dot(x, w1_s[...])
        h3 = _dot(x, w3_s[...])
        act = (h1 * _sigmoid(h1)) * h3
        ys_ref[...] = _pack_rows(_dot(act.astype(BF16), w2_s[...]))

    @pl.when(i >= nu_ref[0])
    def _():
        ys_ref[...] = jnp.zeros_like(ys_ref)


def _experts(xs, block_e, pad_end, n_used, w1, w3, w2, layer):
    n_slots, dp = xs.shape
    d, de = w1.shape[2:]
    assert d == de == 2 * dp
    tm = MOE_ROWS
    n_blocks = n_slots // tm
    prev_e = jnp.concatenate([jnp.full((1,), -1, jnp.int32), block_e[:-1]])
    first = (block_e != prev_e).astype(jnp.int32)
    slot = (jnp.cumsum(first) - 1) & 1
    run_end = pad_end[block_e] // tm
    nxt = jnp.where(run_end < n_used[0], block_e[jnp.minimum(run_end, n_blocks - 1)], -1)
    any_spec = pl.BlockSpec(memory_space=pl.ANY)
    row_spec = lambda f: pl.BlockSpec((tm, dp), lambda i, be, fi, sl, nx, nu: (f(i, nu), 0))
    return pl.pallas_call(
        functools.partial(_experts_kernel, layer=layer),
        out_shape=jax.ShapeDtypeStruct((n_slots, dp), jnp.uint32),
        grid_spec=pltpu.PrefetchScalarGridSpec(
            num_scalar_prefetch=5,
            grid=(n_blocks,),
            in_specs=[row_spec(lambda i, nu: jnp.minimum(i, nu[0] - 1)),
                      any_spec, any_spec, any_spec],
            out_specs=row_spec(lambda i, nu: i),
            scratch_shapes=[pltpu.VMEM((2, 3, d, de), F32),
                            pltpu.VMEM((d, de), BF16), pltpu.VMEM((d, de), BF16),
                            pltpu.VMEM((de, d), BF16), pltpu.SemaphoreType.DMA((2, 3))]),
        compiler_params=_params(("arbitrary",), 52),
        name="moe_experts",
    )(block_e, first, slot.astype(jnp.int32), nxt.astype(jnp.int32), n_used, xs, w1, w3, w2)


def _combine_kernel(dest_ref, next_ref, x_ref, gate_ref, mod_ref, gf_ref, ys_hbm, *rest,
                    n_ctx_tiles, final_norm):
    buf, sem = rest[-2:]
    tm = x_ref.shape[0]
    i = pl.program_id(0)
    slot = i & 1

    def gather(idx_ref, into):
        def issue(g, c):
            for r in range(ROW_DMA_UNROLL):
                t = g * ROW_DMA_UNROLL + r
                for k in range(TOP_K):
                    _row_copy(ys_hbm, idx_ref[TOP_K * t + k], buf.at[into, k], t,
                              sem.at[into]).start(priority=k)
            return c
        lax.fori_loop(0, tm // ROW_DMA_UNROLL, issue, 0)

    pl.when(i == 0)(lambda: gather(dest_ref, slot))
    pl.when(i + 1 < pl.num_programs(0))(lambda: gather(next_ref, 1 - slot))

    def drain(g, c):
        for _ in range(ROW_DMA_UNROLL * TOP_K):
            _row_copy(ys_hbm, 0, buf.at[slot, 0], 0, sem.at[slot]).wait()
        return c

    lax.fori_loop(0, tm // ROW_DMA_UNROLL, drain, 0)
    gate = gate_ref[...]
    m = (gate[:, 0:1] * _unpack_rows(buf[slot, 0])
         + gate[:, 1:2] * _unpack_rows(buf[slot, 1]))
    x = x_ref[...] + mod_ref[5:6, :] * m
    if not final_norm:
        rest[0][...] = x
        return
    ms = jnp.mean(x * x, axis=-1, keepdims=True)
    x = x * lax.rsqrt(ms + NORM_EPS) * gf_ref[...]
    @pl.when(i < n_ctx_tiles)
    def _():
        rest[0][...] = x

    @pl.when(i >= n_ctx_tiles)
    def _():
        rest[1][...] = x


def _combine(x, ys, dest, gates, mod, g_final, *, n_ctx_tiles, final_norm):
    t, d = x.shape
    tm = TOKEN_TILE
    lat_tiles = (t // tm - n_ctx_tiles) // (mod.shape[0] - 1)

    def cond_of(i):
        return jnp.where(i < n_ctx_tiles, 0, 1 + (i - n_ctx_tiles) // lat_tiles)

    if final_norm:
        out_shape = (jax.ShapeDtypeStruct((n_ctx_tiles * tm, d), F32),
                     jax.ShapeDtypeStruct((t - n_ctx_tiles * tm, d), F32))
        out_specs = (pl.BlockSpec((tm, d), lambda i: (jnp.minimum(i, n_ctx_tiles - 1), 0)),
                     pl.BlockSpec((tm, d), lambda i: (jnp.maximum(i - n_ctx_tiles, 0), 0)))
        aliases = {}
    else:
        out_shape = jax.ShapeDtypeStruct((t, d), F32)
        out_specs = pl.BlockSpec((tm, d), lambda i: (i, 0))
        aliases = {2: 0}
    n_tiles = t // tm
    return pl.pallas_call(
        functools.partial(_combine_kernel, n_ctx_tiles=n_ctx_tiles, final_norm=final_norm),
        out_shape=out_shape,
        grid=(n_tiles,),
        in_specs=[pl.BlockSpec((TOP_K * tm,), lambda i: (i,), memory_space=pltpu.SMEM),
                  pl.BlockSpec((TOP_K * tm,), lambda i: (jnp.minimum(i + 1, n_tiles - 1),),
                               memory_space=pltpu.SMEM),
                  pl.BlockSpec((tm, d), lambda i: (i, 0)),
                  pl.BlockSpec((tm, TOP_K), lambda i: (i, 0)),
                  pl.BlockSpec((None, 6, d), lambda i: (cond_of(i), 0, 0)),
                  pl.BlockSpec((1, d), lambda i: (0, 0)),
                  pl.BlockSpec(memory_space=pl.ANY)],
        out_specs=out_specs,
        scratch_shapes=[pltpu.VMEM((2, TOP_K, tm, ys.shape[1]), ys.dtype),
                        pltpu.SemaphoreType.DMA((2,))],
        input_output_aliases=aliases,
        compiler_params=_params(("arbitrary",), 40),
        name="moe_combine",
    )(dest.reshape(-1), dest.reshape(-1), x, gates, mod, g_final.reshape(1, d), ys)


def _moe(x, mod, g, router_w, router_bias, w1, w3, w2, layer, g_final, *, n_ctx_tiles,
         final_norm):
    t, d = x.shape
    ne = router_w.shape[1]
    h, idx, gates, rank, cnt = _router(x, mod, g, router_w, router_bias, n_ctx_tiles=n_ctx_tiles)
    counts = cnt[:, 0].astype(jnp.int32)
    padded = (counts + MOE_ROWS - 1) // MOE_ROWS * MOE_ROWS
    pad_end = jnp.cumsum(padded)
    pad_start = pad_end - padded
    e_ids = jnp.arange(ne, dtype=jnp.int32)
    start_of = jnp.sum(jnp.where(idx[None] == e_ids[:, None, None],
                                 pad_start[:, None, None], 0), axis=0)
    dest = (start_of + rank).T
    n_blocks = -(-(t * TOP_K) // MOE_ROWS) + ne
    block_start = jnp.arange(n_blocks, dtype=jnp.int32) * MOE_ROWS
    block_e = jnp.minimum(jnp.sum(block_start[:, None] >= pad_end[None, :], axis=1),
                          ne - 1).astype(jnp.int32)
    n_used = (pad_end[-1:] // MOE_ROWS).astype(jnp.int32)
    xs = _dispatch(h, dest, pad_start + counts, padded - counts, n_used, n_blocks * MOE_ROWS)
    ys = _experts(xs, block_e, pad_end, n_used, w1, w3, w2, layer)
    return _combine(x, ys, dest, gates.T, mod, g_final, n_ctx_tiles=n_ctx_tiles,
                    final_norm=final_norm)


def kernel(x_prompt, x_sample, state_rglru, c, c_ctx, ada_w, ada_b, norm_mix, norm_moe, norm_final, hy_w_in, hy_b_in, hy_conv_w, hy_conv_b, hy_f_w1, hy_f_b1, hy_f_freq, hy_f_w2, hy_f_b2, hy_f_w3, hy_f_skip, hy_w_out, hy_b_out, rg_w_in, rg_b_in, rg_conv_w, rg_conv_b, rg_wa, rg_ba, rg_wx, rg_bx, rg_lambda, rg_w_out, rg_b_out, router_w, router_bias, moe_w1, moe_w3, moe_w2):
    nb_ctx, len_ctx, d = x_prompt.shape
    nb_lat, len_lat, _ = x_sample.shape
    depth = ada_w.shape[0]
    n_rg = rg_w_in.shape[0]
    d_rnn = rg_w_out.shape[1]
    tok_ctx = nb_ctx * len_ctx
    tiles_ctx = tok_ctx // TOKEN_TILE
    assert tok_ctx % TOKEN_TILE == 0 and TOKEN_TILE % len_ctx == 0 and TOKEN_TILE % GRID_W == 0
    assert len_lat % TOKEN_TILE == 0 and tok_ctx % len_lat == 0
    assert V7X_SUBLANES - (nb_lat + 1) >= 0
    lat_off = tok_ctx // len_lat

    x = jnp.concatenate([x_prompt.reshape(tok_ctx, d), x_sample.reshape(nb_lat * len_lat, d)], 0)
    cond = jnp.concatenate([c_ctx[None, :], c,
                            jnp.zeros((V7X_SUBLANES - 1 - nb_lat, d), F32)], axis=0)
    mods = _ada_modulation(cond, ada_w, ada_b)[:, :1 + nb_lat]
    ctx_h0 = jnp.zeros((nb_ctx, 2, d_rnn), F32)
    states = []

    for i in range(depth):
        mod = mods[i]
        j = i // 2
        if i % 2 == 0:
            u = _inproj(x, mod, norm_mix[i], hy_w_in[j].astype(BF16), hy_b_in[j], hy_conv_w[j],
                        hy_conv_b[j], n_ctx_tiles=tiles_ctx, ctx_len=len_ctx, lat_len=GRID_W,
                        plain_cols=0)
            filt_w = (hy_f_w1[j], hy_f_b1[j], hy_f_freq[j], hy_f_w2[j], hy_f_b2[j], hy_f_w3[j])
            y_ctx = _hyena_stream(u, 0, nb_ctx, len_ctx, filt_w, hy_f_skip[j], d)
            y_lat = _hyena_stream(u, lat_off, nb_lat, len_lat, filt_w, hy_f_skip[j], d)
            x = _outproj(y_ctx, y_lat, hy_w_out[j].astype(BF16), hy_b_out[j], mod, x,
                         n_ctx_tiles=tiles_ctx)
        else:
            u = _inproj(x, mod, norm_mix[i], rg_w_in[j].astype(BF16), rg_b_in[j], rg_conv_w[j],
                        rg_conv_b[j], n_ctx_tiles=tiles_ctx, ctx_len=len_ctx, lat_len=GRID_W,
                        plain_cols=d_rnn)
            wa, wx = rg_wa[j].astype(BF16), rg_wx[j].astype(BF16)
            y_ctx, st = _rglru_stream(u, ctx_h0, 0, nb_ctx, len_ctx, wa, wx, rg_ba[j], rg_bx[j],
                                      rg_lambda[j])
            y_lat, _ = _rglru_stream(u, state_rglru[:, j], lat_off, nb_lat, len_lat, wa, wx,
                                     rg_ba[j], rg_bx[j], rg_lambda[j])
            states.append(st)
            x = _outproj(y_ctx, y_lat, rg_w_out[j].astype(BF16), rg_b_out[j], mod, x,
                         n_ctx_tiles=tiles_ctx)
        x = _moe(x, mod, norm_moe[i], router_w, router_bias, moe_w1, moe_w3, moe_w2, i,
                 norm_final, n_ctx_tiles=tiles_ctx, final_norm=(i == depth - 1))

    y_prompt = x[0].reshape(nb_ctx, len_ctx, d)
    y_sample = x[1].reshape(nb_lat, len_lat, d)
    new_state = jnp.stack(states, axis=1).astype(x_prompt.dtype)
    return (y_prompt, y_sample, new_state)
```

```python
import functools
import math

import numpy as np
import jax
import jax.numpy as jnp
from jax import lax
from jax.experimental import pallas as pl
from jax.experimental.pallas import tpu as pltpu

F32 = jnp.float32
BF16 = jnp.bfloat16

GRID_W = 64
FILTER_BANDS = 16
FILTER_EPS = 1e-6
MIN_DECAY = math.log(1e-2) / 0.3
MAX_DECAY = math.log(1e-2) / 1.5
RG_C = 8.0
N_GROUPS = 4
TOP_K = 2
NORM_EPS = 1e-6

V7X_LANES = 128
V7X_SUBLANES = 8
V7X_VMEM_BYTES = 64 * 1024 * 1024

TOKEN_TILE = 512
COL_TILE = 1024
MOE_ROWS = 256
DFT_ROWS = 512
DFT_FREQS = 512
DFT_SPLIT = 64
RG_COLS = 256
RG_ROWS = 1024
ROW_DMA_UNROLL = 8


def _params(sem, vmem_mb):
    return pltpu.CompilerParams(dimension_semantics=sem,
                                vmem_limit_bytes=vmem_mb * 1024 * 1024)


def _dot(a, b):
    return jnp.dot(a, b, preferred_element_type=F32)


def _split(x):
    hi = x.astype(BF16)
    lo = (x - hi.astype(F32)).astype(BF16)
    return hi, lo


def _dot3(a, b):
    ah, al = _split(a)
    bh, bl = _split(b)
    return _dot(ah, bh) + (_dot(ah, bl) + _dot(al, bh))


def _dot3_nt(a, b):
    dn = (((1,), (1,)), ((), ()))
    d = lambda x, y: lax.dot_general(x, y, dn, preferred_element_type=F32)
    ah, al = _split(a)
    bh, bl = _split(b)
    return d(ah, bh) + (d(ah, bl) + d(al, bh))


def _sigmoid(x):
    return 1.0 / (1.0 + jnp.exp(-x))


def _pack_rows(x):
    c = x.shape[1] // 2
    bits = lax.bitcast_convert_type(x.astype(BF16).astype(F32), jnp.uint32)
    return (bits[:, c:] & jnp.uint32(0xFFFF0000)) | (bits[:, :c] >> 16)


def _unpack_rows(p):
    lo = lax.bitcast_convert_type(p << 16, F32)
    hi = lax.bitcast_convert_type(p & jnp.uint32(0xFFFF0000), F32)
    return jnp.concatenate([lo, hi], axis=1)


def _sigmoid_tanh(x):
    return 0.5 * jnp.tanh(0.5 * x) + 0.5


def _norm_mod(x, g, shift, scale):
    ms = jnp.mean(x * x, axis=-1, keepdims=True)
    return (x * lax.rsqrt(ms + NORM_EPS) * g) * (1.0 + scale) + shift


def _ada_kernel(c_ref, w_ref, b_ref, o_ref):
    c = c_ref[...]
    o_ref[...] = _dot3(c * _sigmoid(c), w_ref[...]) + b_ref[...]


def _ada_modulation(cond, ada_w, ada_b):
    depth, d, n = ada_w.shape
    tn = n // 4
    out = pl.pallas_call(
        _ada_kernel,
        out_shape=jax.ShapeDtypeStruct((depth, cond.shape[0], n), F32),
        grid=(depth, n // tn),
        in_specs=[pl.BlockSpec(cond.shape, lambda i, j: (0, 0)),
                  pl.BlockSpec((None, d, tn), lambda i, j: (i, 0, j)),
                  pl.BlockSpec((None, 1, tn), lambda i, j: (i, 0, j))],
        out_specs=pl.BlockSpec((None, cond.shape[0], tn), lambda i, j: (i, 0, j)),
        compiler_params=_params(("parallel", "parallel"), 40),
        name="ada_modulation",
    )(cond, ada_w, ada_b.reshape(depth, 1, n))
    return out.reshape(depth, cond.shape[0], 6, d)


def _inproj_kernel(x_ref, mod_ref, g_ref, w_ref, b_ref, cw_ref, cb_ref, mk_ref, o_ref, h_scr,
                   u_scr, *, sub, width, plain_col_tiles, col_tiles):
    s = pl.program_id(0)
    last = pl.num_programs(0) - 1
    j_cur = lax.rem(jnp.minimum(s, last - 1), col_tiles)
    j_prev = lax.rem(jnp.maximum(s - 1, 0), col_tiles)

    @pl.when(s == 0)
    def _():
        u_scr[...] = jnp.zeros_like(u_scr)

    @pl.when((j_cur == 0) & (s < last))
    def _():
        h = _norm_mod(x_ref[...], g_ref[...], mod_ref[0:1, :], mod_ref[1:2, :])
        h_scr[...] = h.astype(BF16)

    blocks = [slice(r, r + sub) for r in range(0, x_ref.shape[0], sub)]
    pad_l = (width - 1) // 2

    def step(with_conv):
        new = [_dot(h_scr[rows, :], w_ref[...]) + b_ref[...] for rows in blocks]
        for rows in blocks:
            u = u_scr[rows, :]
            if with_conv:
                acc = cw_ref[pad_l:pad_l + 1, :] * u + cb_ref[...]
                m = 0
                for k in range(width):
                    off = k - pad_l
                    if off != 0:
                        shifted = pltpu.roll(u, (-off) % sub, axis=0)
                        acc = acc + cw_ref[k:k + 1, :] * (shifted * mk_ref[m])
                        m += 1
                u = acc
            o_ref[rows, :] = u
        for rows, u in zip(blocks, new):
            u_scr[rows, :] = u

    if plain_col_tiles == 0:
        step(True)
    else:
        pl.when(j_prev >= plain_col_tiles)(lambda: step(True))
        pl.when(j_prev < plain_col_tiles)(lambda: step(False))


def _inproj(x, mod, g, w, b, conv_w, conv_b, *, n_ctx_tiles, ctx_len, lat_len, plain_cols):
    t, d = x.shape
    n = w.shape[1]
    tm, tn = TOKEN_TILE, COL_TILE
    width = conv_w.shape[0]
    plain_tiles = plain_cols // tn
    lat_tiles = (t // tm - n_ctx_tiles) // (mod.shape[0] - 1)

    def cond_of(i):
        return jnp.where(i < n_ctx_tiles, 0, 1 + (i - n_ctx_tiles) // lat_tiles)

    sub = max(ctx_len, lat_len)
    assert sub % ctx_len == 0 and sub % lat_len == 0 and tm % sub == 0
    pad_l = (width - 1) // 2
    offs = jnp.asarray([k - pad_l for k in range(width) if k != pad_l], jnp.int32)
    seg = jnp.asarray([ctx_len, lat_len], jnp.int32)[:, None, None]
    pos = jnp.arange(sub, dtype=jnp.int32)[None, None, :] % seg + offs[None, :, None]
    masks = jnp.broadcast_to(((pos >= 0) & (pos < seg)).astype(F32)[..., None],
                             (2, width - 1, sub, tn))

    nj = n // tn
    n_tiles = (t // tm) * nj

    def cur(s):
        s = jnp.minimum(s, n_tiles - 1)
        return lax.div(s, nj), lax.rem(s, nj)

    def prev(s):
        s = jnp.maximum(s - 1, 0)
        return lax.div(s, nj), lax.rem(s, nj)

    conv_col = lambda s: jnp.maximum(prev(s)[1] - plain_tiles, 0)
    kern = functools.partial(_inproj_kernel, sub=sub, width=width, plain_col_tiles=plain_tiles,
                             col_tiles=nj)
    return pl.pallas_call(
        kern,
        out_shape=jax.ShapeDtypeStruct((t, n), F32),
        grid=(n_tiles + 1,),
        in_specs=[pl.BlockSpec((tm, d), lambda s: (cur(s)[0], 0)),
                  pl.BlockSpec((None, 6, d), lambda s: (cond_of(cur(s)[0]), 0, 0)),
                  pl.BlockSpec((1, d), lambda s: (0, 0)),
                  pl.BlockSpec((d, tn), lambda s: (0, cur(s)[1])),
                  pl.BlockSpec((1, tn), lambda s: (0, cur(s)[1])),
                  pl.BlockSpec((width, tn), lambda s: (0, conv_col(s))),
                  pl.BlockSpec((1, tn), lambda s: (0, conv_col(s))),
                  pl.BlockSpec((None, width - 1, sub, tn),
                               lambda s: (jnp.where(prev(s)[0] < n_ctx_tiles, 0, 1), 0, 0, 0))],
        out_specs=pl.BlockSpec((tm, tn), prev),
        scratch_shapes=[pltpu.VMEM((tm, d), BF16), pltpu.VMEM((tm, tn), F32)],
        compiler_params=_params(("arbitrary",), 48),
        name="norm_inproj_conv",
    )(x, mod, g.reshape(1, d), w, b.reshape(1, n), conv_w, conv_b.reshape(1, -1), masks)


def _filter_kernel(z_ref, w1_ref, b1_ref, fr_ref, w2_ref, b2_ref, w3a_ref, w3b_ref, w3c_ref,
                   w3d_ref, t_ref, dl_ref, o_ref, h_scr, *, orders):
    @pl.when(pl.program_id(0) == 0)
    def _():
        fr = fr_ref[...]
        h1 = jnp.sin(fr * (_dot3(z_ref[...], w1_ref[...]) + b1_ref[...]))
        h_scr[...] = jnp.sin(fr * (_dot3(h1, w2_ref[...]) + b2_ref[...]))

    h = h_scr[...]
    window = jnp.exp(-t_ref[...] * dl_ref[...])
    row = lax.broadcasted_iota(jnp.int32, (h.shape[0], 1), 0)
    w3 = ((w3a_ref, w3b_ref), (w3c_ref, w3d_ref))
    for o in range(orders):
        h_fwd = _dot3(h, w3[0][o][...]) * window
        h_bwd = jnp.where(row == 0, 0.0, _dot3(h, w3[1][o][...]) * window)
        norm = (jnp.sum(jnp.abs(h_fwd), axis=0, keepdims=True)
                + jnp.sum(jnp.abs(h_bwd), axis=0, keepdims=True) + FILTER_EPS)
        o_ref[2 * o] = h_fwd / norm
        o_ref[2 * o + 1] = h_bwd / norm


def _hyena_filter(length, f_w1, f_b1, f_freq, f_w2, f_b2, f_w3, d):
    emb, hid = f_w1.shape
    orders = f_w3.shape[1] // (2 * d)
    t = jnp.linspace(0.0, 1.0, length, dtype=F32)[:, None]
    w = (2.0 * math.pi / length) * jnp.arange(length, dtype=F32)[:, None]
    bands = jnp.linspace(1e-4, FILTER_BANDS - 1, FILTER_BANDS, dtype=F32)[None, :]
    z = jnp.concatenate([t, jnp.cos(bands * w), -jnp.sin(bands * w)], axis=-1)
    emb_pad = V7X_LANES
    z = jnp.pad(z, ((0, 0), (0, emb_pad - emb)))
    w1 = jnp.pad(f_w1, ((0, emb_pad - emb), (0, 0)))
    deltas = jnp.abs(jnp.linspace(MIN_DECAY, MAX_DECAY, d, dtype=F32))[None, :]
    td = 256
    nd = d // td
    full = lambda shape: pl.BlockSpec(shape, lambda j: (0,) * len(shape))
    w3_spec = lambda side, o: pl.BlockSpec((hid, td), lambda j: (0, (side * orders + o) * nd + j))
    assert orders == 2
    return pl.pallas_call(
        functools.partial(_filter_kernel, orders=orders),
        out_shape=jax.ShapeDtypeStruct((2 * orders, length, d), F32),
        grid=(nd,),
        in_specs=[full((length, emb_pad)), full((emb_pad, hid)), full((1, hid)), full((1, hid)),
                  full((hid, hid)), full((1, hid)),
                  w3_spec(0, 0), w3_spec(0, 1), w3_spec(1, 0), w3_spec(1, 1),
                  full((length, 1)), pl.BlockSpec((1, td), lambda j: (0, j))],
        out_specs=pl.BlockSpec((2 * orders, length, td), lambda j: (0, 0, j)),
        scratch_shapes=[pltpu.VMEM((length, hid), F32)],
        compiler_params=_params(("arbitrary",), 48),
        name="hyena_filter",
    )(z, w1, f_b1.reshape(1, hid), f_freq.reshape(1, hid), f_w2, f_b2.reshape(1, hid),
      f_w3, f_w3, f_w3, f_w3, t, deltas)


def _dft_matrices(length):
    n = 2 * length
    f0n, f1n = DFT_SPLIT, length // DFT_SPLIT
    tt = np.arange(length, dtype=np.int64)[None, :]
    ang_a = 2.0 * np.pi * ((DFT_SPLIT * np.arange(f1n, dtype=np.int64)[:, None] * tt) % n) / n
    ang_b = 2.0 * np.pi * ((np.arange(f0n, dtype=np.int64)[:, None] * tt) % n) / n
    ca, sa = (jnp.asarray(f(ang_a), F32)[:, None, :] for f in (np.cos, np.sin))
    cb, sb = (jnp.asarray(f(ang_b), F32)[None, :, :] for f in (np.cos, np.sin))
    cos_m = (ca * cb - sa * sb).reshape(length, length)
    sin_m = (sa * cb + ca * sb).reshape(length, length)
    nyq = jnp.asarray(1.0 - 2.0 * (np.arange(length) % 2), F32)[None, :]
    f_is0 = (jnp.arange(length) == 0)[:, None]
    fwd = jnp.stack([cos_m, jnp.where(f_is0, nyq, -sin_m)], axis=0)
    cat, sat = (jnp.asarray(f(ang_a).T, F32)[:, :, None] for f in (np.cos, np.sin))
    cbt, sbt = (jnp.asarray(f(ang_b).T, F32)[:, None, :] for f in (np.cos, np.sin))
    f_is0_t = f_is0.T
    scale = jnp.where(f_is0_t, 1.0 / n, 2.0 / n)
    cos_t = (cat * cbt - sat * sbt).reshape(length, length) * scale
    im_t = jnp.where(f_is0_t, nyq.T, -(sat * cbt + cat * sbt).reshape(length, length)) * scale
    inv = jnp.concatenate([cos_t, im_t], axis=1)
    return fwd.astype(BF16), inv.astype(BF16)


def _spec_kernel(a_ref, hf_ref, hb_ref, o_ref, sum_scr, dif_scr, nyq_scr):
    m = pl.program_id(2)

    @pl.when(m == 0)
    def _():
        hf, hb = hf_ref[...], hb_ref[...]
        sum_scr[...] = (hf + hb).astype(BF16)
        dif_scr[...] = (hf - hb).astype(BF16)
        t = lax.broadcasted_iota(jnp.int32, (hb.shape[0], 1), 0)
        sign = (1 - 2 * (t & 1)).astype(F32)
        nyq_scr[...] = 2.0 * jnp.sum(sign * hb, axis=0, keepdims=True)

    o_ref[0] = _dot(a_ref[0], sum_scr[...])
    im = _dot(a_ref[1], dif_scr[...])
    row = lax.broadcasted_iota(jnp.int32, (im.shape[0], 1), 0)
    o_ref[1] = jnp.where((row == 0) & (m == 0), im + nyq_scr[...], im)


def _filter_spectrum(fwd, filt):
    n2, length, d = filt.shape
    orders = n2 // 2
    th, tc = min(DFT_FREQS, length), 512
    filt2 = filt.reshape(n2 * length, d)
    return pl.pallas_call(
        _spec_kernel,
        out_shape=jax.ShapeDtypeStruct((orders, 2, length, d), F32),
        grid=(orders, d // tc, length // th),
        in_specs=[pl.BlockSpec((2, th, length), lambda o, c, m: (0, m, 0)),
                  pl.BlockSpec((length, tc), lambda o, c, m: (2 * o, c)),
                  pl.BlockSpec((length, tc), lambda o, c, m: (2 * o + 1, c))],
        out_specs=pl.BlockSpec((None, 2, th, tc), lambda o, c, m: (o, 0, m, c)),
        scratch_shapes=[pltpu.VMEM((length, tc), BF16), pltpu.VMEM((length, tc), BF16),
                        pltpu.VMEM((1, tc), F32)],
        compiler_params=_params(("parallel", "parallel", "arbitrary"), 48),
        name="hyena_filter_spectrum",
    )(fwd, filt2, filt2)


def _dft_fwd_kernel(a_ref, z_ref, k_ref, o_ref, z_scr):
    m = pl.program_id(1)

    @pl.when(m == 0)
    def _():
        z_scr[...] = z_ref[...].astype(BF16)

    xr = _dot(a_ref[0], z_scr[...])
    xi = _dot(a_ref[1], z_scr[...])
    kr, ki = k_ref[0], k_ref[1]
    row = lax.broadcasted_iota(jnp.int32, (xr.shape[0], 1), 0)
    dc = (row == 0) & (m == 0)
    o_ref[0] = (xr * kr - jnp.where(dc, 0.0, xi * ki)).astype(o_ref.dtype)
    o_ref[1] = jnp.where(dc, xi * ki, xr * ki + xi * kr).astype(o_ref.dtype)


def _dft_forward(fwd, src, spec, *, nb, length, row_off, col_blk, order):
    d = spec.shape[3]
    th = min(DFT_FREQS, length)
    return pl.pallas_call(
        _dft_fwd_kernel,
        out_shape=jax.ShapeDtypeStruct((nb, 2, length, d), BF16),
        grid=(nb, length // th),
        in_specs=[pl.BlockSpec((2, th, length), lambda b, m: (0, m, 0)),
                  pl.BlockSpec((length, d), lambda b, m: (row_off + b, col_blk)),
                  pl.BlockSpec((None, 2, th, d), lambda b, m: (order, 0, m, 0))],
        out_specs=pl.BlockSpec((None, 2, th, d), lambda b, m: (b, 0, m, 0)),
        scratch_shapes=[pltpu.VMEM((length, d), BF16)],
        compiler_params=_params(("parallel", "arbitrary"), 56),
        name="hyena_dft_forward",
    )(fwd, src, spec)


def _dft_inv_kernel(a_ref, p_ref, v_ref, g_ref, s_ref, o_ref):
    y = _dot(a_ref[...], p_ref[...])
    v = v_ref[...]
    o_ref[...] = (g_ref[...] * (y + v * s_ref[...])).astype(o_ref.dtype)


def _dft_inverse(inv, prod, vsrc, gsrc, skip, *, nb, length, v_off, v_col, g_off, g_col,
                 out_dtype):
    n = 2 * length
    d = prod.shape[3]
    prod = prod.reshape(nb * n, d)
    tm = min(DFT_ROWS, length)
    nt = length // tm
    return pl.pallas_call(
        _dft_inv_kernel,
        out_shape=jax.ShapeDtypeStruct((nb * length, d), out_dtype),
        grid=(nb, nt),
        in_specs=[pl.BlockSpec((tm, n), lambda b, m: (m, 0)),
                  pl.BlockSpec((n, d), lambda b, m: (b, 0)),
                  pl.BlockSpec((tm, d), lambda b, m: ((v_off + b) * nt + m, v_col)),
                  pl.BlockSpec((tm, d), lambda b, m: ((g_off + b) * nt + m, g_col)),
                  pl.BlockSpec((1, d), lambda b, m: (0, 0))],
        out_specs=pl.BlockSpec((tm, d), lambda b, m: (b * nt + m, 0)),
        compiler_params=_params(("parallel", "arbitrary"), 48),
        name="hyena_dft_inverse",
    )(inv, prod, vsrc, gsrc, skip.reshape(1, d))


def _spectral_product(acc, k_ref, order):
    th = acc.shape[0] // 2
    xr, xi = acc[:th], acc[th:]
    kr, ki = k_ref[order, 0:th, :], k_ref[order, th:, :]
    dc = lax.broadcasted_iota(jnp.int32, (th, 1), 0) == 0
    pr = xr * kr - jnp.where(dc, 0.0, xi * ki)
    pi = jnp.where(dc, xi * ki, xr * ki + xi * kr)
    return jnp.concatenate([pr, pi], axis=0).astype(BF16)


def _hyena_short_kernel(u_ref, f_ref, i_ref, k_ref, s_ref, o_ref):
    d = o_ref.shape[1]
    fwd, inv = f_ref[...], i_ref[...]
    z = u_ref[:, 0:d]
    for order in range(2):
        p = _spectral_product(_dot(fwd, z.astype(BF16)), k_ref, order)
        gate = u_ref[:, (order + 1) * d:(order + 2) * d]
        z = gate * (_dot(inv, p) + z * s_ref[order:order + 1, :])
    o_ref[...] = z.astype(o_ref.dtype)


def _hyena_short(u, fwd, inv, spec, f_skip, *, nb, length):
    n = 2 * length
    d = spec.shape[3]
    assert n == DFT_ROWS and spec.shape[:3] == (2, 2, length)
    fwd = fwd.reshape(n, length)
    return pl.pallas_call(
        _hyena_short_kernel,
        out_shape=jax.ShapeDtypeStruct((nb * length, d), BF16),
        grid=(nb,),
        in_specs=[pl.BlockSpec((length, 3 * d), lambda b: (b, 0)),
                  pl.BlockSpec((n, length), lambda b: (0, 0)),
                  pl.BlockSpec((length, n), lambda b: (0, 0)),
                  pl.BlockSpec((2, n, d), lambda b: (0, 0, 0)),
                  pl.BlockSpec((2, d), lambda b: (0, 0))],
        out_specs=pl.BlockSpec((length, d), lambda b: (b, 0)),
        compiler_params=_params(("parallel",), 48),
        name="hyena_short_sequences",
    )(u, fwd, inv, spec.reshape(2, n, d), f_skip)


def _hyena_stream(u, row_off, nb, length, filt_w, f_skip, d):
    fwd, inv = _dft_matrices(length)
    filt = _hyena_filter(length, *filt_w, d)
    spec = _filter_spectrum(fwd, filt)
    if 2 * length == DFT_ROWS:
        assert row_off == 0
        return _hyena_short(u, fwd, inv, spec, f_skip, nb=nb, length=length)
    p1 = _dft_forward(fwd, u, spec, nb=nb, length=length, row_off=row_off, col_blk=0, order=0)
    z1 = _dft_inverse(inv, p1, u, u, f_skip[0], nb=nb, length=length, v_off=row_off, v_col=0,
                      g_off=row_off, g_col=1, out_dtype=F32)
    p2 = _dft_forward(fwd, z1, spec, nb=nb, length=length, row_off=0, col_blk=0, order=1)
    return _dft_inverse(inv, p2, z1, u, f_skip[1], nb=nb, length=length, v_off=0, v_col=0,
                        g_off=row_off, g_col=2, out_dtype=BF16)


def _gelu_tanh(x):
    return 0.5 * x * (1.0 + jnp.tanh(math.sqrt(2.0 / math.pi) * (x + 0.044715 * (x * x * x))))


def _log1p(e):
    u = 1.0 + e
    d = u - 1.0
    return jnp.where(d == 0.0, e, jnp.log(u) * (e / jnp.where(d == 0.0, 1.0, d)))


def _rglru_kernel(gate_ref, rec_ref, wa_ref, wx_ref, ba_ref, bx_ref, lam_ref, h0_ref,
                  y_ref, st_ref, a_scr, b_scr):
    length, cols = rec_ref.shape
    groups = length // V7X_SUBLANES
    rec = rec_ref[...]
    rec16 = rec.astype(BF16)
    pos = lax.broadcasted_iota(jnp.int32, (groups, V7X_SUBLANES, cols), 1)

    for d in range(2):
        r_gate = _sigmoid_tanh(_dot(rec16, wa_ref[d]) + ba_ref[d])
        i_gate = _sigmoid_tanh(_dot(rec16, wx_ref[d]) + bx_ref[d])
        nlam = -lam_ref[d]
        softplus = jnp.maximum(nlam, 0.0) + _log1p(jnp.exp(-jnp.abs(nlam)))
        log_a = (-RG_C * softplus) * r_gate
        a = jnp.exp(log_a).reshape(groups, V7X_SUBLANES, cols)
        th = jnp.tanh(log_a)
        b = (jnp.sqrt(-2.0 * th / (1.0 - th)) * (i_gate * rec)).reshape(
            groups, V7X_SUBLANES, cols)
        for s in (1, 2, 4):
            if d == 0:
                a_sh = pltpu.roll(a, s, axis=1)
                b_sh = pltpu.roll(b, s, axis=1)
                live = pos >= s
            else:
                a_sh = pltpu.roll(a, V7X_SUBLANES - s, axis=1)
                b_sh = pltpu.roll(b, V7X_SUBLANES - s, axis=1)
                live = pos < V7X_SUBLANES - s
            b = jnp.where(live, a * b_sh, 0.0) + b
            a = jnp.where(live, a * a_sh, a)
        a_scr[d] = a.reshape(length, cols)
        b_scr[d] = b.reshape(length, cols)

    seqs = h0_ref.shape[0]
    seq_groups = groups // seqs

    def step(g, carry):
        out = []
        for q in range(seqs):
            cf, cb = carry[2 * q], carry[2 * q + 1]
            rf = pl.multiple_of((q * seq_groups + g) * V7X_SUBLANES, V7X_SUBLANES)
            rb = pl.multiple_of(((q + 1) * seq_groups - 1 - g) * V7X_SUBLANES, V7X_SUBLANES)
            hf = a_scr[0, pl.ds(rf, V7X_SUBLANES), :] * cf + b_scr[0, pl.ds(rf, V7X_SUBLANES), :]
            hb = a_scr[1, pl.ds(rb, V7X_SUBLANES), :] * cb + b_scr[1, pl.ds(rb, V7X_SUBLANES), :]
            b_scr[0, pl.ds(rf, V7X_SUBLANES), :] = hf
            b_scr[1, pl.ds(rb, V7X_SUBLANES), :] = hb
            out.append(jnp.broadcast_to(hf[V7X_SUBLANES - 1:V7X_SUBLANES, :], hf.shape))
            out.append(jnp.broadcast_to(hb[0:1, :], hb.shape))
        return tuple(out)

    init = tuple(jnp.broadcast_to(h0_ref[q, d], (V7X_SUBLANES, cols))
                 for q in range(seqs) for d in range(2))
    final = lax.fori_loop(0, seq_groups, step, init, unroll=max(1, 4 // seqs))
    for q in range(seqs):
        for d in range(2):
            st_ref[q, d] = final[2 * q + d][0:1, :]
    y_ref[...] = ((b_scr[0] + b_scr[1]) * _gelu_tanh(gate_ref[...])).astype(y_ref.dtype)


def _rglru_stream(u, h0, row_off, nb, length, wa, wx, ba, bx, lam):
    dr = u.shape[1] // 2
    tc = RG_COLS
    nc = dr // tc
    seqs = max(1, min(nb, RG_ROWS // length))
    assert nb % seqs == 0 and (row_off * length) % (seqs * length) == 0
    rows = seqs * length
    blk_off = row_off // seqs
    vec = lambda a: a.reshape(2, 1, dr)
    vec_spec = pl.BlockSpec((2, 1, tc), lambda b, c: (0, 0, c))
    w_spec = pl.BlockSpec((2, None, tc, tc), lambda b, c: (0, c, 0, 0))
    st_spec = pl.BlockSpec((seqs, 2, 1, tc), lambda b, c: (b, 0, 0, c))
    y, st = pl.pallas_call(
        _rglru_kernel,
        out_shape=(jax.ShapeDtypeStruct((nb * length, dr), BF16),
                   jax.ShapeDtypeStruct((nb, 2, 1, dr), F32)),
        grid=(nb // seqs, nc),
        in_specs=[pl.BlockSpec((rows, tc), lambda b, c: (blk_off + b, c)),
                  pl.BlockSpec((rows, tc), lambda b, c: (blk_off + b, nc + c)),
                  w_spec, w_spec, vec_spec, vec_spec, vec_spec, st_spec],
        out_specs=(pl.BlockSpec((rows, tc), lambda b, c: (b, c)), st_spec),
        scratch_shapes=[pltpu.VMEM((2, rows, tc), F32), pltpu.VMEM((2, rows, tc), F32)],
        compiler_params=_params(("parallel", "parallel"), 48),
        name="rglru_scan",
    )(u, u, wa, wx, vec(ba), vec(bx), vec(lam), h0.reshape(nb, 2, 1, dr))
    return y, st.reshape(nb, 2, dr)


def _router_kernel(yc_ref, yl_ref, w_ref, b_ref, x_ref, mod_ref, g_ref, rwt_ref, bias_ref,
                   xo_ref, h_ref, idx_ref, gate_ref, rank_ref, cnt_ref, tri_scr, carry_scr, *,
                   n_experts, n_ctx_tiles):
    i = pl.program_id(0)
    tm = x_ref.shape[0]
    per_group = n_experts // N_GROUPS

    @pl.when(i == 0)
    def _():
        r = lax.broadcasted_iota(jnp.int32, (tm, tm), 0)
        c = lax.broadcasted_iota(jnp.int32, (tm, tm), 1)
        tri_scr[...] = jnp.where(r < c, 1.0, 0.0).astype(BF16)
        carry_scr[...] = jnp.zeros_like(carry_scr)

    y = jnp.where(i < n_ctx_tiles, yc_ref[...], yl_ref[...])
    x = x_ref[...] + mod_ref[2:3, :] * (_dot(y, w_ref[...]) + b_ref[...])
    xo_ref[...] = x
    h = _norm_mod(x, g_ref[...], mod_ref[3:4, :], mod_ref[4:5, :])
    h_ref[...] = _pack_rows(h)
    logits = _dot3_nt(rwt_ref[...], h)
    p = jnp.exp(logits - jnp.max(logits, axis=0, keepdims=True))
    scores = p / jnp.sum(p, axis=0, keepdims=True)
    sel = scores + bias_ref[...]
    rows = [sel[e:e + 1, :] for e in range(n_experts)]

    best_val = None
    for gi in range(N_GROUPS):
        v = rows[gi * per_group:(gi + 1) * per_group]
        pair = None
        for a in range(per_group):
            for b in range(a + 1, per_group):
                s = v[a] + v[b]
                pair = s if pair is None else jnp.maximum(pair, s)
        if best_val is None:
            best_val, best_grp = pair, jnp.zeros_like(pair, dtype=jnp.int32)
        else:
            take = pair > best_val
            best_val = jnp.where(take, pair, best_val)
            best_grp = jnp.where(take, gi, best_grp)

    neg = jnp.float32(-jnp.inf)
    masked = [jnp.where(best_grp == e // per_group, rows[e], neg) for e in range(n_experts)]

    def argmax_first(vals):
        bv, bi = vals[0], jnp.zeros_like(best_grp)
        for e in range(1, n_experts):
            take = vals[e] > bv
            bv = jnp.where(take, vals[e], bv)
            bi = jnp.where(take, e, bi)
        return bi

    idx0 = argmax_first(masked)
    idx1 = argmax_first([jnp.where(idx0 == e, neg, masked[e]) for e in range(n_experts)])

    e_iota = lax.broadcasted_iota(jnp.int32, (n_experts, tm), 0)
    hit0 = e_iota == idx0
    hit1 = e_iota == idx1
    g0 = jnp.sum(jnp.where(hit0, scores, 0.0), axis=0, keepdims=True)
    g1 = jnp.sum(jnp.where(hit1, scores, 0.0), axis=0, keepdims=True)
    gsum = g0 + g1
    onehot = jnp.where(hit0 | hit1, 1.0, 0.0)
    before = _dot(onehot.astype(BF16), tri_scr[...]) + carry_scr[:, 0:1]
    r0 = jnp.sum(jnp.where(hit0, before, 0.0), axis=0, keepdims=True)
    r1 = jnp.sum(jnp.where(hit1, before, 0.0), axis=0, keepdims=True)
    idx_ref[0:1, :] = idx0
    idx_ref[1:2, :] = idx1
    gate_ref[0:1, :] = g0 / gsum
    gate_ref[1:2, :] = g1 / gsum
    rank_ref[0:1, :] = r0.astype(jnp.int32)
    rank_ref[1:2, :] = r1.astype(jnp.int32)
    carry_scr[...] = carry_scr[...] + jnp.sum(onehot, axis=1, keepdims=True)
    cnt_ref[...] = carry_scr[...]


def _outproj_router(y_ctx, y_lat, w_out, b_out, x, mod, g, router_w, router_bias, *,
                    n_ctx_tiles):
    t, d = x.shape
    ne = router_w.shape[1]
    tm = TOKEN_TILE
    lat_tiles = (t // tm - n_ctx_tiles) // (mod.shape[0] - 1)

    def cond_of(i):
        return jnp.where(i < n_ctx_tiles, 0, 1 + (i - n_ctx_tiles) // lat_tiles)

    row2 = pl.BlockSpec((TOP_K, tm), lambda i: (0, i))
    full = lambda shape: pl.BlockSpec(shape, lambda i: (0,) * len(shape))
    return pl.pallas_call(
        functools.partial(_router_kernel, n_experts=ne, n_ctx_tiles=n_ctx_tiles),
        out_shape=(jax.ShapeDtypeStruct((t, d), F32),
                   jax.ShapeDtypeStruct((t, d // 2), jnp.uint32),
                   jax.ShapeDtypeStruct((TOP_K, t), jnp.int32),
                   jax.ShapeDtypeStruct((TOP_K, t), F32),
                   jax.ShapeDtypeStruct((TOP_K, t), jnp.int32),
                   jax.ShapeDtypeStruct((ne, V7X_LANES), F32)),
        grid=(t // tm,),
        in_specs=[pl.BlockSpec((tm, w_out.shape[0]),
                               lambda i: (jnp.minimum(i, n_ctx_tiles - 1), 0)),
                  pl.BlockSpec((tm, w_out.shape[0]),
                               lambda i: (jnp.maximum(i - n_ctx_tiles, 0), 0)),
                  full(w_out.shape), full((1, d)),
                  pl.BlockSpec((tm, d), lambda i: (i, 0)),
                  pl.BlockSpec((None, 6, d), lambda i: (cond_of(i), 0, 0)),
                  full((1, d)), full((ne, d)), full((ne, 1))],
        out_specs=(pl.BlockSpec((tm, d), lambda i: (i, 0)),
                   pl.BlockSpec((tm, d // 2), lambda i: (i, 0)), row2, row2, row2,
                   full((ne, V7X_LANES))),
        scratch_shapes=[pltpu.VMEM((tm, tm), BF16), pltpu.VMEM((ne, V7X_LANES), F32)],
        input_output_aliases={4: 0},
        compiler_params=_params(("arbitrary",), 40),
        name="outproj_router",
    )(y_ctx, y_lat, w_out, b_out.reshape(1, d), x, mod, g.reshape(1, d), router_w.T,
      router_bias.reshape(ne, 1))


def _row_copy(src, src_row, dst, dst_row, sem):
    return pltpu.make_async_copy(src.at[pl.ds(src_row, 1)], dst.at[pl.ds(dst_row, 1)], sem)


def _dispatch_kernel(pad_row_ref, pad_len_ref, used_ref, dest_ref, h_ref, xs_out, zero_scr, sem,
                     pad_sem):
    tm = h_ref.shape[0]
    i = pl.program_id(0)
    n_experts = pad_row_ref.shape[0]
    zrows = zero_scr.shape[0]

    @pl.when(i == 0)
    def _():
        zero_scr[...] = jnp.zeros_like(zero_scr)

    def zero_copy(wanted, start, size):
        start = jnp.where(wanted, start, 0)
        if size >= V7X_SUBLANES:
            start = pl.multiple_of(start, V7X_SUBLANES)
        return wanted, pltpu.make_async_copy(zero_scr.at[pl.ds(0, size)],
                                             xs_out.at[pl.ds(start, size)], pad_sem)

    e = jnp.minimum(i, n_experts - 1)
    pad_len = jnp.where(i < n_experts, pad_len_ref[e], 0)
    pad_row = pad_row_ref[e]
    head = pad_len & (V7X_SUBLANES - 1)
    pad_copies = [zero_copy(r < head, pad_row + r, 1) for r in range(V7X_SUBLANES - 1)]
    size = zrows
    while size >= V7X_SUBLANES:
        start = pad_row + head + ((pad_len - head) & ~(2 * size - 1))
        pad_copies.append(zero_copy((pad_len & size) != 0, start, size))
        size //= 2
    spare = used_ref[0] + i
    has_spare = (i < n_experts) & (spare < xs_out.shape[0] // MOE_ROWS)
    for part in range(MOE_ROWS // zrows):
        pad_copies.append(zero_copy(has_spare, spare * MOE_ROWS + part * zrows, zrows))
    for wanted, copy in pad_copies:
        pl.when(wanted)(copy.start)

    def issue(g, c):
        for r in range(ROW_DMA_UNROLL):
            t = g * ROW_DMA_UNROLL + r
            for k in range(TOP_K):
                _row_copy(h_ref, t, xs_out, dest_ref[TOP_K * t + k], sem).start(priority=k)
        return c

    def drain(g, c):
        for _ in range(ROW_DMA_UNROLL * TOP_K):
            _row_copy(h_ref, 0, xs_out, 0, sem).wait()
        return c

    lax.fori_loop(0, tm // ROW_DMA_UNROLL, issue, 0)
    lax.fori_loop(0, tm // ROW_DMA_UNROLL, drain, 0)
    for wanted, copy in pad_copies:
        pl.when(wanted)(copy.wait)


def _dispatch(h, dest, pad_row, pad_len, n_used, n_slots):
    t, d = h.shape
    tm = TOKEN_TILE
    assert t // tm >= pad_row.shape[0] and n_slots % MOE_ROWS == 0
    return pl.pallas_call(
        _dispatch_kernel,
        out_shape=jax.ShapeDtypeStruct((n_slots, d), h.dtype),
        grid_spec=pltpu.PrefetchScalarGridSpec(
            num_scalar_prefetch=3,
            grid=(t // tm,),
            in_specs=[pl.BlockSpec((TOP_K * tm,), lambda i, pr, pn, nu: (i,),
                                   memory_space=pltpu.SMEM),
                      pl.BlockSpec((tm, d), lambda i, pr, pn, nu: (i, 0))],
            out_specs=pl.BlockSpec(memory_space=pl.ANY),
            scratch_shapes=[pltpu.VMEM((MOE_ROWS // 2, d), h.dtype),
                            pltpu.SemaphoreType.DMA, pltpu.SemaphoreType.DMA]),
        compiler_params=_params(("arbitrary",), 32),
        name="moe_dispatch",
    )(pad_row, pad_len, n_used, dest.reshape(-1), h)


def _experts_kernel(be_ref, first_ref, slot_ref, next_ref, nu_ref, xs_ref, w1_hbm, w3_hbm, w2_hbm,
                    ys_ref, wbuf, w1_s, w3_s, w2_s, sem, *, layer):
    i = pl.program_id(0)
    used = i < nu_ref[0]
    w_hbm = (w1_hbm, w3_hbm, w2_hbm)
    w_s = (w1_s, w3_s, w2_s)

    def weight_copies(expert, slot):
        return [pltpu.make_async_copy(w_hbm[m].at[layer, expert], wbuf.at[slot, m],
                                      sem.at[slot, m]) for m in range(3)]

    @pl.when(i == 0)
    def _():
        for copy in weight_copies(be_ref[0], slot_ref[0]):
            copy.start()

    @pl.when(used & (first_ref[i] != 0))
    def _():
        slot = slot_ref[i]
        for copy in weight_copies(be_ref[i], slot):
            copy.wait()
        for m in range(3):
            w_s[m][...] = wbuf[slot, m].astype(BF16)
        nxt = next_ref[i]

        @pl.when(nxt >= 0)
        def _():
            for copy in weight_copies(nxt, 1 - slot):
                copy.start()

    @pl.when(used)
    def _():
        x = _unpack_rows(xs_ref[...]).astype(BF16)
        h1 = _dot(x, w1_s[...])
        h3 = _dot(x, w3_s[...])
        act = (h1 * _sigmoid(h1)) * h3
        ys_ref[...] = _pack_rows(_dot(act.astype(BF16), w2_s[...]))

    @pl.when(i >= nu_ref[0])
    def _():
        ys_ref[...] = jnp.zeros_like(ys_ref)


def _experts(xs, block_e, pad_end, n_used, w1, w3, w2, layer):
    n_slots, dp = xs.shape
    d, de = w1.shape[2:]
    assert d == de == 2 * dp
    tm = MOE_ROWS
    n_blocks = n_slots // tm
    prev_e = jnp.concatenate([jnp.full((1,), -1, jnp.int32), block_e[:-1]])
    first = (block_e != prev_e).astype(jnp.int32)
    slot = (jnp.cumsum(first) - 1) & 1
    run_end = pad_end[block_e] // tm
    nxt = jnp.where(run_end < n_used[0], block_e[jnp.minimum(run_end, n_blocks - 1)], -1)
    any_spec = pl.BlockSpec(memory_space=pl.ANY)
    row_spec = lambda f: pl.BlockSpec((tm, dp), lambda i, be, fi, sl, nx, nu: (f(i, nu), 0))
    return pl.pallas_call(
        functools.partial(_experts_kernel, layer=layer),
        out_shape=jax.ShapeDtypeStruct((n_slots, dp), jnp.uint32),
        grid_spec=pltpu.PrefetchScalarGridSpec(
            num_scalar_prefetch=5,
            grid=(n_blocks,),
            in_specs=[row_spec(lambda i, nu: jnp.minimum(i, nu[0] - 1)),
                      any_spec, any_spec, any_spec],
            out_specs=row_spec(lambda i, nu: i),
            scratch_shapes=[pltpu.VMEM((2, 3, d, de), F32),
                            pltpu.VMEM((d, de), BF16), pltpu.VMEM((d, de), BF16),
                            pltpu.VMEM((de, d), BF16), pltpu.SemaphoreType.DMA((2, 3))]),
        compiler_params=_params(("arbitrary",), 52),
        name="moe_experts",
    )(block_e, first, slot.astype(jnp.int32), nxt.astype(jnp.int32), n_used, xs, w1, w3, w2)


def _combine_kernel(dest_ref, next_ref, x_ref, gate_ref, mod_ref, gf_ref, ys_hbm, *rest,
                    n_ctx_tiles, final_norm):
    buf, sem = rest[-2:]
    tm = x_ref.shape[0]
    i = pl.program_id(0)
    slot = i & 1

    def gather(idx_ref, into):
        def issue(g, c):
            for r in range(ROW_DMA_UNROLL):
                t = g * ROW_DMA_UNROLL + r
                for k in range(TOP_K):
                    _row_copy(ys_hbm, idx_ref[TOP_K * t + k], buf.at[into, k], t,
                              sem.at[into]).start(priority=k)
            return c
        lax.fori_loop(0, tm // ROW_DMA_UNROLL, issue, 0)

    pl.when(i == 0)(lambda: gather(dest_ref, slot))
    pl.when(i + 1 < pl.num_programs(0))(lambda: gather(next_ref, 1 - slot))

    def drain(g, c):
        for _ in range(ROW_DMA_UNROLL * TOP_K):
            _row_copy(ys_hbm, 0, buf.at[slot, 0], 0, sem.at[slot]).wait()
        return c

    lax.fori_loop(0, tm // ROW_DMA_UNROLL, drain, 0)
    gate = gate_ref[...]
    m = (gate[:, 0:1] * _unpack_rows(buf[slot, 0])
         + gate[:, 1:2] * _unpack_rows(buf[slot, 1]))
    x = x_ref[...] + mod_ref[5:6, :] * m
    if not final_norm:
        rest[0][...] = x
        return
    ms = jnp.mean(x * x, axis=-1, keepdims=True)
    x = x * lax.rsqrt(ms + NORM_EPS) * gf_ref[...]
    @pl.when(i < n_ctx_tiles)
    def _():
        rest[0][...] = x

    @pl.when(i >= n_ctx_tiles)
    def _():
        rest[1][...] = x


def _combine(x, ys, dest, gates, mod, g_final, *, n_ctx_tiles, final_norm):
    t, d = x.shape
    tm = TOKEN_TILE
    lat_tiles = (t // tm - n_ctx_tiles) // (mod.shape[0] - 1)

    def cond_of(i):
        return jnp.where(i < n_ctx_tiles, 0, 1 + (i - n_ctx_tiles) // lat_tiles)

    if final_norm:
        out_shape = (jax.ShapeDtypeStruct((n_ctx_tiles * tm, d), F32),
                     jax.ShapeDtypeStruct((t - n_ctx_tiles * tm, d), F32))
        out_specs = (pl.BlockSpec((tm, d), lambda i: (jnp.minimum(i, n_ctx_tiles - 1), 0)),
                     pl.BlockSpec((tm, d), lambda i: (jnp.maximum(i - n_ctx_tiles, 0), 0)))
        aliases = {}
    else:
        out_shape = jax.ShapeDtypeStruct((t, d), F32)
        out_specs = pl.BlockSpec((tm, d), lambda i: (i, 0))
        aliases = {2: 0}
    n_tiles = t // tm
    return pl.pallas_call(
        functools.partial(_combine_kernel, n_ctx_tiles=n_ctx_tiles, final_norm=final_norm),
        out_shape=out_shape,
        grid=(n_tiles,),
        in_specs=[pl.BlockSpec((TOP_K * tm,), lambda i: (i,), memory_space=pltpu.SMEM),
                  pl.BlockSpec((TOP_K * tm,), lambda i: (jnp.minimum(i + 1, n_tiles - 1),),
                               memory_space=pltpu.SMEM),
                  pl.BlockSpec((tm, d), lambda i: (i, 0)),
                  pl.BlockSpec((tm, TOP_K), lambda i: (i, 0)),
                  pl.BlockSpec((None, 6, d), lambda i: (cond_of(i), 0, 0)),
                  pl.BlockSpec((1, d), lambda i: (0, 0)),
                  pl.BlockSpec(memory_space=pl.ANY)],
        out_specs=out_specs,
        scratch_shapes=[pltpu.VMEM((2, TOP_K, tm, ys.shape[1]), ys.dtype),
                        pltpu.SemaphoreType.DMA((2,))],
        input_output_aliases=aliases,
        compiler_params=_params(("arbitrary",), 40),
        name="moe_combine",
    )(dest.reshape(-1), dest.reshape(-1), x, gates, mod, g_final.reshape(1, d), ys)


def _outproj_moe(y_ctx, y_lat, w_out, b_out, x, mod, g, router_w, router_bias, w1, w3, w2, layer,
                 g_final, *, n_ctx_tiles, final_norm):
    t, d = x.shape
    ne = router_w.shape[1]
    x, h, idx, gates, rank, cnt = _outproj_router(y_ctx, y_lat, w_out, b_out, x, mod, g, router_w,
                                                  router_bias, n_ctx_tiles=n_ctx_tiles)
    counts = cnt[:, 0].astype(jnp.int32)
    padded = (counts + MOE_ROWS - 1) // MOE_ROWS * MOE_ROWS
    pad_end = jnp.cumsum(padded)
    pad_start = pad_end - padded
    e_ids = jnp.arange(ne, dtype=jnp.int32)
    start_of = jnp.sum(jnp.where(idx[None] == e_ids[:, None, None],
                                 pad_start[:, None, None], 0), axis=0)
    dest = (start_of + rank).T
    n_blocks = -(-(t * TOP_K) // MOE_ROWS) + ne
    block_start = jnp.arange(n_blocks, dtype=jnp.int32) * MOE_ROWS
    block_e = jnp.minimum(jnp.sum(block_start[:, None] >= pad_end[None, :], axis=1),
                          ne - 1).astype(jnp.int32)
    n_used = (pad_end[-1:] // MOE_ROWS).astype(jnp.int32)
    xs = _dispatch(h, dest, pad_start + counts, padded - counts, n_used, n_blocks * MOE_ROWS)
    ys = _experts(xs, block_e, pad_end, n_used, w1, w3, w2, layer)
    return _combine(x, ys, dest, gates.T, mod, g_final, n_ctx_tiles=n_ctx_tiles,
                    final_norm=final_norm)


def kernel(x_prompt, x_sample, state_rglru, c, c_ctx, ada_w, ada_b, norm_mix, norm_moe, norm_final, hy_w_in, hy_b_in, hy_conv_w, hy_conv_b, hy_f_w1, hy_f_b1, hy_f_freq, hy_f_w2, hy_f_b2, hy_f_w3, hy_f_skip, hy_w_out, hy_b_out, rg_w_in, rg_b_in, rg_conv_w, rg_conv_b, rg_wa, rg_ba, rg_wx, rg_bx, rg_lambda, rg_w_out, rg_b_out, router_w, router_bias, moe_w1, moe_w3, moe_w2):
    nb_ctx, len_ctx, d = x_prompt.shape
    nb_lat, len_lat, _ = x_sample.shape
    depth = ada_w.shape[0]
    n_rg = rg_w_in.shape[0]
    d_rnn = rg_w_out.shape[1]
    tok_ctx = nb_ctx * len_ctx
    tiles_ctx = tok_ctx // TOKEN_TILE
    assert tok_ctx % TOKEN_TILE == 0 and TOKEN_TILE % len_ctx == 0 and TOKEN_TILE % GRID_W == 0
    assert len_lat % TOKEN_TILE == 0 and tok_ctx % len_lat == 0
    assert V7X_SUBLANES - (nb_lat + 1) >= 0
    lat_off = tok_ctx // len_lat

    x = jnp.concatenate([x_prompt.reshape(tok_ctx, d), x_sample.reshape(nb_lat * len_lat, d)], 0)
    cond = jnp.concatenate([c_ctx[None, :], c,
                            jnp.zeros((V7X_SUBLANES - 1 - nb_lat, d), F32)], axis=0)
    mods = _ada_modulation(cond, ada_w, ada_b)[:, :1 + nb_lat]
    ctx_h0 = jnp.zeros((nb_ctx, 2, d_rnn), F32)
    states = []

    for i in range(depth):
        mod = mods[i]
        j = i // 2
        if i % 2 == 0:
            u = _inproj(x, mod, norm_mix[i], hy_w_in[j].astype(BF16), hy_b_in[j], hy_conv_w[j],
                        hy_conv_b[j], n_ctx_tiles=tiles_ctx, ctx_len=len_ctx, lat_len=GRID_W,
                        plain_cols=0)
            filt_w = (hy_f_w1[j], hy_f_b1[j], hy_f_freq[j], hy_f_w2[j], hy_f_b2[j], hy_f_w3[j])
            y_ctx = _hyena_stream(u, 0, nb_ctx, len_ctx, filt_w, hy_f_skip[j], d)
            y_lat = _hyena_stream(u, lat_off, nb_lat, len_lat, filt_w, hy_f_skip[j], d)
            w_out, b_out = hy_w_out[j].astype(BF16), hy_b_out[j]
        else:
            u = _inproj(x, mod, norm_mix[i], rg_w_in[j].astype(BF16), rg_b_in[j], rg_conv_w[j],
                        rg_conv_b[j], n_ctx_tiles=tiles_ctx, ctx_len=len_ctx, lat_len=GRID_W,
                        plain_cols=d_rnn)
            wa, wx = rg_wa[j].astype(BF16), rg_wx[j].astype(BF16)
            y_ctx, st = _rglru_stream(u, ctx_h0, 0, nb_ctx, len_ctx, wa, wx, rg_ba[j], rg_bx[j],
                                      rg_lambda[j])
            y_lat, _ = _rglru_stream(u, state_rglru[:, j], lat_off, nb_lat, len_lat, wa, wx,
                                     rg_ba[j], rg_bx[j], rg_lambda[j])
            states.append(st)
            w_out, b_out = rg_w_out[j].astype(BF16), rg_b_out[j]
        x = _outproj_moe(y_ctx, y_lat, w_out, b_out, x, mod, norm_moe[i], router_w, router_bias,
                         moe_w1, moe_w3, moe_w2, i, norm_final, n_ctx_tiles=tiles_ctx,
                         final_norm=(i == depth - 1))

    y_prompt = x[0].reshape(nb_ctx, len_ctx, d)
    y_sample = x[1].reshape(nb_lat, len_lat, d)
    new_state = jnp.stack(states, axis=1).astype(x_prompt.dtype)
    return (y_prompt, y_sample, new_state)
```

```python
import functools
import math

import numpy as np
import jax
import jax.numpy as jnp
from jax import lax
from jax.experimental import pallas as pl
from jax.experimental.pallas import tpu as pltpu

F32 = jnp.float32
BF16 = jnp.bfloat16

GRID_W = 64
FILTER_BANDS = 16
FILTER_EPS = 1e-6
MIN_DECAY = math.log(1e-2) / 0.3
MAX_DECAY = math.log(1e-2) / 1.5
RG_C = 8.0
N_GROUPS = 4
TOP_K = 2
NORM_EPS = 1e-6

V7X_LANES = 128
V7X_SUBLANES = 8
V7X_VMEM_BYTES = 64 * 1024 * 1024

TOKEN_TILE = 512
COL_TILE = 1024
MOE_ROWS = 256
DFT_ROWS = 512
DFT_FREQS = 512
DFT_SPLIT = 64
RG_COLS = 256
RG_ROWS = 1024
ROW_DMA_UNROLL = 16


def _params(sem, vmem_mb):
    return pltpu.CompilerParams(dimension_semantics=sem,
                                vmem_limit_bytes=vmem_mb * 1024 * 1024)


def _dot(a, b):
    return jnp.dot(a, b, preferred_element_type=F32)


def _split(x):
    hi = x.astype(BF16)
    lo = (x - hi.astype(F32)).astype(BF16)
    return hi, lo


def _dot3(a, b):
    ah, al = _split(a)
    bh, bl = _split(b)
    return _dot(ah, bh) + (_dot(ah, bl) + _dot(al, bh))


def _dot3_nt(a, b):
    dn = (((1,), (1,)), ((), ()))
    d = lambda x, y: lax.dot_general(x, y, dn, preferred_element_type=F32)
    ah, al = _split(a)
    bh, bl = _split(b)
    return d(ah, bh) + (d(ah, bl) + d(al, bh))


def _sigmoid(x):
    return 1.0 / (1.0 + jnp.exp(-x))


def _pack_rows(x):
    c = x.shape[1] // 2
    bits = lax.bitcast_convert_type(x.astype(BF16).astype(F32), jnp.uint32)
    return (bits[:, c:] & jnp.uint32(0xFFFF0000)) | (bits[:, :c] >> 16)


def _unpack_rows(p):
    lo = lax.bitcast_convert_type(p << 16, F32)
    hi = lax.bitcast_convert_type(p & jnp.uint32(0xFFFF0000), F32)
    return jnp.concatenate([lo, hi], axis=1)


def _sigmoid_tanh(x):
    return 0.5 * jnp.tanh(0.5 * x) + 0.5


def _norm_mod(x, g, shift, scale):
    ms = jnp.mean(x * x, axis=-1, keepdims=True)
    return (x * lax.rsqrt(ms + NORM_EPS) * g) * (1.0 + scale) + shift


def _ada_kernel(c_ref, w_ref, b_ref, o_ref):
    c = c_ref[...]
    o_ref[...] = _dot3(c * _sigmoid(c), w_ref[...]) + b_ref[...]


def _ada_modulation(cond, ada_w, ada_b):
    depth, d, n = ada_w.shape
    tn = n // 4
    out = pl.pallas_call(
        _ada_kernel,
        out_shape=jax.ShapeDtypeStruct((depth, cond.shape[0], n), F32),
        grid=(depth, n // tn),
        in_specs=[pl.BlockSpec(cond.shape, lambda i, j: (0, 0)),
                  pl.BlockSpec((None, d, tn), lambda i, j: (i, 0, j)),
                  pl.BlockSpec((None, 1, tn), lambda i, j: (i, 0, j))],
        out_specs=pl.BlockSpec((None, cond.shape[0], tn), lambda i, j: (i, 0, j)),
        compiler_params=_params(("parallel", "parallel"), 40),
        name="ada_modulation",
    )(cond, ada_w, ada_b.reshape(depth, 1, n))
    return out.reshape(depth, cond.shape[0], 6, d)


def _inproj_kernel(x_ref, mod_ref, g_ref, w_ref, b_ref, cw_ref, cb_ref, mk_ref, o_ref, h_scr,
                   u_scr, *, sub, width, plain_col_tiles, col_tiles):
    s = pl.program_id(0)
    last = pl.num_programs(0) - 1
    j_cur = lax.rem(jnp.minimum(s, last - 1), col_tiles)
    j_prev = lax.rem(jnp.maximum(s - 1, 0), col_tiles)

    @pl.when(s == 0)
    def _():
        u_scr[...] = jnp.zeros_like(u_scr)

    @pl.when((j_cur == 0) & (s < last))
    def _():
        h = _norm_mod(x_ref[...], g_ref[...], mod_ref[0:1, :], mod_ref[1:2, :])
        h_scr[...] = h.astype(BF16)

    blocks = [slice(r, r + sub) for r in range(0, x_ref.shape[0], sub)]
    pad_l = (width - 1) // 2

    def step(with_conv):
        new = [_dot(h_scr[rows, :], w_ref[...]) + b_ref[...] for rows in blocks]
        for rows in blocks:
            u = u_scr[rows, :]
            if with_conv:
                acc = cw_ref[pad_l:pad_l + 1, :] * u + cb_ref[...]
                m = 0
                for k in range(width):
                    off = k - pad_l
                    if off != 0:
                        shifted = pltpu.roll(u, (-off) % sub, axis=0)
                        acc = acc + cw_ref[k:k + 1, :] * (shifted * mk_ref[m])
                        m += 1
                u = acc
            o_ref[rows, :] = u
        for rows, u in zip(blocks, new):
            u_scr[rows, :] = u

    if plain_col_tiles == 0:
        step(True)
    else:
        pl.when(j_prev >= plain_col_tiles)(lambda: step(True))
        pl.when(j_prev < plain_col_tiles)(lambda: step(False))


def _inproj(x, mod, g, w, b, conv_w, conv_b, *, n_ctx_tiles, ctx_len, lat_len, plain_cols):
    t, d = x.shape
    n = w.shape[1]
    tm, tn = TOKEN_TILE, COL_TILE
    width = conv_w.shape[0]
    plain_tiles = plain_cols // tn
    lat_tiles = (t // tm - n_ctx_tiles) // (mod.shape[0] - 1)

    def cond_of(i):
        return jnp.where(i < n_ctx_tiles, 0, 1 + (i - n_ctx_tiles) // lat_tiles)

    sub = max(ctx_len, lat_len)
    assert sub % ctx_len == 0 and sub % lat_len == 0 and tm % sub == 0
    pad_l = (width - 1) // 2
    offs = jnp.asarray([k - pad_l for k in range(width) if k != pad_l], jnp.int32)
    seg = jnp.asarray([ctx_len, lat_len], jnp.int32)[:, None, None]
    pos = jnp.arange(sub, dtype=jnp.int32)[None, None, :] % seg + offs[None, :, None]
    masks = jnp.broadcast_to(((pos >= 0) & (pos < seg)).astype(F32)[..., None],
                             (2, width - 1, sub, tn))

    nj = n // tn
    n_tiles = (t // tm) * nj

    def cur(s):
        s = jnp.minimum(s, n_tiles - 1)
        return lax.div(s, nj), lax.rem(s, nj)

    def prev(s):
        s = jnp.maximum(s - 1, 0)
        return lax.div(s, nj), lax.rem(s, nj)

    conv_col = lambda s: jnp.maximum(prev(s)[1] - plain_tiles, 0)
    kern = functools.partial(_inproj_kernel, sub=sub, width=width, plain_col_tiles=plain_tiles,
                             col_tiles=nj)
    return pl.pallas_call(
        kern,
        out_shape=jax.ShapeDtypeStruct((t, n), F32),
        grid=(n_tiles + 1,),
        in_specs=[pl.BlockSpec((tm, d), lambda s: (cur(s)[0], 0)),
                  pl.BlockSpec((None, 6, d), lambda s: (cond_of(cur(s)[0]), 0, 0)),
                  pl.BlockSpec((1, d), lambda s: (0, 0)),
                  pl.BlockSpec((d, tn), lambda s: (0, cur(s)[1])),
                  pl.BlockSpec((1, tn), lambda s: (0, cur(s)[1])),
                  pl.BlockSpec((width, tn), lambda s: (0, conv_col(s))),
                  pl.BlockSpec((1, tn), lambda s: (0, conv_col(s))),
                  pl.BlockSpec((None, width - 1, sub, tn),
                               lambda s: (jnp.where(prev(s)[0] < n_ctx_tiles, 0, 1), 0, 0, 0))],
        out_specs=pl.BlockSpec((tm, tn), prev),
        scratch_shapes=[pltpu.VMEM((tm, d), BF16), pltpu.VMEM((tm, tn), F32)],
        compiler_params=_params(("arbitrary",), 48),
        name="norm_inproj_conv",
    )(x, mod, g.reshape(1, d), w, b.reshape(1, n), conv_w, conv_b.reshape(1, -1), masks)


def _filter_kernel(z_ref, w1_ref, b1_ref, fr_ref, w2_ref, b2_ref, w3a_ref, w3b_ref, w3c_ref,
                   w3d_ref, t_ref, dl_ref, o_ref, h_scr, *, orders):
    @pl.when(pl.program_id(0) == 0)
    def _():
        fr = fr_ref[...]
        h1 = jnp.sin(fr * (_dot3(z_ref[...], w1_ref[...]) + b1_ref[...]))
        h_scr[...] = jnp.sin(fr * (_dot3(h1, w2_ref[...]) + b2_ref[...]))

    h = h_scr[...]
    window = jnp.exp(-t_ref[...] * dl_ref[...])
    row = lax.broadcasted_iota(jnp.int32, (h.shape[0], 1), 0)
    w3 = ((w3a_ref, w3b_ref), (w3c_ref, w3d_ref))
    for o in range(orders):
        h_fwd = _dot3(h, w3[0][o][...]) * window
        h_bwd = jnp.where(row == 0, 0.0, _dot3(h, w3[1][o][...]) * window)
        norm = (jnp.sum(jnp.abs(h_fwd), axis=0, keepdims=True)
                + jnp.sum(jnp.abs(h_bwd), axis=0, keepdims=True) + FILTER_EPS)
        o_ref[2 * o] = h_fwd / norm
        o_ref[2 * o + 1] = h_bwd / norm


def _hyena_filter(length, f_w1, f_b1, f_freq, f_w2, f_b2, f_w3, d):
    emb, hid = f_w1.shape
    orders = f_w3.shape[1] // (2 * d)
    t = jnp.linspace(0.0, 1.0, length, dtype=F32)[:, None]
    w = (2.0 * math.pi / length) * jnp.arange(length, dtype=F32)[:, None]
    bands = jnp.linspace(1e-4, FILTER_BANDS - 1, FILTER_BANDS, dtype=F32)[None, :]
    z = jnp.concatenate([t, jnp.cos(bands * w), -jnp.sin(bands * w)], axis=-1)
    emb_pad = V7X_LANES
    z = jnp.pad(z, ((0, 0), (0, emb_pad - emb)))
    w1 = jnp.pad(f_w1, ((0, emb_pad - emb), (0, 0)))
    deltas = jnp.abs(jnp.linspace(MIN_DECAY, MAX_DECAY, d, dtype=F32))[None, :]
    td = 256
    nd = d // td
    full = lambda shape: pl.BlockSpec(shape, lambda j: (0,) * len(shape))
    w3_spec = lambda side, o: pl.BlockSpec((hid, td), lambda j: (0, (side * orders + o) * nd + j))
    assert orders == 2
    return pl.pallas_call(
        functools.partial(_filter_kernel, orders=orders),
        out_shape=jax.ShapeDtypeStruct((2 * orders, length, d), F32),
        grid=(nd,),
        in_specs=[full((length, emb_pad)), full((emb_pad, hid)), full((1, hid)), full((1, hid)),
                  full((hid, hid)), full((1, hid)),
                  w3_spec(0, 0), w3_spec(0, 1), w3_spec(1, 0), w3_spec(1, 1),
                  full((length, 1)), pl.BlockSpec((1, td), lambda j: (0, j))],
        out_specs=pl.BlockSpec((2 * orders, length, td), lambda j: (0, 0, j)),
        scratch_shapes=[pltpu.VMEM((length, hid), F32)],
        compiler_params=_params(("arbitrary",), 48),
        name="hyena_filter",
    )(z, w1, f_b1.reshape(1, hid), f_freq.reshape(1, hid), f_w2, f_b2.reshape(1, hid),
      f_w3, f_w3, f_w3, f_w3, t, deltas)


def _dft_matrices(length):
    n = 2 * length
    f0n, f1n = DFT_SPLIT, length // DFT_SPLIT
    tt = np.arange(length, dtype=np.int64)[None, :]
    ang_a = 2.0 * np.pi * ((DFT_SPLIT * np.arange(f1n, dtype=np.int64)[:, None] * tt) % n) / n
    ang_b = 2.0 * np.pi * ((np.arange(f0n, dtype=np.int64)[:, None] * tt) % n) / n
    ca, sa = (jnp.asarray(f(ang_a), F32)[:, None, :] for f in (np.cos, np.sin))
    cb, sb = (jnp.asarray(f(ang_b), F32)[None, :, :] for f in (np.cos, np.sin))
    cos_m = (ca * cb - sa * sb).reshape(length, length)
    sin_m = (sa * cb + ca * sb).reshape(length, length)
    nyq = jnp.asarray(1.0 - 2.0 * (np.arange(length) % 2), F32)[None, :]
    f_is0 = (jnp.arange(length) == 0)[:, None]
    fwd = jnp.stack([cos_m, jnp.where(f_is0, nyq, -sin_m)], axis=0)
    cat, sat = (jnp.asarray(f(ang_a).T, F32)[:, :, None] for f in (np.cos, np.sin))
    cbt, sbt = (jnp.asarray(f(ang_b).T, F32)[:, None, :] for f in (np.cos, np.sin))
    f_is0_t = f_is0.T
    scale = jnp.where(f_is0_t, 1.0 / n, 2.0 / n)
    cos_t = (cat * cbt - sat * sbt).reshape(length, length) * scale
    im_t = jnp.where(f_is0_t, nyq.T, -(sat * cbt + cat * sbt).reshape(length, length)) * scale
    inv = jnp.concatenate([cos_t, im_t], axis=1)
    return fwd.astype(BF16), inv.astype(BF16)


def _spec_kernel(a_ref, hf_ref, hb_ref, o_ref, sum_scr, dif_scr, nyq_scr):
    m = pl.program_id(2)

    @pl.when(m == 0)
    def _():
        hf, hb = hf_ref[...], hb_ref[...]
        sum_scr[...] = (hf + hb).astype(BF16)
        dif_scr[...] = (hf - hb).astype(BF16)
        t = lax.broadcasted_iota(jnp.int32, (hb.shape[0], 1), 0)
        sign = (1 - 2 * (t & 1)).astype(F32)
        nyq_scr[...] = 2.0 * jnp.sum(sign * hb, axis=0, keepdims=True)

    o_ref[0] = _dot(a_ref[0], sum_scr[...])
    im = _dot(a_ref[1], dif_scr[...])
    row = lax.broadcasted_iota(jnp.int32, (im.shape[0], 1), 0)
    o_ref[1] = jnp.where((row == 0) & (m == 0), im + nyq_scr[...], im)


def _filter_spectrum(fwd, filt):
    n2, length, d = filt.shape
    orders = n2 // 2
    th, tc = min(DFT_FREQS, length), 512
    filt2 = filt.reshape(n2 * length, d)
    return pl.pallas_call(
        _spec_kernel,
        out_shape=jax.ShapeDtypeStruct((orders, 2, length, d), F32),
        grid=(orders, d // tc, length // th),
        in_specs=[pl.BlockSpec((2, th, length), lambda o, c, m: (0, m, 0)),
                  pl.BlockSpec((length, tc), lambda o, c, m: (2 * o, c)),
                  pl.BlockSpec((length, tc), lambda o, c, m: (2 * o + 1, c))],
        out_specs=pl.BlockSpec((None, 2, th, tc), lambda o, c, m: (o, 0, m, c)),
        scratch_shapes=[pltpu.VMEM((length, tc), BF16), pltpu.VMEM((length, tc), BF16),
                        pltpu.VMEM((1, tc), F32)],
        compiler_params=_params(("parallel", "parallel", "arbitrary"), 48),
        name="hyena_filter_spectrum",
    )(fwd, filt2, filt2)


def _dft_fwd_kernel(a_ref, z_ref, k_ref, o_ref, z_scr):
    m = pl.program_id(1)

    @pl.when(m == 0)
    def _():
        z_scr[...] = z_ref[...].astype(BF16)

    xr = _dot(a_ref[0], z_scr[...])
    xi = _dot(a_ref[1], z_scr[...])
    kr, ki = k_ref[0], k_ref[1]
    row = lax.broadcasted_iota(jnp.int32, (xr.shape[0], 1), 0)
    dc = (row == 0) & (m == 0)
    o_ref[0] = (xr * kr - jnp.where(dc, 0.0, xi * ki)).astype(o_ref.dtype)
    o_ref[1] = jnp.where(dc, xi * ki, xr * ki + xi * kr).astype(o_ref.dtype)


def _dft_forward(fwd, src, spec, *, nb, length, row_off, col_blk, order):
    d = spec.shape[3]
    th = min(DFT_FREQS, length)
    return pl.pallas_call(
        _dft_fwd_kernel,
        out_shape=jax.ShapeDtypeStruct((nb, 2, length, d), BF16),
        grid=(nb, length // th),
        in_specs=[pl.BlockSpec((2, th, length), lambda b, m: (0, m, 0)),
                  pl.BlockSpec((length, d), lambda b, m: (row_off + b, col_blk)),
                  pl.BlockSpec((None, 2, th, d), lambda b, m: (order, 0, m, 0))],
        out_specs=pl.BlockSpec((None, 2, th, d), lambda b, m: (b, 0, m, 0)),
        scratch_shapes=[pltpu.VMEM((length, d), BF16)],
        compiler_params=_params(("parallel", "arbitrary"), 56),
        name="hyena_dft_forward",
    )(fwd, src, spec)


def _dft_inv_kernel(a_ref, p_ref, v_ref, g_ref, s_ref, o_ref):
    y = _dot(a_ref[...], p_ref[...])
    v = v_ref[...]
    o_ref[...] = (g_ref[...] * (y + v * s_ref[...])).astype(o_ref.dtype)


def _dft_inverse(inv, prod, vsrc, gsrc, skip, *, nb, length, v_off, v_col, g_off, g_col,
                 out_dtype):
    n = 2 * length
    d = prod.shape[3]
    prod = prod.reshape(nb * n, d)
    tm = min(DFT_ROWS, length)
    nt = length // tm
    return pl.pallas_call(
        _dft_inv_kernel,
        out_shape=jax.ShapeDtypeStruct((nb * length, d), out_dtype),
        grid=(nb, nt),
        in_specs=[pl.BlockSpec((tm, n), lambda b, m: (m, 0)),
                  pl.BlockSpec((n, d), lambda b, m: (b, 0)),
                  pl.BlockSpec((tm, d), lambda b, m: ((v_off + b) * nt + m, v_col)),
                  pl.BlockSpec((tm, d), lambda b, m: ((g_off + b) * nt + m, g_col)),
                  pl.BlockSpec((1, d), lambda b, m: (0, 0))],
        out_specs=pl.BlockSpec((tm, d), lambda b, m: (b * nt + m, 0)),
        compiler_params=_params(("parallel", "arbitrary"), 48),
        name="hyena_dft_inverse",
    )(inv, prod, vsrc, gsrc, skip.reshape(1, d))


def _spectral_product(acc, k_ref, order):
    th = acc.shape[0] // 2
    xr, xi = acc[:th], acc[th:]
    kr, ki = k_ref[order, 0:th, :], k_ref[order, th:, :]
    dc = lax.broadcasted_iota(jnp.int32, (th, 1), 0) == 0
    pr = xr * kr - jnp.where(dc, 0.0, xi * ki)
    pi = jnp.where(dc, xi * ki, xr * ki + xi * kr)
    return jnp.concatenate([pr, pi], axis=0).astype(BF16)


def _hyena_short_kernel(u_ref, f_ref, i_ref, k_ref, s_ref, o_ref):
    d = o_ref.shape[1]
    fwd, inv = f_ref[...], i_ref[...]
    z = u_ref[:, 0:d]
    for order in range(2):
        p = _spectral_product(_dot(fwd, z.astype(BF16)), k_ref, order)
        gate = u_ref[:, (order + 1) * d:(order + 2) * d]
        z = gate * (_dot(inv, p) + z * s_ref[order:order + 1, :])
    o_ref[...] = z.astype(o_ref.dtype)


def _hyena_short(u, fwd, inv, spec, f_skip, *, nb, length):
    n = 2 * length
    d = spec.shape[3]
    assert n == DFT_ROWS and spec.shape[:3] == (2, 2, length)
    fwd = fwd.reshape(n, length)
    return pl.pallas_call(
        _hyena_short_kernel,
        out_shape=jax.ShapeDtypeStruct((nb * length, d), BF16),
        grid=(nb,),
        in_specs=[pl.BlockSpec((length, 3 * d), lambda b: (b, 0)),
                  pl.BlockSpec((n, length), lambda b: (0, 0)),
                  pl.BlockSpec((length, n), lambda b: (0, 0)),
                  pl.BlockSpec((2, n, d), lambda b: (0, 0, 0)),
                  pl.BlockSpec((2, d), lambda b: (0, 0))],
        out_specs=pl.BlockSpec((length, d), lambda b: (b, 0)),
        compiler_params=_params(("parallel",), 48),
        name="hyena_short_sequences",
    )(u, fwd, inv, spec.reshape(2, n, d), f_skip)


def _hyena_stream(u, row_off, nb, length, filt_w, f_skip, d):
    fwd, inv = _dft_matrices(length)
    filt = _hyena_filter(length, *filt_w, d)
    spec = _filter_spectrum(fwd, filt)
    if 2 * length == DFT_ROWS:
        assert row_off == 0
        return _hyena_short(u, fwd, inv, spec, f_skip, nb=nb, length=length)
    p1 = _dft_forward(fwd, u, spec, nb=nb, length=length, row_off=row_off, col_blk=0, order=0)
    z1 = _dft_inverse(inv, p1, u, u, f_skip[0], nb=nb, length=length, v_off=row_off, v_col=0,
                      g_off=row_off, g_col=1, out_dtype=F32)
    p2 = _dft_forward(fwd, z1, spec, nb=nb, length=length, row_off=0, col_blk=0, order=1)
    return _dft_inverse(inv, p2, z1, u, f_skip[1], nb=nb, length=length, v_off=0, v_col=0,
                        g_off=row_off, g_col=2, out_dtype=BF16)


def _gelu_tanh(x):
    return 0.5 * x * (1.0 + jnp.tanh(math.sqrt(2.0 / math.pi) * (x + 0.044715 * (x * x * x))))


def _log1p(e):
    u = 1.0 + e
    d = u - 1.0
    return jnp.where(d == 0.0, e, jnp.log(u) * (e / jnp.where(d == 0.0, 1.0, d)))


def _rglru_kernel(gate_ref, rec_ref, wa_ref, wx_ref, ba_ref, bx_ref, lam_ref, h0_ref,
                  y_ref, st_ref, a_scr, b_scr):
    length, cols = rec_ref.shape
    groups = length // V7X_SUBLANES
    rec = rec_ref[...]
    rec16 = rec.astype(BF16)
    pos = lax.broadcasted_iota(jnp.int32, (groups, V7X_SUBLANES, cols), 1)

    for d in range(2):
        r_gate = _sigmoid_tanh(_dot(rec16, wa_ref[d]) + ba_ref[d])
        i_gate = _sigmoid_tanh(_dot(rec16, wx_ref[d]) + bx_ref[d])
        nlam = -lam_ref[d]
        softplus = jnp.maximum(nlam, 0.0) + _log1p(jnp.exp(-jnp.abs(nlam)))
        log_a = (-RG_C * softplus) * r_gate
        a = jnp.exp(log_a).reshape(groups, V7X_SUBLANES, cols)
        th = jnp.tanh(log_a)
        b = (jnp.sqrt(-2.0 * th / (1.0 - th)) * (i_gate * rec)).reshape(
            groups, V7X_SUBLANES, cols)
        for s in (1, 2, 4):
            if d == 0:
                a_sh = pltpu.roll(a, s, axis=1)
                b_sh = pltpu.roll(b, s, axis=1)
                live = pos >= s
            else:
                a_sh = pltpu.roll(a, V7X_SUBLANES - s, axis=1)
                b_sh = pltpu.roll(b, V7X_SUBLANES - s, axis=1)
                live = pos < V7X_SUBLANES - s
            b = jnp.where(live, a * b_sh, 0.0) + b
            a = jnp.where(live, a * a_sh, a)
        a_scr[d] = a.reshape(length, cols)
        b_scr[d] = b.reshape(length, cols)

    seqs = h0_ref.shape[0]
    seq_groups = groups // seqs

    def step(g, carry):
        out = []
        for q in range(seqs):
            cf, cb = carry[2 * q], carry[2 * q + 1]
            rf = pl.multiple_of((q * seq_groups + g) * V7X_SUBLANES, V7X_SUBLANES)
            rb = pl.multiple_of(((q + 1) * seq_groups - 1 - g) * V7X_SUBLANES, V7X_SUBLANES)
            hf = a_scr[0, pl.ds(rf, V7X_SUBLANES), :] * cf + b_scr[0, pl.ds(rf, V7X_SUBLANES), :]
            hb = a_scr[1, pl.ds(rb, V7X_SUBLANES), :] * cb + b_scr[1, pl.ds(rb, V7X_SUBLANES), :]
            b_scr[0, pl.ds(rf, V7X_SUBLANES), :] = hf
            b_scr[1, pl.ds(rb, V7X_SUBLANES), :] = hb
            out.append(jnp.broadcast_to(hf[V7X_SUBLANES - 1:V7X_SUBLANES, :], hf.shape))
            out.append(jnp.broadcast_to(hb[0:1, :], hb.shape))
        return tuple(out)

    init = tuple(jnp.broadcast_to(h0_ref[q, d], (V7X_SUBLANES, cols))
                 for q in range(seqs) for d in range(2))
    final = lax.fori_loop(0, seq_groups, step, init, unroll=max(1, 4 // seqs))
    for q in range(seqs):
        for d in range(2):
            st_ref[q, d] = final[2 * q + d][0:1, :]
    y_ref[...] = ((b_scr[0] + b_scr[1]) * _gelu_tanh(gate_ref[...])).astype(y_ref.dtype)


def _rglru_stream(u, h0, row_off, nb, length, wa, wx, ba, bx, lam):
    dr = u.shape[1] // 2
    tc = RG_COLS
    nc = dr // tc
    seqs = max(1, min(nb, RG_ROWS // length))
    assert nb % seqs == 0 and (row_off * length) % (seqs * length) == 0
    rows = seqs * length
    blk_off = row_off // seqs
    vec = lambda a: a.reshape(2, 1, dr)
    vec_spec = pl.BlockSpec((2, 1, tc), lambda b, c: (0, 0, c))
    w_spec = pl.BlockSpec((2, None, tc, tc), lambda b, c: (0, c, 0, 0))
    st_spec = pl.BlockSpec((seqs, 2, 1, tc), lambda b, c: (b, 0, 0, c))
    y, st = pl.pallas_call(
        _rglru_kernel,
        out_shape=(jax.ShapeDtypeStruct((nb * length, dr), BF16),
                   jax.ShapeDtypeStruct((nb, 2, 1, dr), F32)),
        grid=(nb // seqs, nc),
        in_specs=[pl.BlockSpec((rows, tc), lambda b, c: (blk_off + b, c)),
                  pl.BlockSpec((rows, tc), lambda b, c: (blk_off + b, nc + c)),
                  w_spec, w_spec, vec_spec, vec_spec, vec_spec, st_spec],
        out_specs=(pl.BlockSpec((rows, tc), lambda b, c: (b, c)), st_spec),
        scratch_shapes=[pltpu.VMEM((2, rows, tc), F32), pltpu.VMEM((2, rows, tc), F32)],
        compiler_params=_params(("parallel", "parallel"), 48),
        name="rglru_scan",
    )(u, u, wa, wx, vec(ba), vec(bx), vec(lam), h0.reshape(nb, 2, 1, dr))
    return y, st.reshape(nb, 2, dr)


def _router_kernel(yc_ref, yl_ref, w_ref, b_ref, x_ref, mod_ref, g_ref, rwt_ref, bias_ref,
                   xo_ref, h_ref, idx_ref, gate_ref, rank_ref, cnt_ref, tri_scr, carry_scr, *,
                   n_experts, n_ctx_tiles):
    i = pl.program_id(0)
    tm = x_ref.shape[0]
    per_group = n_experts // N_GROUPS

    @pl.when(i == 0)
    def _():
        r = lax.broadcasted_iota(jnp.int32, (tm, tm), 0)
        c = lax.broadcasted_iota(jnp.int32, (tm, tm), 1)
        tri_scr[...] = jnp.where(r < c, 1.0, 0.0).astype(BF16)
        carry_scr[...] = jnp.zeros_like(carry_scr)

    y = jnp.where(i < n_ctx_tiles, yc_ref[...], yl_ref[...])
    x = x_ref[...] + mod_ref[2:3, :] * (_dot(y, w_ref[...]) + b_ref[...])
    xo_ref[...] = x
    h = _norm_mod(x, g_ref[...], mod_ref[3:4, :], mod_ref[4:5, :])
    h_ref[...] = _pack_rows(h)
    logits = _dot3_nt(rwt_ref[...], h)
    p = jnp.exp(logits - jnp.max(logits, axis=0, keepdims=True))
    scores = p / jnp.sum(p, axis=0, keepdims=True)
    sel = scores + bias_ref[...]
    rows = [sel[e:e + 1, :] for e in range(n_experts)]

    best_val = None
    for gi in range(N_GROUPS):
        v = rows[gi * per_group:(gi + 1) * per_group]
        pair = None
        for a in range(per_group):
            for b in range(a + 1, per_group):
                s = v[a] + v[b]
                pair = s if pair is None else jnp.maximum(pair, s)
        if best_val is None:
            best_val, best_grp = pair, jnp.zeros_like(pair, dtype=jnp.int32)
        else:
            take = pair > best_val
            best_val = jnp.where(take, pair, best_val)
            best_grp = jnp.where(take, gi, best_grp)

    neg = jnp.float32(-jnp.inf)
    masked = [jnp.where(best_grp == e // per_group, rows[e], neg) for e in range(n_experts)]

    def argmax_first(vals):
        bv, bi = vals[0], jnp.zeros_like(best_grp)
        for e in range(1, n_experts):
            take = vals[e] > bv
            bv = jnp.where(take, vals[e], bv)
            bi = jnp.where(take, e, bi)
        return bi

    idx0 = argmax_first(masked)
    idx1 = argmax_first([jnp.where(idx0 == e, neg, masked[e]) for e in range(n_experts)])

    e_iota = lax.broadcasted_iota(jnp.int32, (n_experts, tm), 0)
    hit0 = e_iota == idx0
    hit1 = e_iota == idx1
    g0 = jnp.sum(jnp.where(hit0, scores, 0.0), axis=0, keepdims=True)
    g1 = jnp.sum(jnp.where(hit1, scores, 0.0), axis=0, keepdims=True)
    gsum = g0 + g1
    onehot = jnp.where(hit0 | hit1, 1.0, 0.0)
    before = _dot(onehot.astype(BF16), tri_scr[...]) + carry_scr[:, 0:1]
    r0 = jnp.sum(jnp.where(hit0, before, 0.0), axis=0, keepdims=True)
    r1 = jnp.sum(jnp.where(hit1, before, 0.0), axis=0, keepdims=True)
    idx_ref[0:1, :] = idx0
    idx_ref[1:2, :] = idx1
    gate_ref[0:1, :] = g0 / gsum
    gate_ref[1:2, :] = g1 / gsum
    rank_ref[0:1, :] = r0.astype(jnp.int32)
    rank_ref[1:2, :] = r1.astype(jnp.int32)
    carry_scr[...] = carry_scr[...] + jnp.sum(onehot, axis=1, keepdims=True)
    cnt_ref[...] = carry_scr[...]


def _outproj_router(y_ctx, y_lat, w_out, b_out, x, mod, g, router_w, router_bias, *,
                    n_ctx_tiles):
    t, d = x.shape
    ne = router_w.shape[1]
    tm = TOKEN_TILE
    lat_tiles = (t // tm - n_ctx_tiles) // (mod.shape[0] - 1)

    def cond_of(i):
        return jnp.where(i < n_ctx_tiles, 0, 1 + (i - n_ctx_tiles) // lat_tiles)

    row2 = pl.BlockSpec((TOP_K, tm), lambda i: (0, i))
    full = lambda shape: pl.BlockSpec(shape, lambda i: (0,) * len(shape))
    return pl.pallas_call(
        functools.partial(_router_kernel, n_experts=ne, n_ctx_tiles=n_ctx_tiles),
        out_shape=(jax.ShapeDtypeStruct((t, d), F32),
                   jax.ShapeDtypeStruct((t, d // 2), jnp.uint32),
                   jax.ShapeDtypeStruct((TOP_K, t), jnp.int32),
                   jax.ShapeDtypeStruct((TOP_K, t), F32),
                   jax.ShapeDtypeStruct((TOP_K, t), jnp.int32),
                   jax.ShapeDtypeStruct((ne, V7X_LANES), F32)),
        grid=(t // tm,),
        in_specs=[pl.BlockSpec((tm, w_out.shape[0]),
                               lambda i: (jnp.minimum(i, n_ctx_tiles - 1), 0)),
                  pl.BlockSpec((tm, w_out.shape[0]),
                               lambda i: (jnp.maximum(i - n_ctx_tiles, 0), 0)),
                  full(w_out.shape), full((1, d)),
                  pl.BlockSpec((tm, d), lambda i: (i, 0)),
                  pl.BlockSpec((None, 6, d), lambda i: (cond_of(i), 0, 0)),
                  full((1, d)), full((ne, d)), full((ne, 1))],
        out_specs=(pl.BlockSpec((tm, d), lambda i: (i, 0)),
                   pl.BlockSpec((tm, d // 2), lambda i: (i, 0)), row2, row2, row2,
                   full((ne, V7X_LANES))),
        scratch_shapes=[pltpu.VMEM((tm, tm), BF16), pltpu.VMEM((ne, V7X_LANES), F32)],
        input_output_aliases={4: 0},
        compiler_params=_params(("arbitrary",), 40),
        name="outproj_router",
    )(y_ctx, y_lat, w_out, b_out.reshape(1, d), x, mod, g.reshape(1, d), router_w.T,
      router_bias.reshape(ne, 1))


def _row_copy(src, src_row, dst, dst_row, sem):
    return pltpu.make_async_copy(src.at[pl.ds(src_row, 1)], dst.at[pl.ds(dst_row, 1)], sem)


def _dispatch_kernel(pad_row_ref, pad_len_ref, used_ref, dest_ref, h_ref, xs_out, zero_scr, sem,
                     pad_sem):
    tm = h_ref.shape[0]
    i = pl.program_id(0)
    n_experts = pad_row_ref.shape[0]
    zrows = zero_scr.shape[0]

    @pl.when(i == 0)
    def _():
        zero_scr[...] = jnp.zeros_like(zero_scr)

    def zero_copy(wanted, start, size):
        start = jnp.where(wanted, start, 0)
        if size >= V7X_SUBLANES:
            start = pl.multiple_of(start, V7X_SUBLANES)
        return wanted, pltpu.make_async_copy(zero_scr.at[pl.ds(0, size)],
                                             xs_out.at[pl.ds(start, size)], pad_sem)

    e = jnp.minimum(i, n_experts - 1)
    pad_len = jnp.where(i < n_experts, pad_len_ref[e], 0)
    pad_row = pad_row_ref[e]
    head = pad_len & (V7X_SUBLANES - 1)
    pad_copies = [zero_copy(r < head, pad_row + r, 1) for r in range(V7X_SUBLANES - 1)]
    size = zrows
    while size >= V7X_SUBLANES:
        start = pad_row + head + ((pad_len - head) & ~(2 * size - 1))
        pad_copies.append(zero_copy((pad_len & size) != 0, start, size))
        size //= 2
    spare = used_ref[0] + i
    has_spare = (i < n_experts) & (spare < xs_out.shape[0] // MOE_ROWS)
    for part in range(MOE_ROWS // zrows):
        pad_copies.append(zero_copy(has_spare, spare * MOE_ROWS + part * zrows, zrows))
    for wanted, copy in pad_copies:
        pl.when(wanted)(copy.start)

    def issue(g, c):
        for r in range(ROW_DMA_UNROLL):
            t = g * ROW_DMA_UNROLL + r
            for k in range(TOP_K):
                _row_copy(h_ref, t, xs_out, dest_ref[TOP_K * t + k], sem).start(priority=k)
        return c

    def drain(g, c):
        for _ in range(ROW_DMA_UNROLL * TOP_K):
            _row_copy(h_ref, 0, xs_out, 0, sem).wait()
        return c

    lax.fori_loop(0, tm // ROW_DMA_UNROLL, issue, 0)
    lax.fori_loop(0, tm // ROW_DMA_UNROLL, drain, 0)
    for wanted, copy in pad_copies:
        pl.when(wanted)(copy.wait)


def _dispatch(h, dest, pad_row, pad_len, n_used, n_slots):
    t, d = h.shape
    tm = TOKEN_TILE
    assert t // tm >= pad_row.shape[0] and n_slots % MOE_ROWS == 0
    return pl.pallas_call(
        _dispatch_kernel,
        out_shape=jax.ShapeDtypeStruct((n_slots, d), h.dtype),
        grid_spec=pltpu.PrefetchScalarGridSpec(
            num_scalar_prefetch=3,
            grid=(t // tm,),
            in_specs=[pl.BlockSpec((TOP_K * tm,), lambda i, pr, pn, nu: (i,),
                                   memory_space=pltpu.SMEM),
                      pl.BlockSpec((tm, d), lambda i, pr, pn, nu: (i, 0))],
            out_specs=pl.BlockSpec(memory_space=pl.ANY),
            scratch_shapes=[pltpu.VMEM((MOE_ROWS // 2, d), h.dtype),
                            pltpu.SemaphoreType.DMA, pltpu.SemaphoreType.DMA]),
        compiler_params=_params(("arbitrary",), 32),
        name="moe_dispatch",
    )(pad_row, pad_len, n_used, dest.reshape(-1), h)


def _experts_kernel(be_ref, first_ref, slot_ref, next_ref, nu_ref, xs_ref, w1_hbm, w3_hbm, w2_hbm,
                    ys_ref, wbuf, w1_s, w3_s, w2_s, sem, *, layer):
    i = pl.program_id(0)
    used = i < nu_ref[0]
    w_hbm = (w1_hbm, w3_hbm, w2_hbm)
    w_s = (w1_s, w3_s, w2_s)

    def weight_copies(expert, slot):
        return [pltpu.make_async_copy(w_hbm[m].at[layer, expert], wbuf.at[slot, m],
                                      sem.at[slot, m]) for m in range(3)]

    @pl.when(i == 0)
    def _():
        for copy in weight_copies(be_ref[0], slot_ref[0]):
            copy.start()

    @pl.when(used & (first_ref[i] != 0))
    def _():
        slot = slot_ref[i]
        for copy in weight_copies(be_ref[i], slot):
            copy.wait()
        for static_slot in range(2):

            @pl.when(slot == static_slot)
            def _():
                for m in range(3):
                    w_s[m][...] = wbuf[static_slot, m].astype(BF16)

        nxt = next_ref[i]

        @pl.when(nxt >= 0)
        def _():
            for copy in weight_copies(nxt, 1 - slot):
                copy.start()

    @pl.when(used)
    def _():
        x = _unpack_rows(xs_ref[...]).astype(BF16)
        h1 = _dot(x, w1_s[...])
        h3 = _dot(x, w3_s[...])
        act = (h1 * _sigmoid(h1)) * h3
        ys_ref[...] = _pack_rows(_dot(act.astype(BF16), w2_s[...]))

    @pl.when(i >= nu_ref[0])
    def _():
        ys_ref[...] = jnp.zeros_like(ys_ref)


def _experts(xs, block_e, pad_end, n_used, w1, w3, w2, layer):
    n_slots, dp = xs.shape
    d, de = w1.shape[2:]
    assert d == de == 2 * dp
    tm = MOE_ROWS
    n_blocks = n_slots // tm
    prev_e = jnp.concatenate([jnp.full((1,), -1, jnp.int32), block_e[:-1]])
    first = (block_e != prev_e).astype(jnp.int32)
    slot = (jnp.cumsum(first) - 1) & 1
    run_end = pad_end[block_e] // tm
    nxt = jnp.where(run_end < n_used[0], block_e[jnp.minimum(run_end, n_blocks - 1)], -1)
    any_spec = pl.BlockSpec(memory_space=pl.ANY)
    row_spec = lambda f: pl.BlockSpec((tm, dp), lambda i, be, fi, sl, nx, nu: (f(i, nu), 0))
    return pl.pallas_call(
        functools.partial(_experts_kernel, layer=layer),
        out_shape=jax.ShapeDtypeStruct((n_slots, dp), jnp.uint32),
        grid_spec=pltpu.PrefetchScalarGridSpec(
            num_scalar_prefetch=5,
            grid=(n_blocks,),
            in_specs=[row_spec(lambda i, nu: jnp.minimum(i, nu[0] - 1)),
                      any_spec, any_spec, any_spec],
            out_specs=row_spec(lambda i, nu: i),
            scratch_shapes=[pltpu.VMEM((2, 3, d, de), F32),
                            pltpu.VMEM((d, de), BF16), pltpu.VMEM((d, de), BF16),
                            pltpu.VMEM((de, d), BF16), pltpu.SemaphoreType.DMA((2, 3))]),
        compiler_params=_params(("arbitrary",), 52),
        name="moe_experts",
    )(block_e, first, slot.astype(jnp.int32), nxt.astype(jnp.int32), n_used, xs, w1, w3, w2)


def _combine_kernel(dest_ref, next_ref, x_ref, gate_ref, mod_ref, gf_ref, ys_hbm, *rest,
                    n_ctx_tiles, final_norm):
    buf, sem = rest[-2:]
    tm = x_ref.shape[0]
    i = pl.program_id(0)
    slot = i & 1

    def gather(idx_ref, into):
        def issue(g, c):
            for r in range(ROW_DMA_UNROLL):
                t = g * ROW_DMA_UNROLL + r
                for k in range(TOP_K):
                    _row_copy(ys_hbm, idx_ref[TOP_K * t + k], buf.at[into, k], t,
                              sem.at[into]).start(priority=k)
            return c
        lax.fori_loop(0, tm // ROW_DMA_UNROLL, issue, 0)

    pl.when(i == 0)(lambda: gather(dest_ref, slot))
    pl.when(i + 1 < pl.num_programs(0))(lambda: gather(next_ref, 1 - slot))

    def drain(g, c):
        for _ in range(ROW_DMA_UNROLL * TOP_K):
            _row_copy(ys_hbm, 0, buf.at[slot, 0], 0, sem.at[slot]).wait()
        return c

    lax.fori_loop(0, tm // ROW_DMA_UNROLL, drain, 0)
    gate = gate_ref[...]
    m = (gate[:, 0:1] * _unpack_rows(buf[slot, 0])
         + gate[:, 1:2] * _unpack_rows(buf[slot, 1]))
    x = x_ref[...] + mod_ref[5:6, :] * m
    if not final_norm:
        rest[0][...] = x
        return
    ms = jnp.mean(x * x, axis=-1, keepdims=True)
    x = x * lax.rsqrt(ms + NORM_EPS) * gf_ref[...]
    @pl.when(i < n_ctx_tiles)
    def _():
        rest[0][...] = x

    @pl.when(i >= n_ctx_tiles)
    def _():
        rest[1][...] = x


def _combine(x, ys, dest, gates, mod, g_final, *, n_ctx_tiles, final_norm):
    t, d = x.shape
    tm = TOKEN_TILE
    lat_tiles = (t // tm - n_ctx_tiles) // (mod.shape[0] - 1)

    def cond_of(i):
        return jnp.where(i < n_ctx_tiles, 0, 1 + (i - n_ctx_tiles) // lat_tiles)

    if final_norm:
        out_shape = (jax.ShapeDtypeStruct((n_ctx_tiles * tm, d), F32),
                     jax.ShapeDtypeStruct((t - n_ctx_tiles * tm, d), F32))
        out_specs = (pl.BlockSpec((tm, d), lambda i: (jnp.minimum(i, n_ctx_tiles - 1), 0)),
                     pl.BlockSpec((tm, d), lambda i: (jnp.maximum(i - n_ctx_tiles, 0), 0)))
        aliases = {}
    else:
        out_shape = jax.ShapeDtypeStruct((t, d), F32)
        out_specs = pl.BlockSpec((tm, d), lambda i: (i, 0))
        aliases = {2: 0}
    n_tiles = t // tm
    return pl.pallas_call(
        functools.partial(_combine_kernel, n_ctx_tiles=n_ctx_tiles, final_norm=final_norm),
        out_shape=out_shape,
        grid=(n_tiles,),
        in_specs=[pl.BlockSpec((TOP_K * tm,), lambda i: (i,), memory_space=pltpu.SMEM),
                  pl.BlockSpec((TOP_K * tm,), lambda i: (jnp.minimum(i + 1, n_tiles - 1),),
                               memory_space=pltpu.SMEM),
                  pl.BlockSpec((tm, d), lambda i: (i, 0)),
                  pl.BlockSpec((tm, TOP_K), lambda i: (i, 0)),
                  pl.BlockSpec((None, 6, d), lambda i: (cond_of(i), 0, 0)),
                  pl.BlockSpec((1, d), lambda i: (0, 0)),
                  pl.BlockSpec(memory_space=pl.ANY)],
        out_specs=out_specs,
        scratch_shapes=[pltpu.VMEM((2, TOP_K, tm, ys.shape[1]), ys.dtype),
                        pltpu.SemaphoreType.DMA((2,))],
        input_output_aliases=aliases,
        compiler_params=_params(("arbitrary",), 40),
        name="moe_combine",
    )(dest.reshape(-1), dest.reshape(-1), x, gates, mod, g_final.reshape(1, d), ys)


def _outproj_moe(y_ctx, y_lat, w_out, b_out, x, mod, g, router_w, router_bias, w1, w3, w2, layer,
                 g_final, *, n_ctx_tiles, final_norm):
    t, d = x.shape
    ne = router_w.shape[1]
    x, h, idx, gates, rank, cnt = _outproj_router(y_ctx, y_lat, w_out, b_out, x, mod, g, router_w,
                                                  router_bias, n_ctx_tiles=n_ctx_tiles)
    counts = cnt[:, 0].astype(jnp.int32)
    padded = (counts + MOE_ROWS - 1) // MOE_ROWS * MOE_ROWS
    pad_end = jnp.cumsum(padded)
    pad_start = pad_end - padded
    e_ids = jnp.arange(ne, dtype=jnp.int32)
    start_of = jnp.sum(jnp.where(idx[None] == e_ids[:, None, None],
                                 pad_start[:, None, None], 0), axis=0)
    dest = (start_of + rank).T
    n_blocks = -(-(t * TOP_K) // MOE_ROWS) + ne
    block_start = jnp.arange(n_blocks, dtype=jnp.int32) * MOE_ROWS
    block_e = jnp.minimum(jnp.sum(block_start[:, None] >= pad_end[None, :], axis=1),
                          ne - 1).astype(jnp.int32)
    n_used = (pad_end[-1:] // MOE_ROWS).astype(jnp.int32)
    xs = _dispatch(h, dest, pad_start + counts, padded - counts, n_used, n_blocks * MOE_ROWS)
    ys = _experts(xs, block_e, pad_end, n_used, w1, w3, w2, layer)
    return _combine(x, ys, dest, gates.T, mod, g_final, n_ctx_tiles=n_ctx_tiles,
                    final_norm=final_norm)


def kernel(x_prompt, x_sample, state_rglru, c, c_ctx, ada_w, ada_b, norm_mix, norm_moe, norm_final, hy_w_in, hy_b_in, hy_conv_w, hy_conv_b, hy_f_w1, hy_f_b1, hy_f_freq, hy_f_w2, hy_f_b2, hy_f_w3, hy_f_skip, hy_w_out, hy_b_out, rg_w_in, rg_b_in, rg_conv_w, rg_conv_b, rg_wa, rg_ba, rg_wx, rg_bx, rg_lambda, rg_w_out, rg_b_out, router_w, router_bias, moe_w1, moe_w3, moe_w2):
    nb_ctx, len_ctx, d = x_prompt.shape
    nb_lat, len_lat, _ = x_sample.shape
    depth = ada_w.shape[0]
    n_rg = rg_w_in.shape[0]
    d_rnn = rg_w_out.shape[1]
    tok_ctx = nb_ctx * len_ctx
    tiles_ctx = tok_ctx // TOKEN_TILE
    assert tok_ctx % TOKEN_TILE == 0 and TOKEN_TILE % len_ctx == 0 and TOKEN_TILE % GRID_W == 0
    assert len_lat % TOKEN_TILE == 0 and tok_ctx % len_lat == 0
    assert V7X_SUBLANES - (nb_lat + 1) >= 0
    lat_off = tok_ctx // len_lat

    x = jnp.concatenate([x_prompt.reshape(tok_ctx, d), x_sample.reshape(nb_lat * len_lat, d)], 0)
    cond = jnp.concatenate([c_ctx[None, :], c,
                            jnp.zeros((V7X_SUBLANES - 1 - nb_lat, d), F32)], axis=0)
    mods = _ada_modulation(cond, ada_w, ada_b)[:, :1 + nb_lat]
    ctx_h0 = jnp.zeros((nb_ctx, 2, d_rnn), F32)
    states = []

    for i in range(depth):
        mod = mods[i]
        j = i // 2
        if i % 2 == 0:
            u = _inproj(x, mod, norm_mix[i], hy_w_in[j].astype(BF16), hy_b_in[j], hy_conv_w[j],
                        hy_conv_b[j], n_ctx_tiles=tiles_ctx, ctx_len=len_ctx, lat_len=GRID_W,
                        plain_cols=0)
            filt_w = (hy_f_w1[j], hy_f_b1[j], hy_f_freq[j], hy_f_w2[j], hy_f_b2[j], hy_f_w3[j])
            y_ctx = _hyena_stream(u, 0, nb_ctx, len_ctx, filt_w, hy_f_skip[j], d)
            y_lat = _hyena_stream(u, lat_off, nb_lat, len_lat, filt_w, hy_f_skip[j], d)
            w_out, b_out = hy_w_out[j].astype(BF16), hy_b_out[j]
        else:
            u = _inproj(x, mod, norm_mix[i], rg_w_in[j].astype(BF16), rg_b_in[j], rg_conv_w[j],
                        rg_conv_b[j], n_ctx_tiles=tiles_ctx, ctx_len=len_ctx, lat_len=GRID_W,
                        plain_cols=d_rnn)
            wa, wx = rg_wa[j].astype(BF16), rg_wx[j].astype(BF16)
            y_ctx, st = _rglru_stream(u, ctx_h0, 0, nb_ctx, len_ctx, wa, wx, rg_ba[j], rg_bx[j],
                                      rg_lambda[j])
            y_lat, _ = _rglru_stream(u, state_rglru[:, j], lat_off, nb_lat, len_lat, wa, wx,
                                     rg_ba[j], rg_bx[j], rg_lambda[j])
            states.append(st)
            w_out, b_out = rg_w_out[j].astype(BF16), rg_b_out[j]
        x = _outproj_moe(y_ctx, y_lat, w_out, b_out, x, mod, norm_moe[i], router_w, router_bias,
                         moe_w1, moe_w3, moe_w2, i, norm_final, n_ctx_tiles=tiles_ctx,
                         final_norm=(i == depth - 1))

    y_prompt = x[0].reshape(nb_ctx, len_ctx, d)
    y_sample = x[1].reshape(nb_lat, len_lat, d)
    new_state = jnp.stack(states, axis=1).astype(x_prompt.dtype)
    return (y_prompt, y_sample, new_state)
```

```python
import functools
import math

import numpy as np
import jax
import jax.numpy as jnp
from jax import lax
from jax.experimental import pallas as pl
from jax.experimental.pallas import tpu as pltpu

F32 = jnp.float32
BF16 = jnp.bfloat16

GRID_W = 64
FILTER_BANDS = 16
FILTER_EPS = 1e-6
MIN_DECAY = math.log(1e-2) / 0.3
MAX_DECAY = math.log(1e-2) / 1.5
RG_C = 8.0
N_GROUPS = 4
TOP_K = 2
NORM_EPS = 1e-6

V7X_LANES = 128
V7X_SUBLANES = 8
V7X_VMEM_BYTES = 64 * 1024 * 1024

TOKEN_TILE = 512
COL_TILE = 1024
MOE_ROWS = 512
DFT_ROWS = 512
DFT_FREQS = 512
DFT_SPLIT = 64
RG_COLS = 256
RG_ROWS = 1024
ROW_DMA_UNROLL = 16


def _params(sem, vmem_mb):
    return pltpu.CompilerParams(dimension_semantics=sem,
                                vmem_limit_bytes=vmem_mb * 1024 * 1024)


def _dot(a, b):
    return jnp.dot(a, b, preferred_element_type=F32)


def _split(x):
    hi = x.astype(BF16)
    lo = (x - hi.astype(F32)).astype(BF16)
    return hi, lo


def _dot3(a, b):
    ah, al = _split(a)
    bh, bl = _split(b)
    return _dot(ah, bh) + (_dot(ah, bl) + _dot(al, bh))


def _dot3_nt(a, b):
    dn = (((1,), (1,)), ((), ()))
    d = lambda x, y: lax.dot_general(x, y, dn, preferred_element_type=F32)
    ah, al = _split(a)
    bh, bl = _split(b)
    return d(ah, bh) + (d(ah, bl) + d(al, bh))


def _sigmoid(x):
    return 1.0 / (1.0 + jnp.exp(-x))


def _pack_rows(x):
    c = x.shape[1] // 2
    bits = lax.bitcast_convert_type(x.astype(BF16).astype(F32), jnp.uint32)
    return (bits[:, c:] & jnp.uint32(0xFFFF0000)) | (bits[:, :c] >> 16)


def _unpack_rows(p):
    lo = lax.bitcast_convert_type(p << 16, F32)
    hi = lax.bitcast_convert_type(p & jnp.uint32(0xFFFF0000), F32)
    return jnp.concatenate([lo, hi], axis=1)


def _sigmoid_tanh(x):
    return 0.5 * jnp.tanh(0.5 * x) + 0.5


def _norm_mod(x, g, shift, scale):
    ms = jnp.mean(x * x, axis=-1, keepdims=True)
    return (x * lax.rsqrt(ms + NORM_EPS) * g) * (1.0 + scale) + shift


def _ada_kernel(c_ref, w_ref, b_ref, o_ref):
    c = c_ref[...]
    o_ref[...] = _dot3(c * _sigmoid(c), w_ref[...]) + b_ref[...]


def _ada_modulation(cond, ada_w, ada_b):
    depth, d, n = ada_w.shape
    tn = n // 4
    out = pl.pallas_call(
        _ada_kernel,
        out_shape=jax.ShapeDtypeStruct((depth, cond.shape[0], n), F32),
        grid=(depth, n // tn),
        in_specs=[pl.BlockSpec(cond.shape, lambda i, j: (0, 0)),
                  pl.BlockSpec((None, d, tn), lambda i, j: (i, 0, j)),
                  pl.BlockSpec((None, 1, tn), lambda i, j: (i, 0, j))],
        out_specs=pl.BlockSpec((None, cond.shape[0], tn), lambda i, j: (i, 0, j)),
        compiler_params=_params(("parallel", "parallel"), 40),
        name="ada_modulation",
    )(cond, ada_w, ada_b.reshape(depth, 1, n))
    return out.reshape(depth, cond.shape[0], 6, d)


def _inproj_kernel(x_ref, mod_ref, g_ref, w_ref, b_ref, cw_ref, cb_ref, mk_ref, o_ref, h_scr,
                   u_scr, *, sub, width, plain_col_tiles, col_tiles):
    s = pl.program_id(0)
    last = pl.num_programs(0) - 1
    j_cur = lax.rem(jnp.minimum(s, last - 1), col_tiles)
    j_prev = lax.rem(jnp.maximum(s - 1, 0), col_tiles)

    @pl.when(s == 0)
    def _():
        u_scr[...] = jnp.zeros_like(u_scr)

    @pl.when((j_cur == 0) & (s < last))
    def _():
        h = _norm_mod(x_ref[...], g_ref[...], mod_ref[0:1, :], mod_ref[1:2, :])
        h_scr[...] = h.astype(BF16)

    blocks = [slice(r, r + sub) for r in range(0, x_ref.shape[0], sub)]
    pad_l = (width - 1) // 2

    def step(with_conv):
        new = [_dot(h_scr[rows, :], w_ref[...]) + b_ref[...] for rows in blocks]
        for rows in blocks:
            u = u_scr[rows, :]
            if with_conv:
                acc = cw_ref[pad_l:pad_l + 1, :] * u + cb_ref[...]
                m = 0
                for k in range(width):
                    off = k - pad_l
                    if off != 0:
                        shifted = pltpu.roll(u, (-off) % sub, axis=0)
                        acc = acc + cw_ref[k:k + 1, :] * (shifted * mk_ref[m])
                        m += 1
                u = acc
            o_ref[rows, :] = u
        for rows, u in zip(blocks, new):
            u_scr[rows, :] = u

    if plain_col_tiles == 0:
        step(True)
    else:
        pl.when(j_prev >= plain_col_tiles)(lambda: step(True))
        pl.when(j_prev < plain_col_tiles)(lambda: step(False))


def _inproj(x, mod, g, w, b, conv_w, conv_b, *, n_ctx_tiles, ctx_len, lat_len, plain_cols):
    t, d = x.shape
    n = w.shape[1]
    tm, tn = TOKEN_TILE, COL_TILE
    width = conv_w.shape[0]
    plain_tiles = plain_cols // tn
    lat_tiles = (t // tm - n_ctx_tiles) // (mod.shape[0] - 1)

    def cond_of(i):
        return jnp.where(i < n_ctx_tiles, 0, 1 + (i - n_ctx_tiles) // lat_tiles)

    sub = max(ctx_len, lat_len)
    assert sub % ctx_len == 0 and sub % lat_len == 0 and tm % sub == 0
    pad_l = (width - 1) // 2
    offs = jnp.asarray([k - pad_l for k in range(width) if k != pad_l], jnp.int32)
    seg = jnp.asarray([ctx_len, lat_len], jnp.int32)[:, None, None]
    pos = jnp.arange(sub, dtype=jnp.int32)[None, None, :] % seg + offs[None, :, None]
    masks = jnp.broadcast_to(((pos >= 0) & (pos < seg)).astype(F32)[..., None],
                             (2, width - 1, sub, tn))

    nj = n // tn
    n_tiles = (t // tm) * nj

    def cur(s):
        s = jnp.minimum(s, n_tiles - 1)
        return lax.div(s, nj), lax.rem(s, nj)

    def prev(s):
        s = jnp.maximum(s - 1, 0)
        return lax.div(s, nj), lax.rem(s, nj)

    conv_col = lambda s: jnp.maximum(prev(s)[1] - plain_tiles, 0)
    kern = functools.partial(_inproj_kernel, sub=sub, width=width, plain_col_tiles=plain_tiles,
                             col_tiles=nj)
    return pl.pallas_call(
        kern,
        out_shape=jax.ShapeDtypeStruct((t, n), F32),
        grid=(n_tiles + 1,),
        in_specs=[pl.BlockSpec((tm, d), lambda s: (cur(s)[0], 0)),
                  pl.BlockSpec((None, 6, d), lambda s: (cond_of(cur(s)[0]), 0, 0)),
                  pl.BlockSpec((1, d), lambda s: (0, 0)),
                  pl.BlockSpec((d, tn), lambda s: (0, cur(s)[1])),
                  pl.BlockSpec((1, tn), lambda s: (0, cur(s)[1])),
                  pl.BlockSpec((width, tn), lambda s: (0, conv_col(s))),
                  pl.BlockSpec((1, tn), lambda s: (0, conv_col(s))),
                  pl.BlockSpec((None, width - 1, sub, tn),
                               lambda s: (jnp.where(prev(s)[0] < n_ctx_tiles, 0, 1), 0, 0, 0))],
        out_specs=pl.BlockSpec((tm, tn), prev),
        scratch_shapes=[pltpu.VMEM((tm, d), BF16), pltpu.VMEM((tm, tn), F32)],
        compiler_params=_params(("arbitrary",), 48),
        name="norm_inproj_conv",
    )(x, mod, g.reshape(1, d), w, b.reshape(1, n), conv_w, conv_b.reshape(1, -1), masks)


def _filter_kernel(z_ref, w1_ref, b1_ref, fr_ref, w2_ref, b2_ref, w3a_ref, w3b_ref, w3c_ref,
                   w3d_ref, t_ref, dl_ref, o_ref, h_scr, *, orders):
    @pl.when(pl.program_id(0) == 0)
    def _():
        fr = fr_ref[...]
        h1 = jnp.sin(fr * (_dot3(z_ref[...], w1_ref[...]) + b1_ref[...]))
        h_scr[...] = jnp.sin(fr * (_dot3(h1, w2_ref[...]) + b2_ref[...]))

    h = h_scr[...]
    window = jnp.exp(-t_ref[...] * dl_ref[...])
    row = lax.broadcasted_iota(jnp.int32, (h.shape[0], 1), 0)
    w3 = ((w3a_ref, w3b_ref), (w3c_ref, w3d_ref))
    for o in range(orders):
        h_fwd = _dot3(h, w3[0][o][...]) * window
        h_bwd = jnp.where(row == 0, 0.0, _dot3(h, w3[1][o][...]) * window)
        norm = (jnp.sum(jnp.abs(h_fwd), axis=0, keepdims=True)
                + jnp.sum(jnp.abs(h_bwd), axis=0, keepdims=True) + FILTER_EPS)
        o_ref[2 * o] = h_fwd / norm
        o_ref[2 * o + 1] = h_bwd / norm


def _hyena_filter(length, f_w1, f_b1, f_freq, f_w2, f_b2, f_w3, d):
    emb, hid = f_w1.shape
    orders = f_w3.shape[1] // (2 * d)
    t = jnp.linspace(0.0, 1.0, length, dtype=F32)[:, None]
    w = (2.0 * math.pi / length) * jnp.arange(length, dtype=F32)[:, None]
    bands = jnp.linspace(1e-4, FILTER_BANDS - 1, FILTER_BANDS, dtype=F32)[None, :]
    z = jnp.concatenate([t, jnp.cos(bands * w), -jnp.sin(bands * w)], axis=-1)
    emb_pad = V7X_LANES
    z = jnp.pad(z, ((0, 0), (0, emb_pad - emb)))
    w1 = jnp.pad(f_w1, ((0, emb_pad - emb), (0, 0)))
    deltas = jnp.abs(jnp.linspace(MIN_DECAY, MAX_DECAY, d, dtype=F32))[None, :]
    td = 256
    nd = d // td
    full = lambda shape: pl.BlockSpec(shape, lambda j: (0,) * len(shape))
    w3_spec = lambda side, o: pl.BlockSpec((hid, td), lambda j: (0, (side * orders + o) * nd + j))
    assert orders == 2
    return pl.pallas_call(
        functools.partial(_filter_kernel, orders=orders),
        out_shape=jax.ShapeDtypeStruct((2 * orders, length, d), F32),
        grid=(nd,),
        in_specs=[full((length, emb_pad)), full((emb_pad, hid)), full((1, hid)), full((1, hid)),
                  full((hid, hid)), full((1, hid)),
                  w3_spec(0, 0), w3_spec(0, 1), w3_spec(1, 0), w3_spec(1, 1),
                  full((length, 1)), pl.BlockSpec((1, td), lambda j: (0, j))],
        out_specs=pl.BlockSpec((2 * orders, length, td), lambda j: (0, 0, j)),
        scratch_shapes=[pltpu.VMEM((length, hid), F32)],
        compiler_params=_params(("arbitrary",), 48),
        name="hyena_filter",
    )(z, w1, f_b1.reshape(1, hid), f_freq.reshape(1, hid), f_w2, f_b2.reshape(1, hid),
      f_w3, f_w3, f_w3, f_w3, t, deltas)


def _dft_matrices(length):
    n = 2 * length
    f0n, f1n = DFT_SPLIT, length // DFT_SPLIT
    tt = np.arange(length, dtype=np.int64)[None, :]
    ang_a = 2.0 * np.pi * ((DFT_SPLIT * np.arange(f1n, dtype=np.int64)[:, None] * tt) % n) / n
    ang_b = 2.0 * np.pi * ((np.arange(f0n, dtype=np.int64)[:, None] * tt) % n) / n
    ca, sa = (jnp.asarray(f(ang_a), F32)[:, None, :] for f in (np.cos, np.sin))
    cb, sb = (jnp.asarray(f(ang_b), F32)[None, :, :] for f in (np.cos, np.sin))
    cos_m = (ca * cb - sa * sb).reshape(length, length)
    sin_m = (sa * cb + ca * sb).reshape(length, length)
    nyq = jnp.asarray(1.0 - 2.0 * (np.arange(length) % 2), F32)[None, :]
    f_is0 = (jnp.arange(length) == 0)[:, None]
    fwd = jnp.stack([cos_m, jnp.where(f_is0, nyq, -sin_m)], axis=0)
    cat, sat = (jnp.asarray(f(ang_a).T, F32)[:, :, None] for f in (np.cos, np.sin))
    cbt, sbt = (jnp.asarray(f(ang_b).T, F32)[:, None, :] for f in (np.cos, np.sin))
    f_is0_t = f_is0.T
    scale = jnp.where(f_is0_t, 1.0 / n, 2.0 / n)
    cos_t = (cat * cbt - sat * sbt).reshape(length, length) * scale
    im_t = jnp.where(f_is0_t, nyq.T, -(sat * cbt + cat * sbt).reshape(length, length)) * scale
    inv = jnp.concatenate([cos_t, im_t], axis=1)
    return fwd.astype(BF16), inv.astype(BF16)


def _spec_kernel(a_ref, hf_ref, hb_ref, o_ref, sum_scr, dif_scr, nyq_scr):
    m = pl.program_id(2)

    @pl.when(m == 0)
    def _():
        hf, hb = hf_ref[...], hb_ref[...]
        sum_scr[...] = (hf + hb).astype(BF16)
        dif_scr[...] = (hf - hb).astype(BF16)
        t = lax.broadcasted_iota(jnp.int32, (hb.shape[0], 1), 0)
        sign = (1 - 2 * (t & 1)).astype(F32)
        nyq_scr[...] = 2.0 * jnp.sum(sign * hb, axis=0, keepdims=True)

    o_ref[0] = _dot(a_ref[0], sum_scr[...])
    im = _dot(a_ref[1], dif_scr[...])
    row = lax.broadcasted_iota(jnp.int32, (im.shape[0], 1), 0)
    o_ref[1] = jnp.where((row == 0) & (m == 0), im + nyq_scr[...], im)


def _filter_spectrum(fwd, filt):
    n2, length, d = filt.shape
    orders = n2 // 2
    th, tc = min(DFT_FREQS, length), 512
    filt2 = filt.reshape(n2 * length, d)
    return pl.pallas_call(
        _spec_kernel,
        out_shape=jax.ShapeDtypeStruct((orders, 2, length, d), F32),
        grid=(orders, d // tc, length // th),
        in_specs=[pl.BlockSpec((2, th, length), lambda o, c, m: (0, m, 0)),
                  pl.BlockSpec((length, tc), lambda o, c, m: (2 * o, c)),
                  pl.BlockSpec((length, tc), lambda o, c, m: (2 * o + 1, c))],
        out_specs=pl.BlockSpec((None, 2, th, tc), lambda o, c, m: (o, 0, m, c)),
        scratch_shapes=[pltpu.VMEM((length, tc), BF16), pltpu.VMEM((length, tc), BF16),
                        pltpu.VMEM((1, tc), F32)],
        compiler_params=_params(("parallel", "parallel", "arbitrary"), 48),
        name="hyena_filter_spectrum",
    )(fwd, filt2, filt2)


def _dft_fwd_kernel(a_ref, z_ref, k_ref, o_ref, z_scr):
    m = pl.program_id(1)

    @pl.when(m == 0)
    def _():
        z_scr[...] = z_ref[...].astype(BF16)

    xr = _dot(a_ref[0], z_scr[...])
    xi = _dot(a_ref[1], z_scr[...])
    kr, ki = k_ref[0], k_ref[1]
    row = lax.broadcasted_iota(jnp.int32, (xr.shape[0], 1), 0)
    dc = (row == 0) & (m == 0)
    o_ref[0] = (xr * kr - jnp.where(dc, 0.0, xi * ki)).astype(o_ref.dtype)
    o_ref[1] = jnp.where(dc, xi * ki, xr * ki + xi * kr).astype(o_ref.dtype)


def _dft_forward(fwd, src, spec, *, nb, length, row_off, col_blk, order):
    d = spec.shape[3]
    th = min(DFT_FREQS, length)
    return pl.pallas_call(
        _dft_fwd_kernel,
        out_shape=jax.ShapeDtypeStruct((nb, 2, length, d), BF16),
        grid=(nb, length // th),
        in_specs=[pl.BlockSpec((2, th, length), lambda b, m: (0, m, 0)),
                  pl.BlockSpec((length, d), lambda b, m: (row_off + b, col_blk)),
                  pl.BlockSpec((None, 2, th, d), lambda b, m: (order, 0, m, 0))],
        out_specs=pl.BlockSpec((None, 2, th, d), lambda b, m: (b, 0, m, 0)),
        scratch_shapes=[pltpu.VMEM((length, d), BF16)],
        compiler_params=_params(("parallel", "arbitrary"), 56),
        name="hyena_dft_forward",
    )(fwd, src, spec)


def _dft_inv_kernel(a_ref, p_ref, v_ref, g_ref, s_ref, o_ref):
    y = _dot(a_ref[...], p_ref[...])
    v = v_ref[...]
    o_ref[...] = (g_ref[...] * (y + v * s_ref[...])).astype(o_ref.dtype)


def _dft_inverse(inv, prod, vsrc, gsrc, skip, *, nb, length, v_off, v_col, g_off, g_col,
                 out_dtype):
    n = 2 * length
    d = prod.shape[3]
    prod = prod.reshape(nb * n, d)
    tm = min(DFT_ROWS, length)
    nt = length // tm
    return pl.pallas_call(
        _dft_inv_kernel,
        out_shape=jax.ShapeDtypeStruct((nb * length, d), out_dtype),
        grid=(nb, nt),
        in_specs=[pl.BlockSpec((tm, n), lambda b, m: (m, 0)),
                  pl.BlockSpec((n, d), lambda b, m: (b, 0)),
                  pl.BlockSpec((tm, d), lambda b, m: ((v_off + b) * nt + m, v_col)),
                  pl.BlockSpec((tm, d), lambda b, m: ((g_off + b) * nt + m, g_col)),
                  pl.BlockSpec((1, d), lambda b, m: (0, 0))],
        out_specs=pl.BlockSpec((tm, d), lambda b, m: (b * nt + m, 0)),
        compiler_params=_params(("parallel", "arbitrary"), 48),
        name="hyena_dft_inverse",
    )(inv, prod, vsrc, gsrc, skip.reshape(1, d))


def _spectral_product(acc, k_ref, order):
    th = acc.shape[0] // 2
    xr, xi = acc[:th], acc[th:]
    kr, ki = k_ref[order, 0:th, :], k_ref[order, th:, :]
    dc = lax.broadcasted_iota(jnp.int32, (th, 1), 0) == 0
    pr = xr * kr - jnp.where(dc, 0.0, xi * ki)
    pi = jnp.where(dc, xi * ki, xr * ki + xi * kr)
    return jnp.concatenate([pr, pi], axis=0).astype(BF16)


def _hyena_short_kernel(u_ref, f_ref, i_ref, k_ref, s_ref, o_ref):
    d = o_ref.shape[1]
    fwd, inv = f_ref[...], i_ref[...]
    z = u_ref[:, 0:d]
    for order in range(2):
        p = _spectral_product(_dot(fwd, z.astype(BF16)), k_ref, order)
        gate = u_ref[:, (order + 1) * d:(order + 2) * d]
        z = gate * (_dot(inv, p) + z * s_ref[order:order + 1, :])
    o_ref[...] = z.astype(o_ref.dtype)


def _hyena_short(u, fwd, inv, spec, f_skip, *, nb, length):
    n = 2 * length
    d = spec.shape[3]
    assert n == DFT_ROWS and spec.shape[:3] == (2, 2, length)
    fwd = fwd.reshape(n, length)
    return pl.pallas_call(
        _hyena_short_kernel,
        out_shape=jax.ShapeDtypeStruct((nb * length, d), BF16),
        grid=(nb,),
        in_specs=[pl.BlockSpec((length, 3 * d), lambda b: (b, 0)),
                  pl.BlockSpec((n, length), lambda b: (0, 0)),
                  pl.BlockSpec((length, n), lambda b: (0, 0)),
                  pl.BlockSpec((2, n, d), lambda b: (0, 0, 0)),
                  pl.BlockSpec((2, d), lambda b: (0, 0))],
        out_specs=pl.BlockSpec((length, d), lambda b: (b, 0)),
        compiler_params=_params(("parallel",), 48),
        name="hyena_short_sequences",
    )(u, fwd, inv, spec.reshape(2, n, d), f_skip)


def _hyena_stream(u, row_off, nb, length, filt_w, f_skip, d):
    fwd, inv = _dft_matrices(length)
    filt = _hyena_filter(length, *filt_w, d)
    spec = _filter_spectrum(fwd, filt)
    if 2 * length == DFT_ROWS:
        assert row_off == 0
        return _hyena_short(u, fwd, inv, spec, f_skip, nb=nb, length=length)
    p1 = _dft_forward(fwd, u, spec, nb=nb, length=length, row_off=row_off, col_blk=0, order=0)
    z1 = _dft_inverse(inv, p1, u, u, f_skip[0], nb=nb, length=length, v_off=row_off, v_col=0,
                      g_off=row_off, g_col=1, out_dtype=F32)
    p2 = _dft_forward(fwd, z1, spec, nb=nb, length=length, row_off=0, col_blk=0, order=1)
    return _dft_inverse(inv, p2, z1, u, f_skip[1], nb=nb, length=length, v_off=0, v_col=0,
                        g_off=row_off, g_col=2, out_dtype=BF16)


def _gelu_tanh(x):
    return 0.5 * x * (1.0 + jnp.tanh(math.sqrt(2.0 / math.pi) * (x + 0.044715 * (x * x * x))))


def _log1p(e):
    u = 1.0 + e
    d = u - 1.0
    return jnp.where(d == 0.0, e, jnp.log(u) * (e / jnp.where(d == 0.0, 1.0, d)))


def _rglru_kernel(gate_ref, rec_ref, wa_ref, wx_ref, ba_ref, bx_ref, lam_ref, h0_ref,
                  y_ref, st_ref, a_scr, b_scr):
    length, cols = rec_ref.shape
    groups = length // V7X_SUBLANES
    rec = rec_ref[...]
    rec16 = rec.astype(BF16)
    pos = lax.broadcasted_iota(jnp.int32, (groups, V7X_SUBLANES, cols), 1)

    for d in range(2):
        r_gate = _sigmoid_tanh(_dot(rec16, wa_ref[d]) + ba_ref[d])
        i_gate = _sigmoid_tanh(_dot(rec16, wx_ref[d]) + bx_ref[d])
        nlam = -lam_ref[d]
        softplus = jnp.maximum(nlam, 0.0) + _log1p(jnp.exp(-jnp.abs(nlam)))
        log_a = (-RG_C * softplus) * r_gate
        a = jnp.exp(log_a).reshape(groups, V7X_SUBLANES, cols)
        th = jnp.tanh(log_a)
        b = (jnp.sqrt(-2.0 * th / (1.0 - th)) * (i_gate * rec)).reshape(
            groups, V7X_SUBLANES, cols)
        for s in (1, 2, 4):
            if d == 0:
                a_sh = pltpu.roll(a, s, axis=1)
                b_sh = pltpu.roll(b, s, axis=1)
                live = pos >= s
            else:
                a_sh = pltpu.roll(a, V7X_SUBLANES - s, axis=1)
                b_sh = pltpu.roll(b, V7X_SUBLANES - s, axis=1)
                live = pos < V7X_SUBLANES - s
            b = jnp.where(live, a * b_sh, 0.0) + b
            a = jnp.where(live, a * a_sh, a)
        a_scr[d] = a.reshape(length, cols)
        b_scr[d] = b.reshape(length, cols)

    seqs = h0_ref.shape[0]
    seq_groups = groups // seqs

    def step(g, carry):
        out = []
        for q in range(seqs):
            cf, cb = carry[2 * q], carry[2 * q + 1]
            rf = pl.multiple_of((q * seq_groups + g) * V7X_SUBLANES, V7X_SUBLANES)
            rb = pl.multiple_of(((q + 1) * seq_groups - 1 - g) * V7X_SUBLANES, V7X_SUBLANES)
            hf = a_scr[0, pl.ds(rf, V7X_SUBLANES), :] * cf + b_scr[0, pl.ds(rf, V7X_SUBLANES), :]
            hb = a_scr[1, pl.ds(rb, V7X_SUBLANES), :] * cb + b_scr[1, pl.ds(rb, V7X_SUBLANES), :]
            b_scr[0, pl.ds(rf, V7X_SUBLANES), :] = hf
            b_scr[1, pl.ds(rb, V7X_SUBLANES), :] = hb
            out.append(jnp.broadcast_to(hf[V7X_SUBLANES - 1:V7X_SUBLANES, :], hf.shape))
            out.append(jnp.broadcast_to(hb[0:1, :], hb.shape))
        return tuple(out)

    init = tuple(jnp.broadcast_to(h0_ref[q, d], (V7X_SUBLANES, cols))
                 for q in range(seqs) for d in range(2))
    final = lax.fori_loop(0, seq_groups, step, init, unroll=max(1, 4 // seqs))
    for q in range(seqs):
        for d in range(2):
            st_ref[q, d] = final[2 * q + d][0:1, :]
    y_ref[...] = ((b_scr[0] + b_scr[1]) * _gelu_tanh(gate_ref[...])).astype(y_ref.dtype)


def _rglru_stream(u, h0, row_off, nb, length, wa, wx, ba, bx, lam):
    dr = u.shape[1] // 2
    tc = RG_COLS
    nc = dr // tc
    seqs = max(1, min(nb, RG_ROWS // length))
    assert nb % seqs == 0 and (row_off * length) % (seqs * length) == 0
    rows = seqs * length
    blk_off = row_off // seqs
    vec = lambda a: a.reshape(2, 1, dr)
    vec_spec = pl.BlockSpec((2, 1, tc), lambda b, c: (0, 0, c))
    w_spec = pl.BlockSpec((2, None, tc, tc), lambda b, c: (0, c, 0, 0))
    st_spec = pl.BlockSpec((seqs, 2, 1, tc), lambda b, c: (b, 0, 0, c))
    y, st = pl.pallas_call(
        _rglru_kernel,
        out_shape=(jax.ShapeDtypeStruct((nb * length, dr), BF16),
                   jax.ShapeDtypeStruct((nb, 2, 1, dr), F32)),
        grid=(nb // seqs, nc),
        in_specs=[pl.BlockSpec((rows, tc), lambda b, c: (blk_off + b, c)),
                  pl.BlockSpec((rows, tc), lambda b, c: (blk_off + b, nc + c)),
                  w_spec, w_spec, vec_spec, vec_spec, vec_spec, st_spec],
        out_specs=(pl.BlockSpec((rows, tc), lambda b, c: (b, c)), st_spec),
        scratch_shapes=[pltpu.VMEM((2, rows, tc), F32), pltpu.VMEM((2, rows, tc), F32)],
        compiler_params=_params(("parallel", "parallel"), 48),
        name="rglru_scan",
    )(u, u, wa, wx, vec(ba), vec(bx), vec(lam), h0.reshape(nb, 2, 1, dr))
    return y, st.reshape(nb, 2, dr)


def _router_kernel(yc_ref, yl_ref, w_ref, b_ref, x_ref, mod_ref, g_ref, rwt_ref, bias_ref,
                   xo_ref, h_ref, idx_ref, gate_ref, rank_ref, cnt_ref, tri_scr, carry_scr, *,
                   n_experts, n_ctx_tiles):
    i = pl.program_id(0)
    tm = x_ref.shape[0]
    per_group = n_experts // N_GROUPS

    @pl.when(i == 0)
    def _():
        r = lax.broadcasted_iota(jnp.int32, (tm, tm), 0)
        c = lax.broadcasted_iota(jnp.int32, (tm, tm), 1)
        tri_scr[...] = jnp.where(r < c, 1.0, 0.0).astype(BF16)
        carry_scr[...] = jnp.zeros_like(carry_scr)

    y = jnp.where(i < n_ctx_tiles, yc_ref[...], yl_ref[...])
    x = x_ref[...] + mod_ref[2:3, :] * (_dot(y, w_ref[...]) + b_ref[...])
    xo_ref[...] = x
    h = _norm_mod(x, g_ref[...], mod_ref[3:4, :], mod_ref[4:5, :])
    h_ref[...] = _pack_rows(h)
    logits = _dot3_nt(rwt_ref[...], h)
    p = jnp.exp(logits - jnp.max(logits, axis=0, keepdims=True))
    scores = p / jnp.sum(p, axis=0, keepdims=True)
    sel = scores + bias_ref[...]
    rows = [sel[e:e + 1, :] for e in range(n_experts)]

    best_val = None
    for gi in range(N_GROUPS):
        v = rows[gi * per_group:(gi + 1) * per_group]
        pair = None
        for a in range(per_group):
            for b in range(a + 1, per_group):
                s = v[a] + v[b]
                pair = s if pair is None else jnp.maximum(pair, s)
        if best_val is None:
            best_val, best_grp = pair, jnp.zeros_like(pair, dtype=jnp.int32)
        else:
            take = pair > best_val
            best_val = jnp.where(take, pair, best_val)
            best_grp = jnp.where(take, gi, best_grp)

    neg = jnp.float32(-jnp.inf)
    masked = [jnp.where(best_grp == e // per_group, rows[e], neg) for e in range(n_experts)]

    def argmax_first(vals):
        bv, bi = vals[0], jnp.zeros_like(best_grp)
        for e in range(1, n_experts):
            take = vals[e] > bv
            bv = jnp.where(take, vals[e], bv)
            bi = jnp.where(take, e, bi)
        return bi

    idx0 = argmax_first(masked)
    idx1 = argmax_first([jnp.where(idx0 == e, neg, masked[e]) for e in range(n_experts)])

    e_iota = lax.broadcasted_iota(jnp.int32, (n_experts, tm), 0)
    hit0 = e_iota == idx0
    hit1 = e_iota == idx1
    g0 = jnp.sum(jnp.where(hit0, scores, 0.0), axis=0, keepdims=True)
    g1 = jnp.sum(jnp.where(hit1, scores, 0.0), axis=0, keepdims=True)
    gsum = g0 + g1
    onehot = jnp.where(hit0 | hit1, 1.0, 0.0)
    before = _dot(onehot.astype(BF16), tri_scr[...]) + carry_scr[:, 0:1]
    r0 = jnp.sum(jnp.where(hit0, before, 0.0), axis=0, keepdims=True)
    r1 = jnp.sum(jnp.where(hit1, before, 0.0), axis=0, keepdims=True)
    idx_ref[0:1, :] = idx0
    idx_ref[1:2, :] = idx1
    gate_ref[0:1, :] = g0 / gsum
    gate_ref[1:2, :] = g1 / gsum
    rank_ref[0:1, :] = r0.astype(jnp.int32)
    rank_ref[1:2, :] = r1.astype(jnp.int32)
    carry_scr[...] = carry_scr[...] + jnp.sum(onehot, axis=1, keepdims=True)
    cnt_ref[...] = carry_scr[...]


def _outproj_router(y_ctx, y_lat, w_out, b_out, x, mod, g, router_w, router_bias, *,
                    n_ctx_tiles):
    t, d = x.shape
    ne = router_w.shape[1]
    tm = TOKEN_TILE
    lat_tiles = (t // tm - n_ctx_tiles) // (mod.shape[0] - 1)

    def cond_of(i):
        return jnp.where(i < n_ctx_tiles, 0, 1 + (i - n_ctx_tiles) // lat_tiles)

    row2 = pl.BlockSpec((TOP_K, tm), lambda i: (0, i))
    full = lambda shape: pl.BlockSpec(shape, lambda i: (0,) * len(shape))
    return pl.pallas_call(
        functools.partial(_router_kernel, n_experts=ne, n_ctx_tiles=n_ctx_tiles),
        out_shape=(jax.ShapeDtypeStruct((t, d), F32),
                   jax.ShapeDtypeStruct((t, d // 2), jnp.uint32),
                   jax.ShapeDtypeStruct((TOP_K, t), jnp.int32),
                   jax.ShapeDtypeStruct((TOP_K, t), F32),
                   jax.ShapeDtypeStruct((TOP_K, t), jnp.int32),
                   jax.ShapeDtypeStruct((ne, V7X_LANES), F32)),
        grid=(t // tm,),
        in_specs=[pl.BlockSpec((tm, w_out.shape[0]),
                               lambda i: (jnp.minimum(i, n_ctx_tiles - 1), 0)),
                  pl.BlockSpec((tm, w_out.shape[0]),
                               lambda i: (jnp.maximum(i - n_ctx_tiles, 0), 0)),
                  full(w_out.shape), full((1, d)),
                  pl.BlockSpec((tm, d), lambda i: (i, 0)),
                  pl.BlockSpec((None, 6, d), lambda i: (cond_of(i), 0, 0)),
                  full((1, d)), full((ne, d)), full((ne, 1))],
        out_specs=(pl.BlockSpec((tm, d), lambda i: (i, 0)),
                   pl.BlockSpec((tm, d // 2), lambda i: (i, 0)), row2, row2, row2,
                   full((ne, V7X_LANES))),
        scratch_shapes=[pltpu.VMEM((tm, tm), BF16), pltpu.VMEM((ne, V7X_LANES), F32)],
        input_output_aliases={4: 0},
        compiler_params=_params(("arbitrary",), 40),
        name="outproj_router",
    )(y_ctx, y_lat, w_out, b_out.reshape(1, d), x, mod, g.reshape(1, d), router_w.T,
      router_bias.reshape(ne, 1))


def _row_copy(src, src_row, dst, dst_row, sem):
    return pltpu.make_async_copy(src.at[pl.ds(src_row, 1)], dst.at[pl.ds(dst_row, 1)], sem)


def _dispatch_kernel(pad_row_ref, pad_len_ref, used_ref, dest_ref, h_ref, xs_out, zero_scr, sem,
                     pad_sem):
    tm = h_ref.shape[0]
    i = pl.program_id(0)
    n_experts = pad_row_ref.shape[0]
    zrows = zero_scr.shape[0]

    @pl.when(i == 0)
    def _():
        zero_scr[...] = jnp.zeros_like(zero_scr)

    def zero_copy(wanted, start, size):
        start = jnp.where(wanted, start, 0)
        if size >= V7X_SUBLANES:
            start = pl.multiple_of(start, V7X_SUBLANES)
        return wanted, pltpu.make_async_copy(zero_scr.at[pl.ds(0, size)],
                                             xs_out.at[pl.ds(start, size)], pad_sem)

    e = jnp.minimum(i, n_experts - 1)
    pad_len = jnp.where(i < n_experts, pad_len_ref[e], 0)
    pad_row = pad_row_ref[e]
    head = pad_len & (V7X_SUBLANES - 1)
    pad_copies = [zero_copy(r < head, pad_row + r, 1) for r in range(V7X_SUBLANES - 1)]
    size = zrows
    while size >= V7X_SUBLANES:
        start = pad_row + head + ((pad_len - head) & ~(2 * size - 1))
        pad_copies.append(zero_copy((pad_len & size) != 0, start, size))
        size //= 2
    spare = used_ref[0] + i
    has_spare = (i < n_experts) & (spare < xs_out.shape[0] // MOE_ROWS)
    for part in range(MOE_ROWS // zrows):
        pad_copies.append(zero_copy(has_spare, spare * MOE_ROWS + part * zrows, zrows))
    for wanted, copy in pad_copies:
        pl.when(wanted)(copy.start)

    def issue(g, c):
        for r in range(ROW_DMA_UNROLL):
            t = g * ROW_DMA_UNROLL + r
            for k in range(TOP_K):
                _row_copy(h_ref, t, xs_out, dest_ref[TOP_K * t + k], sem).start(priority=k)
        return c

    def drain(g, c):
        for _ in range(ROW_DMA_UNROLL * TOP_K):
            _row_copy(h_ref, 0, xs_out, 0, sem).wait()
        return c

    lax.fori_loop(0, tm // ROW_DMA_UNROLL, issue, 0)
    lax.fori_loop(0, tm // ROW_DMA_UNROLL, drain, 0)
    for wanted, copy in pad_copies:
        pl.when(wanted)(copy.wait)


def _dispatch(h, dest, pad_row, pad_len, n_used, n_slots):
    t, d = h.shape
    tm = TOKEN_TILE
    assert t // tm >= pad_row.shape[0] and n_slots % MOE_ROWS == 0
    return pl.pallas_call(
        _dispatch_kernel,
        out_shape=jax.ShapeDtypeStruct((n_slots, d), h.dtype),
        grid_spec=pltpu.PrefetchScalarGridSpec(
            num_scalar_prefetch=3,
            grid=(t // tm,),
            in_specs=[pl.BlockSpec((TOP_K * tm,), lambda i, pr, pn, nu: (i,),
                                   memory_space=pltpu.SMEM),
                      pl.BlockSpec((tm, d), lambda i, pr, pn, nu: (i, 0))],
            out_specs=pl.BlockSpec(memory_space=pl.ANY),
            scratch_shapes=[pltpu.VMEM((MOE_ROWS // 2, d), h.dtype),
                            pltpu.SemaphoreType.DMA, pltpu.SemaphoreType.DMA]),
        compiler_params=_params(("arbitrary",), 32),
        name="moe_dispatch",
    )(pad_row, pad_len, n_used, dest.reshape(-1), h)


def _experts_kernel(be_ref, first_ref, slot_ref, next_ref, nu_ref, xs_ref, w1_hbm, w3_hbm, w2_hbm,
                    ys_ref, wbuf, w1_s, w3_s, w2_s, sem, *, layer):
    i = pl.program_id(0)
    used = i < nu_ref[0]
    w_hbm = (w1_hbm, w3_hbm, w2_hbm)
    w_s = (w1_s, w3_s, w2_s)

    def weight_copies(expert, slot):
        return [pltpu.make_async_copy(w_hbm[m].at[layer, expert], wbuf.at[slot, m],
                                      sem.at[slot, m]) for m in range(3)]

    @pl.when(i == 0)
    def _():
        for copy in weight_copies(be_ref[0], slot_ref[0]):
            copy.start()

    @pl.when(used & (first_ref[i] != 0))
    def _():
        slot = slot_ref[i]
        for copy in weight_copies(be_ref[i], slot):
            copy.wait()
        for static_slot in range(2):

            @pl.when(slot == static_slot)
            def _():
                for m in range(3):
                    w_s[m][...] = wbuf[static_slot, m].astype(BF16)

        nxt = next_ref[i]

        @pl.when(nxt >= 0)
        def _():
            for copy in weight_copies(nxt, 1 - slot):
                copy.start()

    @pl.when(used)
    def _():
        x = _unpack_rows(xs_ref[...]).astype(BF16)
        h1 = _dot(x, w1_s[...])
        h3 = _dot(x, w3_s[...])
        act = (h1 * _sigmoid(h1)) * h3
        ys_ref[...] = _pack_rows(_dot(act.astype(BF16), w2_s[...]))

    @pl.when(i >= nu_ref[0])
    def _():
        ys_ref[...] = jnp.zeros_like(ys_ref)


def _experts(xs, block_e, pad_end, n_used, w1, w3, w2, layer):
    n_slots, dp = xs.shape
    d, de = w1.shape[2:]
    assert d == de == 2 * dp
    tm = MOE_ROWS
    n_blocks = n_slots // tm
    prev_e = jnp.concatenate([jnp.full((1,), -1, jnp.int32), block_e[:-1]])
    first = (block_e != prev_e).astype(jnp.int32)
    slot = (jnp.cumsum(first) - 1) & 1
    run_end = pad_end[block_e] // tm
    nxt = jnp.where(run_end < n_used[0], block_e[jnp.minimum(run_end, n_blocks - 1)], -1)
    any_spec = pl.BlockSpec(memory_space=pl.ANY)
    row_spec = lambda f: pl.BlockSpec((tm, dp), lambda i, be, fi, sl, nx, nu: (f(i, nu), 0))
    return pl.pallas_call(
        functools.partial(_experts_kernel, layer=layer),
        out_shape=jax.ShapeDtypeStruct((n_slots, dp), jnp.uint32),
        grid_spec=pltpu.PrefetchScalarGridSpec(
            num_scalar_prefetch=5,
            grid=(n_blocks,),
            in_specs=[row_spec(lambda i, nu: jnp.minimum(i, nu[0] - 1)),
                      any_spec, any_spec, any_spec],
            out_specs=row_spec(lambda i, nu: i),
            scratch_shapes=[pltpu.VMEM((2, 3, d, de), F32),
                            pltpu.VMEM((d, de), BF16), pltpu.VMEM((d, de), BF16),
                            pltpu.VMEM((de, d), BF16), pltpu.SemaphoreType.DMA((2, 3))]),
        compiler_params=_params(("arbitrary",), 52),
        name="moe_experts",
    )(block_e, first, slot.astype(jnp.int32), nxt.astype(jnp.int32), n_used, xs, w1, w3, w2)


def _combine_kernel(dest_ref, next_ref, x_ref, gate_ref, mod_ref, gf_ref, ys_hbm, *rest,
                    n_ctx_tiles, final_norm):
    buf, sem = rest[-2:]
    tm = x_ref.shape[0]
    i = pl.program_id(0)
    slot = i & 1

    def gather(idx_ref, into):
        def issue(g, c):
            for r in range(ROW_DMA_UNROLL):
                t = g * ROW_DMA_UNROLL + r
                for k in range(TOP_K):
                    _row_copy(ys_hbm, idx_ref[TOP_K * t + k], buf.at[into, k], t,
                              sem.at[into]).start(priority=k)
            return c
        lax.fori_loop(0, tm // ROW_DMA_UNROLL, issue, 0)

    pl.when(i == 0)(lambda: gather(dest_ref, slot))
    pl.when(i + 1 < pl.num_programs(0))(lambda: gather(next_ref, 1 - slot))

    def drain(g, c):
        for _ in range(ROW_DMA_UNROLL * TOP_K):
            _row_copy(ys_hbm, 0, buf.at[slot, 0], 0, sem.at[slot]).wait()
        return c

    lax.fori_loop(0, tm // ROW_DMA_UNROLL, drain, 0)
    gate = gate_ref[...]
    m = (gate[:, 0:1] * _unpack_rows(buf[slot, 0])
         + gate[:, 1:2] * _unpack_rows(buf[slot, 1]))
    x = x_ref[...] + mod_ref[5:6, :] * m
    if not final_norm:
        rest[0][...] = x
        return
    ms = jnp.mean(x * x, axis=-1, keepdims=True)
    x = x * lax.rsqrt(ms + NORM_EPS) * gf_ref[...]
    @pl.when(i < n_ctx_tiles)
    def _():
        rest[0][...] = x

    @pl.when(i >= n_ctx_tiles)
    def _():
        rest[1][...] = x


def _combine(x, ys, dest, gates, mod, g_final, *, n_ctx_tiles, final_norm):
    t, d = x.shape
    tm = TOKEN_TILE
    lat_tiles = (t // tm - n_ctx_tiles) // (mod.shape[0] - 1)

    def cond_of(i):
        return jnp.where(i < n_ctx_tiles, 0, 1 + (i - n_ctx_tiles) // lat_tiles)

    if final_norm:
        out_shape = (jax.ShapeDtypeStruct((n_ctx_tiles * tm, d), F32),
                     jax.ShapeDtypeStruct((t - n_ctx_tiles * tm, d), F32))
        out_specs = (pl.BlockSpec((tm, d), lambda i: (jnp.minimum(i, n_ctx_tiles - 1), 0)),
                     pl.BlockSpec((tm, d), lambda i: (jnp.maximum(i - n_ctx_tiles, 0), 0)))
        aliases = {}
    else:
        out_shape = jax.ShapeDtypeStruct((t, d), F32)
        out_specs = pl.BlockSpec((tm, d), lambda i: (i, 0))
        aliases = {2: 0}
    n_tiles = t // tm
    return pl.pallas_call(
        functools.partial(_combine_kernel, n_ctx_tiles=n_ctx_tiles, final_norm=final_norm),
        out_shape=out_shape,
        grid=(n_tiles,),
        in_specs=[pl.BlockSpec((TOP_K * tm,), lambda i: (i,), memory_space=pltpu.SMEM),
                  pl.BlockSpec((TOP_K * tm,), lambda i: (jnp.minimum(i + 1, n_tiles - 1),),
                               memory_space=pltpu.SMEM),
                  pl.BlockSpec((tm, d), lambda i: (i, 0)),
                  pl.BlockSpec((tm, TOP_K), lambda i: (i, 0)),
                  pl.BlockSpec((None, 6, d), lambda i: (cond_of(i), 0, 0)),
                  pl.BlockSpec((1, d), lambda i: (0, 0)),
                  pl.BlockSpec(memory_space=pl.ANY)],
        out_specs=out_specs,
        scratch_shapes=[pltpu.VMEM((2, TOP_K, tm, ys.shape[1]), ys.dtype),
                        pltpu.SemaphoreType.DMA((2,))],
        input_output_aliases=aliases,
        compiler_params=_params(("arbitrary",), 40),
        name="moe_combine",
    )(dest.reshape(-1), dest.reshape(-1), x, gates, mod, g_final.reshape(1, d), ys)


def _outproj_moe(y_ctx, y_lat, w_out, b_out, x, mod, g, router_w, router_bias, w1, w3, w2, layer,
                 g_final, *, n_ctx_tiles, final_norm):
    t, d = x.shape
    ne = router_w.shape[1]
    x, h, idx, gates, rank, cnt = _outproj_router(y_ctx, y_lat, w_out, b_out, x, mod, g, router_w,
                                                  router_bias, n_ctx_tiles=n_ctx_tiles)
    counts = cnt[:, 0].astype(jnp.int32)
    padded = (counts + MOE_ROWS - 1) // MOE_ROWS * MOE_ROWS
    pad_end = jnp.cumsum(padded)
    pad_start = pad_end - padded
    e_ids = jnp.arange(ne, dtype=jnp.int32)
    start_of = jnp.sum(jnp.where(idx[None] == e_ids[:, None, None],
                                 pad_start[:, None, None], 0), axis=0)
    dest = (start_of + rank).T
    n_blocks = -(-(t * TOP_K) // MOE_ROWS) + ne
    block_start = jnp.arange(n_blocks, dtype=jnp.int32) * MOE_ROWS
    block_e = jnp.minimum(jnp.sum(block_start[:, None] >= pad_end[None, :], axis=1),
                          ne - 1).astype(jnp.int32)
    n_used = (pad_end[-1:] // MOE_ROWS).astype(jnp.int32)
    xs = _dispatch(h, dest, pad_start + counts, padded - counts, n_used, n_blocks * MOE_ROWS)
    ys = _experts(xs, block_e, pad_end, n_used, w1, w3, w2, layer)
    return _combine(x, ys, dest, gates.T, mod, g_final, n_ctx_tiles=n_ctx_tiles,
                    final_norm=final_norm)


def kernel(x_prompt, x_sample, state_rglru, c, c_ctx, ada_w, ada_b, norm_mix, norm_moe, norm_final, hy_w_in, hy_b_in, hy_conv_w, hy_conv_b, hy_f_w1, hy_f_b1, hy_f_freq, hy_f_w2, hy_f_b2, hy_f_w3, hy_f_skip, hy_w_out, hy_b_out, rg_w_in, rg_b_in, rg_conv_w, rg_conv_b, rg_wa, rg_ba, rg_wx, rg_bx, rg_lambda, rg_w_out, rg_b_out, router_w, router_bias, moe_w1, moe_w3, moe_w2):
    nb_ctx, len_ctx, d = x_prompt.shape
    nb_lat, len_lat, _ = x_sample.shape
    depth = ada_w.shape[0]
    n_rg = rg_w_in.shape[0]
    d_rnn = rg_w_out.shape[1]
    tok_ctx = nb_ctx * len_ctx
    tiles_ctx = tok_ctx // TOKEN_TILE
    assert tok_ctx % TOKEN_TILE == 0 and TOKEN_TILE % len_ctx == 0 and TOKEN_TILE % GRID_W == 0
    assert len_lat % TOKEN_TILE == 0 and tok_ctx % len_lat == 0
    assert V7X_SUBLANES - (nb_lat + 1) >= 0
    lat_off = tok_ctx // len_lat

    x = jnp.concatenate([x_prompt.reshape(tok_ctx, d), x_sample.reshape(nb_lat * len_lat, d)], 0)
    cond = jnp.concatenate([c_ctx[None, :], c,
                            jnp.zeros((V7X_SUBLANES - 1 - nb_lat, d), F32)], axis=0)
    mods = _ada_modulation(cond, ada_w, ada_b)[:, :1 + nb_lat]
    ctx_h0 = jnp.zeros((nb_ctx, 2, d_rnn), F32)
    states = []

    for i in range(depth):
        mod = mods[i]
        j = i // 2
        if i % 2 == 0:
            u = _inproj(x, mod, norm_mix[i], hy_w_in[j].astype(BF16), hy_b_in[j], hy_conv_w[j],
                        hy_conv_b[j], n_ctx_tiles=tiles_ctx, ctx_len=len_ctx, lat_len=GRID_W,
                        plain_cols=0)
            filt_w = (hy_f_w1[j], hy_f_b1[j], hy_f_freq[j], hy_f_w2[j], hy_f_b2[j], hy_f_w3[j])
            y_ctx = _hyena_stream(u, 0, nb_ctx, len_ctx, filt_w, hy_f_skip[j], d)
            y_lat = _hyena_stream(u, lat_off, nb_lat, len_lat, filt_w, hy_f_skip[j], d)
            w_out, b_out = hy_w_out[j].astype(BF16), hy_b_out[j]
        else:
            u = _inproj(x, mod, norm_mix[i], rg_w_in[j].astype(BF16), rg_b_in[j], rg_conv_w[j],
                        rg_conv_b[j], n_ctx_tiles=tiles_ctx, ctx_len=len_ctx, lat_len=GRID_W,
                        plain_cols=d_rnn)
            wa, wx = rg_wa[j].astype(BF16), rg_wx[j].astype(BF16)
            y_ctx, st = _rglru_stream(u, ctx_h0, 0, nb_ctx, len_ctx, wa, wx, rg_ba[j], rg_bx[j],
                                      rg_lambda[j])
            y_lat, _ = _rglru_stream(u, state_rglru[:, j], lat_off, nb_lat, len_lat, wa, wx,
                                     rg_ba[j], rg_bx[j], rg_lambda[j])
            states.append(st)
            w_out, b_out = rg_w_out[j].astype(BF16), rg_b_out[j]
        x = _outproj_moe(y_ctx, y_lat, w_out, b_out, x, mod, norm_moe[i], router_w, router_bias,
                         moe_w1, moe_w3, moe_w2, i, norm_final, n_ctx_tiles=tiles_ctx,
                         final_norm=(i == depth - 1))

    y_prompt = x[0].reshape(nb_ctx, len_ctx, d)
    y_sample = x[1].reshape(nb_lat, len_lat, d)
    new_state = jnp.stack(states, axis=1).astype(x_prompt.dtype)
    return (y_prompt, y_sample, new_state)
```
